```python
import math
import jax, jax.numpy as jnp
from jax import lax
import numpy as np

D_MODEL = 1024
BATCH = 16
SEQ = 2048
DEPTH = 2

N_META = 16
BLOCK = 128
PAD = BLOCK - N_META
HEAD_DIM = 64
A_HEADS = 8
A_KV_HEADS = 2
WINDOW = 128
B_HEADS = 4
B_QK_DIM = 32
B_V_DIM = 64
C_GROUPS = 4
C_GROUP_DIM = 64
POOL_WINDOWS = (2, 4, 8, 16)
N_BUCKETS = 32
MAX_DISTANCE = 128
N_BIAS_HEADS = A_HEADS + B_HEADS
D_FF = 2816
N_EXPERTS = 8
TOP_K = 2
D_FF_EXPERT = 3584
EPS = 1e-6
NEG = -1e30

A_Q = A_HEADS * HEAD_DIM
A_KV = A_KV_HEADS * HEAD_DIM
B_QK = B_HEADS * 2 * B_QK_DIM
B_V = B_HEADS * B_V_DIM
C_W = C_GROUPS * C_GROUP_DIM
D_IN = A_Q + 2 * A_KV + 2 * B_QK + B_V + C_W
D_MIX = A_Q + B_V + C_W
N_DENSE = (DEPTH + 1) // 2
N_MOE = DEPTH // 2

kernel_name = "hybrid_swa_diffattn_pool_moe"


def rmsnorm(x, g):
    xf = x.astype(jnp.float32)
    y = xf * lax.rsqrt(jnp.mean(xf * xf, axis=-1, keepdims=True) + EPS)
    return (y * g.astype(jnp.float32)).astype(x.dtype)


def t5_bucket(n):
    max_exact = N_BUCKETS // 2
    nf = jnp.maximum(n, 1).astype(jnp.float32)
    large = max_exact + (jnp.log(nf / max_exact) / math.log(MAX_DISTANCE / max_exact)
                         * (N_BUCKETS - max_exact)).astype(jnp.int32)
    large = jnp.minimum(large, N_BUCKETS - 1)
    return jnp.where(n < max_exact, n, large)


def pad_front(t):
    return jnp.pad(t, ((0, 0), (PAD, 0)) + ((0, 0),) * (t.ndim - 2))


def sliding_window_attention(q, k, v, sinks, table_a):
    B, Lp = q.shape[:2]
    nb = Lp // BLOCK
    g = A_HEADS // A_KV_HEADS
    qb = q.reshape(B, nb, BLOCK, A_KV_HEADS, g, HEAD_DIM)

    def band(t):
        tp = jnp.pad(t, ((0, 0), (BLOCK, 0), (0, 0), (0, 0)))
        tp = tp.reshape(B, nb + 1, BLOCK, A_KV_HEADS, HEAD_DIM)
        return jnp.concatenate([tp[:, :-1], tp[:, 1:]], axis=2)

    kb, vb = band(k), band(v)
    a_idx = jnp.arange(BLOCK)[:, None]
    b_idx = jnp.arange(2 * BLOCK)[None, :]
    n = BLOCK + a_idx - b_idx
    in_win = (n >= 0) & (n < WINDOW)
    blk = jnp.arange(nb)[:, None, None]
    key_ok = (blk - 1) * BLOCK + b_idx[None] >= PAD
    mask = in_win[None] & key_ok
    bias = table_a[t5_bucket(jnp.maximum(n, 0))]
    bias = jnp.transpose(bias, (2, 0, 1)).reshape(A_KV_HEADS, g, BLOCK, 2 * BLOCK)

    s = jnp.einsum('bnqhgd,bnkhd->bnhgqk', qb, kb).astype(jnp.float32) * (HEAD_DIM ** -0.5)
    s = s + bias[None, None].astype(jnp.float32)
    s = jnp.where(mask[None, :, None, None], s, NEG)
    sink = jnp.broadcast_to(sinks.astype(jnp.float32).reshape(1, 1, A_KV_HEADS, g, 1, 1),
                            s.shape[:-1] + (1,))
    p = jax.nn.softmax(jnp.concatenate([s, sink], axis=-1), axis=-1)[..., :-1]
    o = jnp.einsum('bnhgqk,bnkhd->bnqhgd', p.astype(v.dtype), vb)
    return o.reshape(B, Lp, A_Q)


def diff_attention(q, k, v, table_b, lam, g_out, lam_init):
    B, Lp = q.shape[:2]
    nb = Lp // BLOCK
    scale = B_QK_DIM ** -0.5
    outs = []
    for i in range(nb):
        q0, kend = i * BLOCK, (i + 1) * BLOCK
        qi, kk, vv = q[:, q0:kend], k[:, :kend], v[:, :kend]
        s = jnp.einsum('bqhcd,bkhcd->bhcqk', qi, kk).astype(jnp.float32) * scale
        qpos = q0 + jnp.arange(BLOCK)
        kpos = jnp.arange(kend)
        n = qpos[:, None] - kpos[None, :]
        bias = table_b[t5_bucket(jnp.maximum(n, 0))]
        s = s + jnp.transpose(bias, (2, 0, 1))[None, :, None].astype(jnp.float32)
        mask = (n >= 0) & (kpos[None, :] >= PAD)
        s = jnp.where(mask, s, NEG)
        p = jax.nn.softmax(s, axis=-1)
        a = p[:, :, 0] - lam * p[:, :, 1]
        outs.append(jnp.einsum('bhqk,bkhd->bqhd', a.astype(v.dtype), vv))
    o = jnp.concatenate(outs, axis=1)
    o = rmsnorm(o, g_out) * (1.0 - lam_init)
    return o.reshape(B, Lp, B_V)


def multiscale_pool(u, w_pool, pool_scale):
    B, L, _ = u.shape
    ug = u.reshape(B, L, C_GROUPS, C_GROUP_DIM)
    t = jnp.arange(L)
    pooled = []
    for gi, w in enumerate(POOL_WINDOWS):
        xg = ug[:, :, gi].astype(jnp.float32)
        cs = jnp.concatenate([jnp.zeros((B, 1, C_GROUP_DIM), jnp.float32),
                              jnp.cumsum(xg, axis=1)], axis=1)
        lo = jnp.maximum(t + 1 - w, 0)
        cnt = jnp.minimum(t + 1, w).astype(jnp.float32)
        mean = (cs[:, 1:] - cs[:, lo]) / cnt[None, :, None]
        pooled.append(mean - xg)
    d = jnp.stack(pooled, axis=2).astype(u.dtype)
    y = jnp.einsum('blgc,gcd->blgd', d, w_pool).reshape(B, L, C_W)
    return y * pool_scale


def swiglu(h, wg, wu, wd):
    return (jax.nn.silu(h @ wg) * (h @ wu)) @ wd


def moe_swiglu(h, w_router, w_gate, w_up, w_down):
    B, L, D = h.shape
    t = h.reshape(-1, D)
    logits = (t @ w_router).astype(jnp.float32)
    top_v, top_i = lax.top_k(logits, TOP_K)
    gates = jax.nn.softmax(top_v, axis=-1)
    combine = jnp.sum(jax.nn.one_hot(top_i, N_EXPERTS, dtype=jnp.float32) * gates[..., None], axis=1)
    out = jnp.zeros_like(t)
    for e in range(N_EXPERTS):
        ye = swiglu(t, w_gate[e], w_up[e], w_down[e])
        out = out + combine[:, e:e + 1].astype(t.dtype) * ye
    return out.reshape(B, L, D)


def setup_inputs(seed: int = 0) -> dict:
    key = jax.random.key(seed)
    ks = jax.random.split(key, 24)
    f32 = jnp.float32
    nrm = lambda k, s, sc: jax.random.normal(k, s, f32) * sc
    return {
        "x": nrm(ks[0], (BATCH, SEQ, D_MODEL), 1.0),
        "meta_tokens": nrm(ks[1], (N_META, D_MODEL), 1.0),
        "rel_bias_table": nrm(ks[2], (N_BUCKETS, N_BIAS_HEADS), 0.5),
        "g_mix": 1.0 + nrm(ks[3], (DEPTH, D_MODEL), 0.05),
        "w_in": nrm(ks[4], (DEPTH, D_MODEL, D_IN), D_MODEL ** -0.5),
        "qk_gain_a": 1.0 + nrm(ks[5], (DEPTH, 2, HEAD_DIM), 0.05),
        "sinks": nrm(ks[6], (DEPTH, A_HEADS), 0.5),
        "qk_gain_b": 1.0 + nrm(ks[7], (DEPTH, 2, B_QK_DIM), 0.05),
        "diff_lambda": nrm(ks[8], (DEPTH, 4, B_QK_DIM), 0.1),
        "g_diff_out": 1.0 + nrm(ks[9], (DEPTH, B_V_DIM), 0.05),
        "w_pool": nrm(ks[10], (DEPTH, C_GROUPS, C_GROUP_DIM, C_GROUP_DIM), C_GROUP_DIM ** -0.5),
        "pool_scale": 1.0 + nrm(ks[11], (DEPTH, C_W), 0.1),
        "w_out": nrm(ks[12], (DEPTH, D_MIX, D_MODEL), D_MIX ** -0.5),
        "g_ffn": 1.0 + nrm(ks[13], (DEPTH, D_MODEL), 0.05),
        "w_ffn_gate": nrm(ks[14], (N_DENSE, D_MODEL, D_FF), D_MODEL ** -0.5),
        "w_ffn_up": nrm(ks[15], (N_DENSE, D_MODEL, D_FF), D_MODEL ** -0.5),
        "w_ffn_down": nrm(ks[16], (N_DENSE, D_FF, D_MODEL), D_FF ** -0.5),
        "w_router": nrm(ks[17], (N_MOE, D_MODEL, N_EXPERTS), D_MODEL ** -0.5),
        "w_exp_gate": nrm(ks[18], (N_MOE, N_EXPERTS, D_MODEL, D_FF_EXPERT), D_MODEL ** -0.5),
        "w_exp_up": nrm(ks[19], (N_MOE, N_EXPERTS, D_MODEL, D_FF_EXPERT), D_MODEL ** -0.5),
        "w_exp_down": nrm(ks[20], (N_MOE, N_EXPERTS, D_FF_EXPERT, D_MODEL), D_FF_EXPERT ** -0.5),
    }


def reference(x, meta_tokens, rel_bias_table, g_mix, w_in, qk_gain_a, sinks, qk_gain_b,
              diff_lambda, g_diff_out, w_pool, pool_scale, w_out, g_ffn, w_ffn_gate,
              w_ffn_up, w_ffn_down, w_router, w_exp_gate, w_exp_up, w_exp_down):
    B = x.shape[0]
    meta = jnp.broadcast_to(meta_tokens[None].astype(x.dtype), (B, N_META, D_MODEL))
    h_res = jnp.concatenate([meta, x], axis=1)
    L = h_res.shape[1]
    table_a = rel_bias_table[:, :A_HEADS]
    table_b = rel_bias_table[:, A_HEADS:]
    splits = list(np.cumsum([A_Q, A_KV, A_KV, B_QK, B_QK, B_V]))
    for l in range(DEPTH):
        h = rmsnorm(h_res, g_mix[l])
        z = h @ w_in[l]
        qa, ka, va, qb, kb, vb, uc = jnp.split(z, splits, axis=-1)
        qa = rmsnorm(qa.reshape(B, L, A_HEADS, HEAD_DIM), qk_gain_a[l, 0])
        ka = rmsnorm(ka.reshape(B, L, A_KV_HEADS, HEAD_DIM), qk_gain_a[l, 1])
        va = va.reshape(B, L, A_KV_HEADS, HEAD_DIM)
        o_a = sliding_window_attention(pad_front(qa), pad_front(ka), pad_front(va),
                                       sinks[l], table_a)[:, PAD:]
        qb = rmsnorm(qb.reshape(B, L, B_HEADS, 2, B_QK_DIM), qk_gain_b[l, 0])
        kb = rmsnorm(kb.reshape(B, L, B_HEADS, 2, B_QK_DIM), qk_gain_b[l, 1])
        vb = vb.reshape(B, L, B_HEADS, B_V_DIM)
        lam_init = 0.8 - 0.6 * math.exp(-0.3 * l)
        dl = diff_lambda[l].astype(jnp.float32)
        lam = jnp.exp(jnp.sum(dl[0] * dl[1])) - jnp.exp(jnp.sum(dl[2] * dl[3])) + lam_init
        o_b = diff_attention(pad_front(qb), pad_front(kb), pad_front(vb), table_b, lam,
                             g_diff_out[l], lam_init)[:, PAD:]
        o_c = multiscale_pool(uc, w_pool[l], pool_scale[l])
        mix = jnp.concatenate([o_a.astype(h.dtype), o_b.astype(h.dtype), o_c.astype(h.dtype)], axis=-1)
        h_res = h_res + mix @ w_out[l]
        h = rmsnorm(h_res, g_ffn[l])
        if l % 2 == 0:
            j = l // 2
            f = swiglu(h, w_ffn_gate[j], w_ffn_up[j], w_ffn_down[j])
        else:
            j = l // 2
            f = moe_swiglu(h, w_router[j], w_exp_gate[j], w_exp_up[j], w_exp_down[j])
        h_res = h_res + f
    return h_res[:, N_META:]
```

```python
import functools
import math

import jax
import jax.numpy as jnp
import numpy as np
from jax import lax
from jax.experimental import pallas as pl
from jax.experimental.pallas import tpu as pltpu

f32 = jnp.float32
bf16 = jnp.bfloat16
i32 = jnp.int32

D = 1024
BLK = 128
N_META = 16
PAD = BLK - N_META
A_HEADS = 8
B_HEADS = 4
N_BUCKETS = 32
MAX_DISTANCE = 128
D_IN = 1792
N_EXPERTS = 8
D_FF_EXPERT = 3584
EPS = 1e-6
NEG = -1e30
LANES = 128
C_QA, C_KA, C_VA, C_QB, C_KB, C_VB, C_UC = 0, 4, 5, 6, 8, 10, 12
POOL_WINDOWS = (2, 4, 8, 16)

VMEM_LIMIT = 56 * 1024 * 1024


def _cparams(sem):
    return pltpu.CompilerParams(dimension_semantics=sem, vmem_limit_bytes=VMEM_LIMIT)


def _t5_bucket(n):
    max_exact = N_BUCKETS // 2
    nf = jnp.maximum(n, 1).astype(f32)
    large = max_exact + (jnp.log(nf / max_exact) / math.log(MAX_DISTANCE / max_exact)
                         * (N_BUCKETS - max_exact)).astype(i32)
    large = jnp.minimum(large, N_BUCKETS - 1)
    return jnp.where(n < max_exact, n, large)


def _dot(a, b):
    return jnp.dot(a, b, preferred_element_type=f32)


def _dot_nt(a, b):
    return lax.dot_general(a, b, (((1,), (1,)), ((), ())), preferred_element_type=f32)


def _swap_halves(x):
    return jnp.concatenate([x[:, 64:], x[:, :64]], axis=1)


def _embed_body(x_ref, m_ref, o_ref):
    i = pl.program_id(1)

    @pl.when(i == 0)
    def _():
        o_ref[0:PAD, :] = jnp.zeros((PAD, D), f32)
        o_ref[PAD:, :] = m_ref[...]

    @pl.when(i > 0)
    def _():
        o_ref[...] = x_ref[...]


def _embed(x, meta, B, nb):
    seq = x.shape[1]
    x2 = x.reshape(B * seq, D)
    return pl.pallas_call(
        _embed_body,
        grid=(B, nb),
        in_specs=[pl.BlockSpec((BLK, D), lambda b, i: (b * (nb - 1) + jnp.maximum(i - 1, 0), 0)),
                  pl.BlockSpec((N_META, D), lambda b, i: (0, 0))],
        out_specs=pl.BlockSpec((BLK, D), lambda b, i: (b * nb + i, 0)),
        out_shape=jax.ShapeDtypeStruct((B * nb * BLK, D), f32),
        compiler_params=_cparams(("parallel", "arbitrary")),
        name="embed",
    )(x2, meta)


def _inproj_body(h_ref, g_ref, w_ref, cs_ref, g64_ref, g32_ref, z_ref):
    x = h_ref[...]
    ms = jnp.mean(x * x, axis=-1, keepdims=True)
    xn = (x * lax.rsqrt(ms + EPS) * g_ref[...]).astype(bf16)
    for c2 in range(D_IN // 256):
        zc2 = _dot(xn, w_ref[:, c2 * 256:(c2 + 1) * 256])
        for half in range(2):
            c = 2 * c2 + half
            zc = zc2[:, half * LANES:(half + 1) * LANES]
            if c < C_VA:
                gm, inv = g64_ref, 1.0 / 64
            elif C_QB <= c < C_VB:
                gm, inv = g32_ref, 1.0 / 32
            else:
                gm = None
            if gm is not None:
                sq = zc * zc
                hi = sq.astype(bf16)
                lo = (sq - hi.astype(f32)).astype(bf16)
                ss = _dot(hi, gm[...]) + _dot(lo, gm[...])
                zc = zc * lax.rsqrt(ss * inv + EPS)
            z_ref[:, c * LANES:(c + 1) * LANES] = (zc * cs_ref[:, c * LANES:(c + 1) * LANES]).astype(bf16)


def _inproj(h, g, w, cs, g64, g32, tm):
    n = h.shape[0]
    const = lambda i: (0, 0)
    return pl.pallas_call(
        _inproj_body,
        grid=(n // tm,),
        in_specs=[pl.BlockSpec((tm, D), lambda i: (i, 0)),
                  pl.BlockSpec((1, D), const),
                  pl.BlockSpec((D, D_IN), const),
                  pl.BlockSpec((1, D_IN), const),
                  pl.BlockSpec((LANES, LANES), const),
                  pl.BlockSpec((LANES, LANES), const)],
        out_specs=pl.BlockSpec((tm, D_IN), lambda i: (i, 0)),
        out_shape=jax.ShapeDtypeStruct((n, D_IN), bf16),
        compiler_params=_cparams(("parallel",)),
        name="norm_inproj",
    )(h, g, w, cs, g64, g32)


def _swa_pool_body(q_ref, kp_ref, kc_ref, vp_ref, vc_ref, up_ref, uc_ref, bias_ref, sink_ref,
                   wp_ref, ps_ref, oa_ref, oc_ref):
    i = pl.program_id(1)
    lane = lax.broadcasted_iota(i32, (1, LANES), 1)
    lo = lane < 64
    k_cat = jnp.concatenate([kp_ref[...], kc_ref[...]], axis=0)
    v_cat = jnp.concatenate([vp_ref[...], vc_ref[...]], axis=0)
    k_sw = _swap_halves(k_cat)
    v_sw = _swap_halves(v_cat)
    zero = jnp.zeros((), bf16)
    for pb in range(4):
        qpb = q_ref[:, pb * LANES:(pb + 1) * LANES]
        grp = pb // 2
        outs = []
        for e in range(2):
            h = 2 * pb + e
            qm = jnp.where(lo if e == 0 else jnp.logical_not(lo), qpb, zero)
            kk = k_cat if e == grp else k_sw
            vv = v_cat if e == grp else v_sw
            s = _dot_nt(qm, kk) + bias_ref[h]
            sink = sink_ref[h]
            m = jnp.maximum(jnp.max(s, axis=-1, keepdims=True), sink)
            p = jnp.exp(s - m)
            l = jnp.sum(p, axis=-1, keepdims=True) + jnp.exp(sink - m)
            outs.append(_dot(p.astype(bf16), vv) * (1.0 / l))
        oa_ref[:, pb * LANES:(pb + 1) * LANES] = jnp.where(lo, outs[0], outs[1]).astype(bf16)

    row = lax.broadcasted_iota(i32, (BLK, 1), 0)
    t = i * BLK + row - PAD
    u_cur = jnp.where(t >= 0, uc_ref[...].astype(f32), 0.0)
    u_prev = jnp.where(i > 0, up_ref[BLK - 16:, :].astype(f32), 0.0)
    ext = jnp.concatenate([u_prev, u_cur], axis=0)
    s2 = ext + pltpu.roll(ext, 1, 0)
    s4 = s2 + pltpu.roll(s2, 2, 0)
    s8 = s4 + pltpu.roll(s4, 4, 0)
    s16 = s8 + pltpu.roll(s8, 8, 0)
    lane2 = lax.broadcasted_iota(i32, (1, 256), 1)
    grp2 = lane2 // 64
    sums = jnp.where(grp2 == 0, s2, jnp.where(grp2 == 1, s4, jnp.where(grp2 == 2, s8, s16)))[16:, :]
    win = jnp.where(grp2 == 0, 2, jnp.where(grp2 == 1, 4, jnp.where(grp2 == 2, 8, 16)))
    cnt = jnp.maximum(jnp.minimum(t + 1, win), 1).astype(f32)
    dlt = (sums / cnt - u_cur).astype(bf16)
    oc_ref[...] = (_dot(dlt, wp_ref[...]) * ps_ref[...]).astype(bf16)


def _swa_pool(z, bias_a, sinks, wpool_bd, pool_scale, B, nb):
    n = z.shape[0]
    cur = lambda c: (lambda b, i: (b * nb + i, c))
    prev = lambda c: (lambda b, i: (b * nb + jnp.maximum(i - 1, 0), c))
    return pl.pallas_call(
        _swa_pool_body,
        grid=(B, nb),
        in_specs=[pl.BlockSpec((BLK, 512), cur(0)),
                  pl.BlockSpec((BLK, LANES), prev(C_KA)),
                  pl.BlockSpec((BLK, LANES), cur(C_KA)),
                  pl.BlockSpec((BLK, LANES), prev(C_VA)),
                  pl.BlockSpec((BLK, LANES), cur(C_VA)),
                  pl.BlockSpec((BLK, 256), prev(C_UC // 2)),
                  pl.BlockSpec((BLK, 256), cur(C_UC // 2)),
                  pl.BlockSpec((None, A_HEADS, BLK, 2 * BLK), lambda b, i: (jnp.minimum(i, 2), 0, 0, 0)),
                  pl.BlockSpec(memory_space=pltpu.SMEM),
                  pl.BlockSpec((256, 256), lambda b, i: (0, 0)),
                  pl.BlockSpec((1, 256), lambda b, i: (0, 0))],
        out_specs=[pl.BlockSpec((BLK, 512), lambda b, i: (b * nb + i, 0)),
                   pl.BlockSpec((BLK, 256), lambda b, i: (b * nb + i, 0))],
        out_shape=[jax.ShapeDtypeStruct((n, 512), bf16), jax.ShapeDtypeStruct((n, 256), bf16)],
        compiler_params=_cparams(("parallel", "arbitrary")),
        name="swa_pool",
    )(z, z, z, z, z, z, z, bias_a, sinks, wpool_bd, pool_scale)


def _diff_attn_body(lam_init, q_ref, k_ref, v_ref, bias_ref, dl_ref, go_ref, o_ref):
    i = pl.program_id(1)
    lane = lax.broadcasted_iota(i32, (1, LANES), 1)
    lo = lane < 64
    zero = jnp.zeros((), bf16)
    qm = []
    for p in range(2):
        qp = q_ref[:, p * LANES:(p + 1) * LANES]
        qm.append([[jnp.where((lane >= 64 * e + 32 * c) & (lane < 64 * e + 32 * c + 32), qp, zero)
                    for c in range(2)] for e in range(2)])
    colmask0 = jnp.where(lane < PAD, NEG, 0.0).astype(f32)

    def step(j, carry):
        ms, ls, accs = carry
        d = i - j
        start = pl.multiple_of(j * BLK, BLK)
        kj = k_ref[pl.ds(start, BLK), :]
        vj = v_ref[pl.ds(start, BLK), :]
        cm = jnp.where(j == 0, colmask0, 0.0)
        ms, ls, accs = list(ms), list(ls), list(accs)
        for p in range(2):
            kp = kj[:, p * LANES:(p + 1) * LANES]
            vp = vj[:, p * LANES:(p + 1) * LANES]
            vbd = jnp.concatenate([jnp.where(lo, vp, zero), jnp.where(lo, zero, vp)], axis=0)
            for c in range(2):
                ps, alphas = [], []
                for e in range(2):
                    h = 2 * p + e
                    idx = 2 * h + c
                    s = _dot_nt(qm[p][e][c], kp) + bias_ref[h, d] + cm
                    m_new = jnp.maximum(ms[idx], jnp.max(s, axis=-1, keepdims=True))
                    alpha = jnp.exp(ms[idx] - m_new)
                    pe = jnp.exp(s - m_new)
                    ls[idx] = alpha * ls[idx] + jnp.sum(pe, axis=-1, keepdims=True)
                    ms[idx] = m_new
                    ps.append(pe.astype(bf16))
                    alphas.append(alpha)
                pcat = jnp.concatenate(ps, axis=1)
                alpha_l = jnp.where(lo, alphas[0], alphas[1])
                accs[2 * p + c] = accs[2 * p + c] * alpha_l + _dot(pcat, vbd)
        return tuple(ms), tuple(ls), tuple(accs)

    init = (tuple(jnp.full((BLK, 1), 3 * NEG, f32) for _ in range(8)),
            tuple(jnp.zeros((BLK, 1), f32) for _ in range(8)),
            tuple(jnp.zeros((BLK, LANES), f32) for _ in range(4)))
    ms, ls, accs = lax.fori_loop(0, i + 1, step, init)

    dl = dl_ref[...]
    lam = (jnp.exp(jnp.sum(dl[0:1] * dl[1:2], axis=-1, keepdims=True))
           - jnp.exp(jnp.sum(dl[2:3] * dl[3:4], axis=-1, keepdims=True)) + lam_init)
    for p in range(2):
        l0 = jnp.where(lo, ls[4 * p], ls[4 * p + 2])
        l1 = jnp.where(lo, ls[4 * p + 1], ls[4 * p + 3])
        o = accs[2 * p] / l0 - lam * (accs[2 * p + 1] / l1)
        sq = o * o
        ss_lo = jnp.sum(jnp.where(lo, sq, 0.0), axis=-1, keepdims=True)
        ss_hi = jnp.sum(jnp.where(lo, 0.0, sq), axis=-1, keepdims=True)
        msq = jnp.where(lo, ss_lo, ss_hi) * (1.0 / 64)
        o = o * lax.rsqrt(msq + EPS) * go_ref[...] * (1.0 - lam_init)
        o_ref[:, p * LANES:(p + 1) * LANES] = o.astype(bf16)


def _diff_attn(z3, bias_b, dl, go, lam_init, B, nb):
    L = nb * BLK
    return pl.pallas_call(
        functools.partial(_diff_attn_body, lam_init),
        grid=(B, nb),
        in_specs=[pl.BlockSpec((None, BLK, 256), lambda b, i: (b, i, C_QB // 2)),
                  pl.BlockSpec((None, L, 256), lambda b, i: (b, 0, C_KB // 2)),
                  pl.BlockSpec((None, L, 256), lambda b, i: (b, 0, C_VB // 2)),
                  pl.BlockSpec((B_HEADS, nb, BLK, BLK), lambda b, i: (0, 0, 0, 0)),
                  pl.BlockSpec((4, 32), lambda b, i: (0, 0)),
                  pl.BlockSpec((1, LANES), lambda b, i: (0, 0))],
        out_specs=pl.BlockSpec((None, BLK, 256), lambda b, i: (b, i, 0)),
        out_shape=jax.ShapeDtypeStruct((B, L, 256), bf16),
        compiler_params=_cparams(("parallel", "arbitrary")),
        name="diff_attn",
    )(z3, z3, z3, bias_b, dl, go)


def _outproj_common(oa_ref, ob_ref, oc_ref, w_ref, h_ref, g_ref):
    y = (_dot(oa_ref[...], w_ref[0:512, :]) + _dot(ob_ref[...], w_ref[512:768, :])
         + _dot(oc_ref[...], w_ref[768:1024, :]))
    hn = h_ref[...] + y
    ms = jnp.mean(hn * hn, axis=-1, keepdims=True)
    return hn, hn * lax.rsqrt(ms + EPS) * g_ref[...]


def _outproj_body(oa_ref, ob_ref, oc_ref, w_ref, h_ref, g_ref, ho_ref, xn_ref):
    hn, xn = _outproj_common(oa_ref, ob_ref, oc_ref, w_ref, h_ref, g_ref)
    ho_ref[...] = hn
    xn_ref[...] = xn.astype(bf16)


def _outproj_router_body(nb, tm, oa_ref, ob_ref, oc_ref, w_ref, h_ref, g_ref, wrh_ref, wrl_ref, tri_ref,
                         ho_ref, xn_ref, info_ref, cnt_ref):
    step = pl.program_id(0)
    hn, xn = _outproj_common(oa_ref, ob_ref, oc_ref, w_ref, h_ref, g_ref)
    ho_ref[...] = hn
    xn_ref[...] = xn

    hi = xn.astype(bf16)
    lo = (xn - hi.astype(f32)).astype(bf16)
    logits = _dot(hi, wrh_ref[...]) + _dot(lo, wrh_ref[...]) + _dot(hi, wrl_ref[...])
    lane = lax.broadcasted_iota(i32, (1, LANES), 1)
    ninf = -jnp.inf
    lg = jnp.where(lane < N_EXPERTS, logits, ninf)
    m1 = jnp.max(lg, axis=-1, keepdims=True)
    i1 = jnp.min(jnp.where(lg == m1, lane, LANES), axis=-1, keepdims=True)
    lg2 = jnp.where(lane == i1, ninf, lg)
    m2 = jnp.max(lg2, axis=-1, keepdims=True)
    i2 = jnp.min(jnp.where(lg2 == m2, lane, LANES), axis=-1, keepdims=True)
    e21 = jnp.exp(m2 - m1)
    g1 = 1.0 / (1.0 + e21)
    g2 = e21 * g1

    rowin = lax.broadcasted_iota(i32, (BLK, 1), 0)
    valids = []
    for k in range(tm // BLK):
        blk = step * (tm // BLK) + k
        valids.append(jnp.logical_or(blk % nb != 0, rowin >= PAD))
    valid = jnp.concatenate(valids, axis=0)
    oh = jnp.where(valid & ((lane == i1) | (lane == i2)), 1.0, 0.0)

    @pl.when(step == 0)
    def _():
        cnt_ref[...] = jnp.zeros_like(cnt_ref)

    base = cnt_ref[...]
    rank = _dot(tri_ref[...], oh.astype(bf16)) + base
    r1 = jnp.sum(jnp.where(lane == i1, rank, 0.0), axis=-1, keepdims=True)
    r2 = jnp.sum(jnp.where(lane == i2, rank, 0.0), axis=-1, keepdims=True)
    cnt_ref[...] = base + jnp.sum(oh, axis=0, keepdims=True)
    info = jnp.where(lane == 0, i1.astype(f32),
           jnp.where(lane == 1, i2.astype(f32),
           jnp.where(lane == 2, g1,
           jnp.where(lane == 3, g2,
           jnp.where(lane == 4, r1,
           jnp.where(lane == 5, r2, 0.0))))))
    info_ref[...] = info


def _outproj(oa, ob, oc, w, h, g, tm, router=None, nb=None):
    n = h.shape[0]
    row = lambda c: pl.BlockSpec((tm, c), lambda i: (i, 0))
    const = lambda r, c: pl.BlockSpec((r, c), lambda i: (0, 0))
    in_specs = [row(512), row(256), row(256), const(D, D), row(D), const(1, D)]
    if router is None:
        return pl.pallas_call(
            _outproj_body,
            grid=(n // tm,),
            in_specs=in_specs,
            out_specs=[row(D), row(D)],
            out_shape=[jax.ShapeDtypeStruct((n, D), f32), jax.ShapeDtypeStruct((n, D), bf16)],
            compiler_params=_cparams(("parallel",)),
            name="outproj",
        )(oa, ob, oc, w, h, g)
    wrh, wrl, tri = router
    return pl.pallas_call(
        functools.partial(_outproj_router_body, nb, tm),
        grid=(n // tm,),
        in_specs=in_specs + [const(D, LANES), const(D, LANES), const(tm, tm)],
        out_specs=[row(D), row(D), row(LANES), const(1, LANES)],
        out_shape=[jax.ShapeDtypeStruct((n, D), f32), jax.ShapeDtypeStruct((n, D), f32),
                   jax.ShapeDtypeStruct((n, LANES), f32), jax.ShapeDtypeStruct((1, LANES), f32)],
        compiler_params=_cparams(("arbitrary",)),
        name="outproj_router",
    )(oa, ob, oc, w, h, g, wrh, wrl, tri)


def _ffn_body(tf, x_ref, h_ref, wg_ref, wu_ref, wd_ref, o_ref):
    x = x_ref[...]
    acc = h_ref[...]
    for c in range(wg_ref.shape[1] // tf):
        g = _dot(x, wg_ref[:, c * tf:(c + 1) * tf])
        u = _dot(x, wu_ref[:, c * tf:(c + 1) * tf])
        a = (g * jax.nn.sigmoid(g) * u).astype(bf16)
        acc = acc + _dot(a, wd_ref[c * tf:(c + 1) * tf, :])
    o_ref[...] = acc


def _ffn(xn, h, wg, wu, wd, tm, tf):
    n = h.shape[0]
    F = wg.shape[1]
    const = lambda i: (0, 0)
    return pl.pallas_call(
        functools.partial(_ffn_body, tf),
        grid=(n // tm,),
        in_specs=[pl.BlockSpec((tm, D), lambda i: (i, 0)),
                  pl.BlockSpec((tm, D), lambda i: (i, 0)),
                  pl.BlockSpec((D, F), const),
                  pl.BlockSpec((D, F), const),
                  pl.BlockSpec((F, D), const)],
        out_specs=pl.BlockSpec((tm, D), lambda i: (i, 0)),
        out_shape=jax.ShapeDtypeStruct((n, D), f32),
        compiler_params=_cparams(("parallel",)),
        name="ffn_dense",
    )(xn, h, wg, wu, wd)


def _dispatch_body(nb, pos_ref, xn_ref, xs_in_ref, xs_ref, sem):
    del xs_in_ref
    b = pl.program_id(0)
    i = pl.program_id(1)
    base = (b * nb + i) * BLK
    first = jnp.where(i == 0, PAD, 0)

    def copy(r, sl):
        p = pos_ref[0, 0, sl * BLK + r]
        return pltpu.make_async_copy(xn_ref.at[pl.ds(base + r, 1)], xs_ref.at[pl.ds(p, 1)], sem)

    def issue(r, c):
        copy(r, 0).start()
        copy(r, 1).start()
        return c

    def drain(r, c):
        copy(r, 0).wait()
        copy(r, 1).wait()
        return c

    lax.fori_loop(first, BLK, issue, 0)
    lax.fori_loop(first, BLK, drain, 0)


def _dispatch(xn, pos, xs0, B, nb):
    return pl.pallas_call(
        functools.partial(_dispatch_body, nb),
        grid=(B, nb),
        in_specs=[pl.BlockSpec((1, 1, 2 * BLK), lambda b, i: (b * nb + i, 0, 0), memory_space=pltpu.SMEM),
                  pl.BlockSpec(memory_space=pl.ANY),
                  pl.BlockSpec(memory_space=pl.ANY)],
        out_specs=pl.BlockSpec(memory_space=pl.ANY),
        out_shape=jax.ShapeDtypeStruct(xs0.shape, f32),
        scratch_shapes=[pltpu.SemaphoreType.DMA(())],
        input_output_aliases={2: 0},
        compiler_params=pltpu.CompilerParams(dimension_semantics=("arbitrary", "arbitrary"),
                                             has_side_effects=True),
        name="moe_dispatch",
    )(pos, xn, xs0)


def _experts_body(nf, te_ref, tv_ref, x_ref, wg_ref, wu_ref, wd_ref, y_ref, xb_ref, acc_ref):
    del te_ref
    t = pl.program_id(0)
    f = pl.program_id(1)
    valid = tv_ref[t] == 1

    @pl.when(valid & (f == 0))
    def _():
        xb_ref[...] = x_ref[...].astype(bf16)
        acc_ref[...] = jnp.zeros_like(acc_ref)

    @pl.when(valid)
    def _():
        x = xb_ref[...]
        g = _dot(x, wg_ref[...])
        u = _dot(x, wu_ref[...])
        a = (g * jax.nn.sigmoid(g) * u).astype(bf16)
        acc_ref[...] += _dot(a, wd_ref[...])

    @pl.when(valid & (f == nf - 1))
    def _():
        y_ref[...] = acc_ref[...]

    @pl.when(jnp.logical_not(valid) & (f == nf - 1))
    def _():
        y_ref[...] = jnp.zeros_like(y_ref)


def _experts(xs, te, tv, wg, wu, wd, tme, tf):
    rows = xs.shape[0]
    T = rows // tme
    nf = D_FF_EXPERT // tf
    grid_spec = pltpu.PrefetchScalarGridSpec(
        num_scalar_prefetch=2,
        grid=(T, nf),
        in_specs=[pl.BlockSpec((tme, D), lambda t, f, te, tv: (t, 0)),
                  pl.BlockSpec((None, D, tf), lambda t, f, te, tv: (te[t], 0, f)),
                  pl.BlockSpec((None, D, tf), lambda t, f, te, tv: (te[t], 0, f)),
                  pl.BlockSpec((None, tf, D), lambda t, f, te, tv: (te[t], f, 0))],
        out_specs=pl.BlockSpec((tme, D), lambda t, f, te, tv: (t, 0)),
        scratch_shapes=[pltpu.VMEM((tme, D), bf16), pltpu.VMEM((tme, D), f32)],
    )
    return pl.pallas_call(
        functools.partial(_experts_body, nf),
        grid_spec=grid_spec,
        out_shape=jax.ShapeDtypeStruct((rows, D), f32),
        compiler_params=_cparams(("arbitrary", "arbitrary")),
        name="moe_experts",
    )(te, tv, xs, wg, wu, wd)


def _combine_body(pos_ref, h_ref, info_ref, ys_ref, o_ref, buf_ref, sem):
    def copy(r, sl):
        p = pos_ref[0, 0, sl * BLK + r]
        return pltpu.make_async_copy(ys_ref.at[pl.ds(p, 1)], buf_ref.at[sl, pl.ds(r, 1)], sem)

    def issue(r, c):
        copy(r, 0).start()
        copy(r, 1).start()
        return c

    def drain(r, c):
        copy(r, 0).wait()
        copy(r, 1).wait()
        return c

    lax.fori_loop(0, BLK, issue, 0)
    lax.fori_loop(0, BLK, drain, 0)
    info = info_ref[...]
    o_ref[...] = h_ref[...] + info[:, 2:3] * buf_ref[0] + info[:, 3:4] * buf_ref[1]


def _combine(h, info, pos, ys, B, nb):
    nt = nb - 1
    return pl.pallas_call(
        _combine_body,
        grid=(B, nt),
        in_specs=[pl.BlockSpec((1, 1, 2 * BLK), lambda b, i: (b * nb + i + 1, 0, 0), memory_space=pltpu.SMEM),
                  pl.BlockSpec((BLK, D), lambda b, i: (b * nb + i + 1, 0)),
                  pl.BlockSpec((BLK, LANES), lambda b, i: (b * nb + i + 1, 0)),
                  pl.BlockSpec(memory_space=pl.ANY)],
        out_specs=pl.BlockSpec((BLK, D), lambda b, i: (b * nt + i, 0)),
        out_shape=jax.ShapeDtypeStruct((B * nt * BLK, D), f32),
        scratch_shapes=[pltpu.VMEM((2, BLK, D), f32), pltpu.SemaphoreType.DMA(())],
        compiler_params=_cparams(("arbitrary", "arbitrary")),
        name="moe_combine",
    )(pos, h, info, ys)


def _strip_body(h_ref, o_ref):
    o_ref[...] = h_ref[...]


def _strip_meta(h, B, nb):
    nt = nb - 1
    return pl.pallas_call(
        _strip_body,
        grid=(B, nt),
        in_specs=[pl.BlockSpec((BLK, D), lambda b, i: (b * nb + i + 1, 0))],
        out_specs=pl.BlockSpec((BLK, D), lambda b, i: (b * nt + i, 0)),
        out_shape=jax.ShapeDtypeStruct((B * nt * BLK, D), f32),
        compiler_params=_cparams(("parallel", "arbitrary")),
        name="strip_meta",
    )(h)


def _bias_tables(rel, nb):
    table_a = rel[:, :A_HEADS]
    table_b = rel[:, A_HEADS:]
    a = jnp.arange(BLK)[:, None]
    bb = jnp.arange(2 * BLK)[None, :]
    n = BLK + a - bb
    in_win = (n >= 0) & (n < BLK)
    bias = jnp.transpose(table_a[_t5_bucket(jnp.maximum(n, 0))], (2, 0, 1)).astype(f32)
    variants = []
    for blk in range(3):
        key_ok = (blk - 1) * BLK + bb >= PAD
        variants.append(jnp.where((in_win & key_ok)[None], bias, NEG))
    bias_a = jnp.stack(variants, axis=0)

    d = jnp.arange(nb)[:, None, None]
    kb = jnp.arange(BLK)[None, None, :]
    nn = d * BLK + a[None] - kb
    bias_b = jnp.transpose(table_b[_t5_bucket(jnp.maximum(nn, 0))], (3, 0, 1, 2)).astype(f32)
    bias_b = jnp.where((nn >= 0)[None], bias_b, NEG)
    return bias_a, bias_b


def _block_diag_ones(gs):
    idx = np.arange(LANES) // gs
    return jnp.asarray((idx[:, None] == idx[None, :]).astype(np.float32), dtype=bf16)


def _col_scale(gain_a, gain_b):
    one = jnp.ones((LANES,), f32)
    parts = [jnp.tile(gain_a[0], 8) * (64 ** -0.5), jnp.tile(gain_a[1], 2), one,
             jnp.tile(gain_b[0], 8) * (32 ** -0.5), jnp.tile(gain_b[1], 8), one, one, one, one]
    return jnp.concatenate(parts).reshape(1, D_IN).astype(f32)


def _pool_weight(w_pool):
    out = jnp.zeros((256, 256), f32)
    for gi in range(4):
        out = out.at[gi * 64:(gi + 1) * 64, gi * 64:(gi + 1) * 64].set(w_pool[gi])
    return out.astype(bf16)


def _row_tile(n):
    for tm in (512, 256, 128):
        if n % tm == 0:
            return tm
    raise ValueError(n)


def kernel(x, meta_tokens, rel_bias_table, g_mix, w_in, qk_gain_a, sinks, qk_gain_b, diff_lambda, g_diff_out,
           w_pool, pool_scale, w_out, g_ffn, w_ffn_gate, w_ffn_up, w_ffn_down, w_router, w_exp_gate, w_exp_up,
           w_exp_down):
    B, seq, _ = x.shape
    assert seq % BLK == 0
    depth = g_mix.shape[0]
    nb = seq // BLK + 1
    L = nb * BLK
    n = B * L
    tm = _row_tile(n)
    tme = 512
    tf_e = 512
    tf_d = 256

    bias_a, bias_b = _bias_tables(rel_bias_table, nb)
    g64 = _block_diag_ones(64)
    g32 = _block_diag_ones(32)

    h = _embed(x, meta_tokens, B, nb)
    out = None
    for l in range(depth):
        lam_init = 0.8 - 0.6 * math.exp(-0.3 * l)
        cs = _col_scale(qk_gain_a[l], qk_gain_b[l])
        z = _inproj(h, g_mix[l].reshape(1, D), w_in[l].astype(bf16), cs, g64, g32, tm)
        oa, oc = _swa_pool(z, bias_a, sinks[l], _pool_weight(w_pool[l]), pool_scale[l].reshape(1, 256), B, nb)
        ob = _diff_attn(z.reshape(B, L, D_IN), bias_b, diff_lambda[l], jnp.tile(g_diff_out[l], 2).reshape(1, LANES),
                        lam_init, B, nb).reshape(n, 256)
        wo = w_out[l].astype(bf16)
        gf = g_ffn[l].reshape(1, D)
        j = l // 2
        if l % 2 == 0:
            h, xn = _outproj(oa, ob, oc, wo, h, gf, tm)
            h = _ffn(xn, h, w_ffn_gate[j].astype(bf16), w_ffn_up[j].astype(bf16), w_ffn_down[j].astype(bf16),
                     tm, tf_d)
            out = None
        else:
            wr = jnp.pad(w_router[j], ((0, 0), (0, LANES - N_EXPERTS)))
            wrh = wr.astype(bf16)
            wrl = (wr - wrh.astype(f32)).astype(bf16)
            tri = jnp.asarray(np.tril(np.ones((tm, tm), np.float32), -1), dtype=bf16)
            h, xn, info, cnt = _outproj(oa, ob, oc, wo, h, gf, tm, router=(wrh, wrl, tri), nb=nb)

            counts = cnt[0, :N_EXPERTS].astype(i32)
            padded = ((counts + tme - 1) // tme) * tme
            ends = jnp.cumsum(padded)
            offs = ends - padded
            e1 = info[:, 0].astype(i32)
            e2 = info[:, 1].astype(i32)
            pos1 = offs[e1] + info[:, 4].astype(i32)
            pos2 = offs[e2] + info[:, 5].astype(i32)
            pos = jnp.concatenate([pos1.reshape(B * nb, 1, BLK), pos2.reshape(B * nb, 1, BLK)], axis=2)
            n_real = B * (seq + N_META) * 2
            T = -(-n_real // tme) + N_EXPERTS
            starts = jnp.arange(T, dtype=i32) * tme
            te = jnp.minimum(jnp.sum((ends[None, :] <= starts[:, None]).astype(i32), axis=1), N_EXPERTS - 1)
            tv = (starts < ends[-1]).astype(i32)

            xs = _dispatch(xn, pos, jnp.zeros((T * tme, D), f32), B, nb)
            ys = _experts(xs, te, tv, w_exp_gate[j].astype(bf16), w_exp_up[j].astype(bf16),
                          w_exp_down[j].astype(bf16), tme, tf_e)
            if l == depth - 1:
                out = _combine(h, info, pos, ys, B, nb)
            else:
                raise NotImplementedError("MoE layer must be the last layer")
    if out is None:
        out = _strip_meta(h, B, nb)
    return out.reshape(B, seq, D)
```

```python
import functools
import math

import jax
import jax.numpy as jnp
import numpy as np
from jax import lax
from jax.experimental import pallas as pl
from jax.experimental.pallas import tpu as pltpu

f32 = jnp.float32
bf16 = jnp.bfloat16
i32 = jnp.int32

D = 1024
BLK = 128
N_META = 16
PAD = BLK - N_META
A_HEADS = 8
B_HEADS = 4
N_BUCKETS = 32
MAX_DISTANCE = 128
N_EXPERTS = 8
D_FF_EXPERT = 3584
EPS = 1e-6
NEG = -1e30
LANES = 128
C_QA, C_KA, C_VA, C_KB, C_UC = 0, 4, 5, 6, 8
D_MAIN = 1280
D_T = 512

VMEM_LIMIT = 56 * 1024 * 1024


def _cparams(sem):
    return pltpu.CompilerParams(dimension_semantics=sem, vmem_limit_bytes=VMEM_LIMIT)


def _t5_bucket(n):
    max_exact = N_BUCKETS // 2
    nf = jnp.maximum(n, 1).astype(f32)
    large = max_exact + (jnp.log(nf / max_exact) / math.log(MAX_DISTANCE / max_exact)
                         * (N_BUCKETS - max_exact)).astype(i32)
    large = jnp.minimum(large, N_BUCKETS - 1)
    return jnp.where(n < max_exact, n, large)


def _dot(a, b):
    return jnp.dot(a, b, preferred_element_type=f32)


def _dot_nt(a, b):
    return lax.dot_general(a, b, (((1,), (1,)), ((), ())), preferred_element_type=f32)


def _swap_halves(x):
    return jnp.concatenate([x[:, 64:], x[:, :64]], axis=1)


def _embed_body(x_ref, m_ref, o_ref):
    i = pl.program_id(1)

    @pl.when(i == 0)
    def _():
        o_ref[0:PAD, :] = jnp.zeros((PAD, D), f32)
        o_ref[PAD:, :] = m_ref[...]

    @pl.when(i > 0)
    def _():
        o_ref[...] = x_ref[...]


def _embed(x, meta, B, nb):
    seq = x.shape[1]
    x2 = x.reshape(B * seq, D)
    return pl.pallas_call(
        _embed_body,
        grid=(B, nb),
        in_specs=[pl.BlockSpec((BLK, D), lambda b, i: (b * (nb - 1) + jnp.maximum(i - 1, 0), 0)),
                  pl.BlockSpec((N_META, D), lambda b, i: (0, 0))],
        out_specs=pl.BlockSpec((BLK, D), lambda b, i: (b * nb + i, 0)),
        out_shape=jax.ShapeDtypeStruct((B * nb * BLK, D), f32),
        compiler_params=_cparams(("parallel", "arbitrary")),
        name="embed",
    )(x2, meta)


def _split_bf16(x):
    hi = x.astype(bf16)
    return hi, (x - hi.astype(f32)).astype(bf16)


def _inproj_body(tm, h_ref, g_ref, w_ref, wt_ref, cs_ref, cst_ref, g64_ref, g32_ref, g32t_ref,
                 z_ref, qt_ref, vt_ref):
    x = h_ref[...]
    ms = jnp.mean(x * x, axis=-1, keepdims=True)
    xn = (x * lax.rsqrt(ms + EPS) * g_ref[...]).astype(bf16)
    for c2 in range(D_MAIN // 256):
        zc2 = _dot(xn, w_ref[:, c2 * 256:(c2 + 1) * 256])
        for half in range(2):
            c = 2 * c2 + half
            zc = zc2[:, half * LANES:(half + 1) * LANES]
            if c < C_VA:
                gm, inv = g64_ref, 1.0 / 64
            elif C_KB <= c < C_UC:
                gm, inv = g32_ref, 1.0 / 32
            else:
                gm = None
            if gm is not None:
                hi, lo = _split_bf16(zc * zc)
                ss = _dot(hi, gm[...]) + _dot(lo, gm[...])
                zc = zc * lax.rsqrt(ss * inv + EPS)
            z_ref[:, c * LANES:(c + 1) * LANES] = (zc * cs_ref[:, c * LANES:(c + 1) * LANES]).astype(bf16)

    zt = _dot_nt(wt_ref[...], xn)
    zq = zt[0:256]
    hi, lo = _split_bf16(zq * zq)
    ss = _dot(g32t_ref[...], hi) + _dot(g32t_ref[...], lo)
    zq = zq * lax.rsqrt(ss * (1.0 / 32) + EPS) * cst_ref[...]
    zv = zt[256:512]
    for k in range(tm // BLK):
        qt_ref[k] = zq[:, k * BLK:(k + 1) * BLK].astype(bf16)
        vt_ref[k] = zv[:, k * BLK:(k + 1) * BLK].astype(bf16)


def _inproj(h, g, w, wt, cs, cst, g64, g32, g32t, tm):
    n = h.shape[0]
    const = lambda i: (0, 0)
    kb = tm // BLK
    return pl.pallas_call(
        functools.partial(_inproj_body, tm),
        grid=(n // tm,),
        in_specs=[pl.BlockSpec((tm, D), lambda i: (i, 0)),
                  pl.BlockSpec((1, D), const),
                  pl.BlockSpec((D, D_MAIN), const),
                  pl.BlockSpec((D_T, D), const),
                  pl.BlockSpec((1, D_MAIN), const),
                  pl.BlockSpec((256, 1), const),
                  pl.BlockSpec((LANES, LANES), const),
                  pl.BlockSpec((LANES, LANES), const),
                  pl.BlockSpec((256, 256), const)],
        out_specs=[pl.BlockSpec((tm, D_MAIN), lambda i: (i, 0)),
                   pl.BlockSpec((kb, 256, BLK), lambda i: (i, 0, 0)),
                   pl.BlockSpec((kb, 256, BLK), lambda i: (i, 0, 0))],
        out_shape=[jax.ShapeDtypeStruct((n, D_MAIN), bf16),
                   jax.ShapeDtypeStruct((n // BLK, 256, BLK), bf16),
                   jax.ShapeDtypeStruct((n // BLK, 256, BLK), bf16)],
        compiler_params=_cparams(("parallel",)),
        name="norm_inproj",
    )(h, g, w, wt, cs, cst, g64, g32, g32t)


def _swa_pool_body(q_ref, kp_ref, kc_ref, vp_ref, vc_ref, up_ref, uc_ref, bias_ref, sink_ref,
                   wp_ref, ps_ref, oa_ref, oc_ref):
    i = pl.program_id(1)
    lane = lax.broadcasted_iota(i32, (1, LANES), 1)
    lo = lane < 64
    k_cat = jnp.concatenate([kp_ref[...], kc_ref[...]], axis=0)
    v_cat = jnp.concatenate([vp_ref[...], vc_ref[...]], axis=0)
    k_sw = _swap_halves(k_cat)
    v_sw = _swap_halves(v_cat)
    zero = jnp.zeros((), bf16)
    for pb in range(4):
        qpb = q_ref[:, pb * LANES:(pb + 1) * LANES]
        grp = pb // 2
        outs = []
        for e in range(2):
            h = 2 * pb + e
            qm = jnp.where(lo if e == 0 else jnp.logical_not(lo), qpb, zero)
            kk = k_cat if e == grp else k_sw
            vv = v_cat if e == grp else v_sw
            s = _dot_nt(qm, kk) + bias_ref[h]
            sink = sink_ref[h]
            m = jnp.maximum(jnp.max(s, axis=-1, keepdims=True), sink)
            p = jnp.exp(s - m)
            l = jnp.sum(p, axis=-1, keepdims=True) + jnp.exp(sink - m)
            outs.append(_dot(p.astype(bf16), vv) * (1.0 / l))
        oa_ref[:, pb * LANES:(pb + 1) * LANES] = jnp.where(lo, outs[0], outs[1]).astype(bf16)

    row = lax.broadcasted_iota(i32, (BLK, 1), 0)
    t = i * BLK + row - PAD
    u_cur = jnp.where(t >= 0, uc_ref[...].astype(f32), 0.0)
    u_prev = jnp.where(i > 0, up_ref[BLK - 16:, :].astype(f32), 0.0)
    ext = jnp.concatenate([u_prev, u_cur], axis=0)
    s2 = ext + pltpu.roll(ext, 1, 0)
    s4 = s2 + pltpu.roll(s2, 2, 0)
    s8 = s4 + pltpu.roll(s4, 4, 0)
    s16 = s8 + pltpu.roll(s8, 8, 0)
    lane2 = lax.broadcasted_iota(i32, (1, 256), 1)
    grp2 = lane2 // 64
    sums = jnp.where(grp2 == 0, s2, jnp.where(grp2 == 1, s4, jnp.where(grp2 == 2, s8, s16)))[16:, :]
    win = jnp.where(grp2 == 0, 2, jnp.where(grp2 == 1, 4, jnp.where(grp2 == 2, 8, 16)))
    cnt = jnp.maximum(jnp.minimum(t + 1, win), 1).astype(f32)
    dlt = (sums / cnt - u_cur).astype(bf16)
    oc_ref[...] = (_dot(dlt, wp_ref[...]) * ps_ref[...]).astype(bf16)


def _swa_pool(z, bias_a, sinks, wpool_bd, pool_scale, B, nb):
    n = z.shape[0]
    cur = lambda c: (lambda b, i: (b * nb + i, c))
    prev = lambda c: (lambda b, i: (b * nb + jnp.maximum(i - 1, 0), c))
    return pl.pallas_call(
        _swa_pool_body,
        grid=(B, nb),
        in_specs=[pl.BlockSpec((BLK, 512), cur(0)),
                  pl.BlockSpec((BLK, LANES), prev(C_KA)),
                  pl.BlockSpec((BLK, LANES), cur(C_KA)),
                  pl.BlockSpec((BLK, LANES), prev(C_VA)),
                  pl.BlockSpec((BLK, LANES), cur(C_VA)),
                  pl.BlockSpec((BLK, 256), prev(C_UC // 2)),
                  pl.BlockSpec((BLK, 256), cur(C_UC // 2)),
                  pl.BlockSpec((None, A_HEADS, BLK, 2 * BLK), lambda b, i: (jnp.minimum(i, 2), 0, 0, 0)),
                  pl.BlockSpec(memory_space=pltpu.SMEM),
                  pl.BlockSpec((256, 256), lambda b, i: (0, 0)),
                  pl.BlockSpec((1, 256), lambda b, i: (0, 0))],
        out_specs=[pl.BlockSpec((BLK, 512), lambda b, i: (b * nb + i, 0)),
                   pl.BlockSpec((BLK, 256), lambda b, i: (b * nb + i, 0))],
        out_shape=[jax.ShapeDtypeStruct((n, 512), bf16), jax.ShapeDtypeStruct((n, 256), bf16)],
        compiler_params=_cparams(("parallel", "arbitrary")),
        name="swa_pool",
    )(z, z, z, z, z, z, z, bias_a, sinks, wpool_bd, pool_scale)


def _diff_attn_body(lam_init, nb, kb_step, qt_ref, k_ref, vt_ref, bias_ref, dl_ref, go_ref, o_ref,
                    acc_ref, m_ref, l_ref):
    i = pl.program_id(1)
    row = lax.broadcasted_iota(i32, (BLK, 1), 0)
    top = row < 64
    zero = jnp.zeros((), bf16)
    qcat = []
    for p in range(2):
        qp = qt_ref[p * BLK:(p + 1) * BLK, :]
        qcat.append([jnp.concatenate(
            [jnp.where((row >= 64 * e + 32 * c) & (row < 64 * e + 32 * c + 32), qp, zero) for c in range(2)],
            axis=1) for e in range(2)])
    rows_k = lax.broadcasted_iota(i32, (kb_step * BLK, 1), 0)
    padmask = jnp.where(rows_k < PAD, NEG, 0.0).astype(f32)

    m_ref[...] = jnp.full(m_ref.shape, 3 * NEG, f32)
    l_ref[...] = jnp.zeros(l_ref.shape, f32)
    acc_ref[...] = jnp.zeros(acc_ref.shape, f32)

    def step(j0, first):
        blks = [jnp.minimum(j0 + kb, nb - 1) for kb in range(kb_step)]
        dix = [jnp.where(i - (j0 + kb) >= 0, i - (j0 + kb), nb) for kb in range(kb_step)]
        kparts = [k_ref[pl.ds(pl.multiple_of(b * BLK, BLK), BLK), :] for b in blks]
        vparts = [vt_ref[b] for b in blks]
        for p in range(2):
            kcat = jnp.concatenate([kp[:, p * BLK:(p + 1) * BLK] for kp in kparts], axis=0)
            vtc = jnp.concatenate([vp[p * BLK:(p + 1) * BLK, :] for vp in vparts], axis=1)
            lhs = jnp.concatenate([jnp.where(top, vtc, zero), jnp.where(top, zero, vtc)], axis=1)
            sts = {}
            for e in range(2):
                h = 2 * p + e
                st2 = _dot(kcat, qcat[p][e])
                bias = jnp.concatenate([bias_ref[h, dd] for dd in dix], axis=0)
                if first:
                    bias = bias + padmask
                for c in range(2):
                    sts[(e, c)] = st2[:, c * BLK:(c + 1) * BLK] + bias
            for c in range(2):
                pts, alphas = [], []
                for e in range(2):
                    idx = 2 * (2 * p + e) + c
                    st = sts[(e, c)]
                    m_old = m_ref[idx:idx + 1, :]
                    m_new = jnp.maximum(m_old, jnp.max(st, axis=0, keepdims=True))
                    alpha = jnp.exp(m_old - m_new)
                    pt = jnp.exp(st - m_new)
                    l_ref[idx:idx + 1, :] = alpha * l_ref[idx:idx + 1, :] + jnp.sum(pt, axis=0, keepdims=True)
                    m_ref[idx:idx + 1, :] = m_new
                    pts.append(pt.astype(bf16))
                    alphas.append(alpha)
                upd = _dot(lhs, jnp.concatenate(pts, axis=0))
                a = 2 * p + c
                acc_ref[a, 0:64, :] = acc_ref[a, 0:64, :] * alphas[0] + upd[0:64]
                acc_ref[a, 64:128, :] = acc_ref[a, 64:128, :] * alphas[1] + upd[64:128]

    step(0, True)

    def body(s, carry):
        step(s * kb_step, False)
        return carry

    lax.fori_loop(1, (i + kb_step) // kb_step, body, 0)

    dl = dl_ref[...]
    lam = (jnp.exp(jnp.sum(dl[0:1] * dl[1:2], axis=-1, keepdims=True))
           - jnp.exp(jnp.sum(dl[2:3] * dl[3:4], axis=-1, keepdims=True)) + lam_init)
    for p in range(2):
        parts = []
        for e in range(2):
            h = 2 * p + e
            rows = slice(64 * e, 64 * e + 64)
            o = (acc_ref[2 * p, rows, :] / l_ref[2 * h:2 * h + 1, :]
                 - lam * (acc_ref[2 * p + 1, rows, :] / l_ref[2 * h + 1:2 * h + 2, :]))
            ssq = jnp.sum(o * o, axis=0, keepdims=True)
            parts.append(o * lax.rsqrt(ssq * (1.0 / 64) + EPS))
        o = jnp.concatenate(parts, axis=0) * go_ref[...] * (1.0 - lam_init)
        o_ref[:, p * BLK:(p + 1) * BLK] = o.T.astype(bf16)


def _diff_attn(qt, z3, vt, bias_bt, dl, go, lam_init, B, nb, kb_step):
    L = nb * BLK
    return pl.pallas_call(
        functools.partial(_diff_attn_body, lam_init, nb, kb_step),
        grid=(B, nb),
        in_specs=[pl.BlockSpec((None, 256, BLK), lambda b, i: (b * nb + i, 0, 0)),
                  pl.BlockSpec((None, L, 256), lambda b, i: (b, 0, C_KB // 2)),
                  pl.BlockSpec((nb, 256, BLK), lambda b, i: (b, 0, 0)),
                  pl.BlockSpec((B_HEADS, nb + 1, BLK, BLK), lambda b, i: (0, 0, 0, 0)),
                  pl.BlockSpec((4, 32), lambda b, i: (0, 0)),
                  pl.BlockSpec((BLK, 1), lambda b, i: (0, 0))],
        out_specs=pl.BlockSpec((BLK, 256), lambda b, i: (b * nb + i, 0)),
        out_shape=jax.ShapeDtypeStruct((B * L, 256), bf16),
        scratch_shapes=[pltpu.VMEM((4, BLK, BLK), f32), pltpu.VMEM((8, BLK), f32), pltpu.VMEM((8, BLK), f32)],
        compiler_params=_cparams(("parallel", "arbitrary")),
        name="diff_attn",
    )(qt, z3, vt, bias_bt, dl, go)


def _outproj_common(oa_ref, ob_ref, oc_ref, w_ref, h_ref, g_ref):
    y = (_dot(oa_ref[...], w_ref[0:512, :]) + _dot(ob_ref[...], w_ref[512:768, :])
         + _dot(oc_ref[...], w_ref[768:1024, :]))
    hn = h_ref[...] + y
    ms = jnp.mean(hn * hn, axis=-1, keepdims=True)
    return hn, hn * lax.rsqrt(ms + EPS) * g_ref[...]


def _outproj_body(oa_ref, ob_ref, oc_ref, w_ref, h_ref, g_ref, ho_ref, xn_ref):
    hn, xn = _outproj_common(oa_ref, ob_ref, oc_ref, w_ref, h_ref, g_ref)
    ho_ref[...] = hn
    xn_ref[...] = xn.astype(bf16)


def _outproj_router_body(nb, tm, oa_ref, ob_ref, oc_ref, w_ref, h_ref, g_ref, wrh_ref, wrl_ref, tri_ref,
                         ho_ref, xn_ref, info_ref, cnt_ref):
    step = pl.program_id(0)
    hn, xn = _outproj_common(oa_ref, ob_ref, oc_ref, w_ref, h_ref, g_ref)
    ho_ref[...] = hn
    xn_ref[...] = xn

    hi, lo = _split_bf16(xn)
    logits = _dot(hi, wrh_ref[...]) + _dot(lo, wrh_ref[...]) + _dot(hi, wrl_ref[...])
    lane = lax.broadcasted_iota(i32, (1, LANES), 1)
    ninf = -jnp.inf
    lg = jnp.where(lane < N_EXPERTS, logits, ninf)
    m1 = jnp.max(lg, axis=-1, keepdims=True)
    i1 = jnp.min(jnp.where(lg == m1, lane, LANES), axis=-1, keepdims=True)
    lg2 = jnp.where(lane == i1, ninf, lg)
    m2 = jnp.max(lg2, axis=-1, keepdims=True)
    i2 = jnp.min(jnp.where(lg2 == m2, lane, LANES), axis=-1, keepdims=True)
    e21 = jnp.exp(m2 - m1)
    g1 = 1.0 / (1.0 + e21)
    g2 = e21 * g1

    rowin = lax.broadcasted_iota(i32, (BLK, 1), 0)
    valids = []
    for k in range(tm // BLK):
        blk = step * (tm // BLK) + k
        valids.append(jnp.logical_or(blk % nb != 0, rowin >= PAD))
    valid = jnp.concatenate(valids, axis=0)
    oh = jnp.where(valid & ((lane == i1) | (lane == i2)), 1.0, 0.0)

    @pl.when(step == 0)
    def _():
        cnt_ref[...] = jnp.zeros_like(cnt_ref)

    base = cnt_ref[...]
    rank = _dot(tri_ref[...], oh.astype(bf16)) + base
    r1 = jnp.sum(jnp.where(lane == i1, rank, 0.0), axis=-1, keepdims=True)
    r2 = jnp.sum(jnp.where(lane == i2, rank, 0.0), axis=-1, keepdims=True)
    cnt_ref[...] = base + jnp.sum(oh, axis=0, keepdims=True)
    info = jnp.where(lane == 0, i1.astype(f32),
           jnp.where(lane == 1, i2.astype(f32),
           jnp.where(lane == 2, g1,
           jnp.where(lane == 3, g2,
           jnp.where(lane == 4, r1,
           jnp.where(lane == 5, r2, 0.0))))))
    info_ref[...] = info


def _outproj(oa, ob, oc, w, h, g, tm, router=None, nb=None):
    n = h.shape[0]
    row = lambda c: pl.BlockSpec((tm, c), lambda i: (i, 0))
    const = lambda r, c: pl.BlockSpec((r, c), lambda i: (0, 0))
    in_specs = [row(512), row(256), row(256), const(D, D), row(D), const(1, D)]
    if router is None:
        return pl.pallas_call(
            _outproj_body,
            grid=(n // tm,),
            in_specs=in_specs,
            out_specs=[row(D), row(D)],
            out_shape=[jax.ShapeDtypeStruct((n, D), f32), jax.ShapeDtypeStruct((n, D), bf16)],
            compiler_params=_cparams(("parallel",)),
            name="outproj",
        )(oa, ob, oc, w, h, g)
    wrh, wrl, tri = router
    return pl.pallas_call(
        functools.partial(_outproj_router_body, nb, tm),
        grid=(n // tm,),
        in_specs=in_specs + [const(D, LANES), const(D, LANES), const(tm, tm)],
        out_specs=[row(D), row(D), row(LANES), const(1, LANES)],
        out_shape=[jax.ShapeDtypeStruct((n, D), f32), jax.ShapeDtypeStruct((n, D), f32),
                   jax.ShapeDtypeStruct((n, LANES), f32), jax.ShapeDtypeStruct((1, LANES), f32)],
        compiler_params=_cparams(("arbitrary",)),
        name="outproj_router",
    )(oa, ob, oc, w, h, g, wrh, wrl, tri)


def _ffn_body(tf, x_ref, h_ref, wg_ref, wu_ref, wd_ref, o_ref):
    x = x_ref[...]
    acc = h_ref[...]
    for c in range(wg_ref.shape[1] // tf):
        g = _dot(x, wg_ref[:, c * tf:(c + 1) * tf])
        u = _dot(x, wu_ref[:, c * tf:(c + 1) * tf])
        a = (g * jax.nn.sigmoid(g) * u).astype(bf16)
        acc = acc + _dot(a, wd_ref[c * tf:(c + 1) * tf, :])
    o_ref[...] = acc


def _ffn(xn, h, wg, wu, wd, tm, tf):
    n = h.shape[0]
    F = wg.shape[1]
    const = lambda i: (0, 0)
    return pl.pallas_call(
        functools.partial(_ffn_body, tf),
        grid=(n // tm,),
        in_specs=[pl.BlockSpec((tm, D), lambda i: (i, 0)),
                  pl.BlockSpec((tm, D), lambda i: (i, 0)),
                  pl.BlockSpec((D, F), const),
                  pl.BlockSpec((D, F), const),
                  pl.BlockSpec((F, D), const)],
        out_specs=pl.BlockSpec((tm, D), lambda i: (i, 0)),
        out_shape=jax.ShapeDtypeStruct((n, D), f32),
        compiler_params=_cparams(("parallel",)),
        name="ffn_dense",
    )(xn, h, wg, wu, wd)


def _dispatch_body(pos_ref, xn_ref, xs_in_ref, xs_ref, sem):
    del xs_in_ref
    i = pl.program_id(1)
    first = jnp.where(i == 0, PAD, 0)

    def copy(r, sl):
        p = pos_ref[0, 0, sl * BLK + r]
        return pltpu.make_async_copy(xn_ref.at[pl.ds(r, 1)], xs_ref.at[pl.ds(p, 1)], sem)

    def issue(r, c):
        copy(r, 0).start()
        copy(r, 1).start()
        return c

    def drain(r, c):
        copy(r, 0).wait()
        copy(r, 1).wait()
        return c

    lax.fori_loop(first, BLK, issue, 0)
    lax.fori_loop(first, BLK, drain, 0)


def _dispatch(xn, pos, xs0, B, nb):
    return pl.pallas_call(
        _dispatch_body,
        grid=(B, nb),
        in_specs=[pl.BlockSpec((1, 1, 2 * BLK), lambda b, i: (b * nb + i, 0, 0), memory_space=pltpu.SMEM),
                  pl.BlockSpec((BLK, D), lambda b, i: (b * nb + i, 0)),
                  pl.BlockSpec(memory_space=pl.ANY)],
        out_specs=pl.BlockSpec(memory_space=pl.ANY),
        out_shape=jax.ShapeDtypeStruct(xs0.shape, f32),
        scratch_shapes=[pltpu.SemaphoreType.DMA(())],
        input_output_aliases={2: 0},
        compiler_params=pltpu.CompilerParams(dimension_semantics=("arbitrary", "arbitrary"),
                                             has_side_effects=True),
        name="moe_dispatch",
    )(pos, xn, xs0)


def _experts_body(nf, te_ref, tv_ref, x_ref, wg_ref, wu_ref, wd_ref, y_ref, xb_ref, acc_ref):
    del te_ref
    t = pl.program_id(0)
    f = pl.program_id(1)
    valid = tv_ref[t] == 1

    @pl.when(valid & (f == 0))
    def _():
        xb_ref[...] = x_ref[...].astype(bf16)
        acc_ref[...] = jnp.zeros_like(acc_ref)

    @pl.when(valid)
    def _():
        x = xb_ref[...]
        g = _dot(x, wg_ref[...])
        u = _dot(x, wu_ref[...])
        a = (g * jax.nn.sigmoid(g) * u).astype(bf16)
        acc_ref[...] += _dot(a, wd_ref[...])

    @pl.when(valid & (f == nf - 1))
    def _():
        y_ref[...] = acc_ref[...]

    @pl.when(jnp.logical_not(valid) & (f == nf - 1))
    def _():
        y_ref[...] = jnp.zeros_like(y_ref)


def _experts(xs, te, tv, wg, wu, wd, tme, tf):
    rows = xs.shape[0]
    T = rows // tme
    nf = D_FF_EXPERT // tf
    grid_spec = pltpu.PrefetchScalarGridSpec(
        num_scalar_prefetch=2,
        grid=(T, nf),
        in_specs=[pl.BlockSpec((tme, D), lambda t, f, te, tv: (t, 0)),
                  pl.BlockSpec((None, D, tf), lambda t, f, te, tv: (te[t], 0, f)),
                  pl.BlockSpec((None, D, tf), lambda t, f, te, tv: (te[t], 0, f)),
                  pl.BlockSpec((None, tf, D), lambda t, f, te, tv: (te[t], f, 0))],
        out_specs=pl.BlockSpec((tme, D), lambda t, f, te, tv: (t, 0)),
        scratch_shapes=[pltpu.VMEM((tme, D), bf16), pltpu.VMEM((tme, D), f32)],
    )
    return pl.pallas_call(
        functools.partial(_experts_body, nf),
        grid_spec=grid_spec,
        out_shape=jax.ShapeDtypeStruct((rows, D), f32),
        compiler_params=_cparams(("arbitrary", "arbitrary")),
        name="moe_experts",
    )(te, tv, xs, wg, wu, wd)


def _combine_body(pos_ref, h_ref, info_ref, ys_ref, o_ref, buf_ref, sem):
    def copy(r, sl):
        p = pos_ref[0, 0, sl * BLK + r]
        return pltpu.make_async_copy(ys_ref.at[pl.ds(p, 1)], buf_ref.at[sl, pl.ds(r, 1)], sem)

    def issue(r, c):
        copy(r, 0).start()
        copy(r, 1).start()
        return c

    def drain(r, c):
        copy(r, 0).wait()
        copy(r, 1).wait()
        return c

    lax.fori_loop(0, BLK, issue, 0)
    lax.fori_loop(0, BLK, drain, 0)
    info = info_ref[...]
    o_ref[...] = h_ref[...] + info[:, 2:3] * buf_ref[0] + info[:, 3:4] * buf_ref[1]


def _combine(h, info, pos, ys, B, nb):
    nt = nb - 1
    return pl.pallas_call(
        _combine_body,
        grid=(B, nt),
        in_specs=[pl.BlockSpec((1, 1, 2 * BLK), lambda b, i: (b * nb + i + 1, 0, 0), memory_space=pltpu.SMEM),
                  pl.BlockSpec((BLK, D), lambda b, i: (b * nb + i + 1, 0)),
                  pl.BlockSpec((BLK, LANES), lambda b, i: (b * nb + i + 1, 0)),
                  pl.BlockSpec(memory_space=pl.ANY)],
        out_specs=pl.BlockSpec((BLK, D), lambda b, i: (b * nt + i, 0)),
        out_shape=jax.ShapeDtypeStruct((B * nt * BLK, D), f32),
        scratch_shapes=[pltpu.VMEM((2, BLK, D), f32), pltpu.SemaphoreType.DMA(())],
        compiler_params=_cparams(("arbitrary", "arbitrary")),
        name="moe_combine",
    )(pos, h, info, ys)


def _strip_body(h_ref, o_ref):
    o_ref[...] = h_ref[...]


def _strip_meta(h, B, nb):
    nt = nb - 1
    return pl.pallas_call(
        _strip_body,
        grid=(B, nt),
        in_specs=[pl.BlockSpec((BLK, D), lambda b, i: (b * nb + i + 1, 0))],
        out_specs=pl.BlockSpec((BLK, D), lambda b, i: (b * nt + i, 0)),
        out_shape=jax.ShapeDtypeStruct((B * nt * BLK, D), f32),
        compiler_params=_cparams(("parallel", "arbitrary")),
        name="strip_meta",
    )(h)


def _bucket_lookup(table, bucket):
    tshape = (table.shape[1],) + (1,) * bucket.ndim
    out = jnp.zeros((table.shape[1],) + bucket.shape, f32)
    for k in range(N_BUCKETS):
        out = out + jnp.where((bucket == k)[None], table[k].astype(f32).reshape(tshape), 0.0)
    return out


def _bias_tables(rel, nb):
    table_a = rel[:, :A_HEADS]
    table_b = rel[:, A_HEADS:]
    a = jnp.arange(BLK)[:, None]
    bb = jnp.arange(2 * BLK)[None, :]
    n = BLK + a - bb
    in_win = (n >= 0) & (n < BLK)
    bias = _bucket_lookup(table_a, _t5_bucket(jnp.maximum(n, 0)))
    variants = []
    for blk in range(3):
        key_ok = (blk - 1) * BLK + bb >= PAD
        variants.append(jnp.where((in_win & key_ok)[None], bias, NEG))
    bias_a = jnp.stack(variants, axis=0)

    d = jnp.arange(nb + 1)[:, None, None]
    kk = jnp.arange(BLK)[None, :, None]
    qq = jnp.arange(BLK)[None, None, :]
    nn = jnp.where(d < nb, d * BLK + qq - kk, -1)
    bias_bt = _bucket_lookup(table_b, _t5_bucket(jnp.maximum(nn, 0)))
    bias_bt = jnp.where((nn >= 0)[None], bias_bt, NEG)
    return bias_a, bias_bt


def _block_diag_ones(gs, size=LANES):
    idx = np.arange(size) // gs
    return jnp.asarray((idx[:, None] == idx[None, :]).astype(np.float32), dtype=bf16)


def _col_scale(gain_a, gain_b):
    one = jnp.ones((LANES,), f32)
    parts = [jnp.tile(gain_a[0], 8) * (64 ** -0.5), jnp.tile(gain_a[1], 2), one,
             jnp.tile(gain_b[1], 8), one, one]
    cs = jnp.concatenate(parts).reshape(1, D_MAIN).astype(f32)
    cst = (jnp.tile(gain_b[0], 8) * (32 ** -0.5)).reshape(256, 1).astype(f32)
    return cs, cst


def _split_in_weight(w):
    main = jnp.concatenate([w[:, 0:768], w[:, 1024:1280], w[:, 1536:1792]], axis=1)
    tr = jnp.concatenate([w[:, 768:1024], w[:, 1280:1536]], axis=1).T
    return main.astype(bf16), tr.astype(bf16)


def _pool_weight(w_pool):
    out = jnp.zeros((256, 256), f32)
    for gi in range(4):
        out = out.at[gi * 64:(gi + 1) * 64, gi * 64:(gi + 1) * 64].set(w_pool[gi])
    return out.astype(bf16)


def _row_tile(n):
    for tm in (512, 256, 128):
        if n % tm == 0:
            return tm
    raise ValueError(n)


def kernel(x, meta_tokens, rel_bias_table, g_mix, w_in, qk_gain_a, sinks, qk_gain_b, diff_lambda, g_diff_out,
           w_pool, pool_scale, w_out, g_ffn, w_ffn_gate, w_ffn_up, w_ffn_down, w_router, w_exp_gate, w_exp_up,
           w_exp_down):
    B, seq, _ = x.shape
    assert seq % BLK == 0
    depth = g_mix.shape[0]
    nb = seq // BLK + 1
    L = nb * BLK
    n = B * L
    tm = _row_tile(n)
    tme = 512
    tf_e = 512
    tf_d = 256
    kb_d = 4

    bias_a, bias_bt = _bias_tables(rel_bias_table, nb)
    g64 = _block_diag_ones(64)
    g32 = _block_diag_ones(32)
    g32t = _block_diag_ones(32, 256)

    h = _embed(x, meta_tokens, B, nb)
    out = None
    for l in range(depth):
        lam_init = 0.8 - 0.6 * math.exp(-0.3 * l)
        cs, cst = _col_scale(qk_gain_a[l], qk_gain_b[l])
        w_main, w_tr = _split_in_weight(w_in[l])
        z, qt, vt = _inproj(h, g_mix[l].reshape(1, D), w_main, w_tr, cs, cst, g64, g32, g32t, tm)
        oa, oc = _swa_pool(z, bias_a, sinks[l], _pool_weight(w_pool[l]), pool_scale[l].reshape(1, 256), B, nb)
        ob = _diff_attn(qt, z.reshape(B, L, D_MAIN), vt, bias_bt, diff_lambda[l],
                        jnp.tile(g_diff_out[l], 2).reshape(BLK, 1), lam_init, B, nb, kb_d)
        wo = w_out[l].astype(bf16)
        gf = g_ffn[l].reshape(1, D)
        j = l // 2
        if l % 2 == 0:
            h, xn = _outproj(oa, ob, oc, wo, h, gf, tm)
            h = _ffn(xn, h, w_ffn_gate[j].astype(bf16), w_ffn_up[j].astype(bf16), w_ffn_down[j].astype(bf16),
                     tm, tf_d)
            out = None
        else:
            assert l == depth - 1, "the expert layer must be the last layer"
            wr = jnp.pad(w_router[j], ((0, 0), (0, LANES - N_EXPERTS)))
            wrh = wr.astype(bf16)
            wrl = (wr - wrh.astype(f32)).astype(bf16)
            tri = jnp.asarray(np.tril(np.ones((tm, tm), np.float32), -1), dtype=bf16)
            h, xn, info, cnt = _outproj(oa, ob, oc, wo, h, gf, tm, router=(wrh, wrl, tri), nb=nb)

            counts = cnt[0, :N_EXPERTS].astype(i32)
            padded = ((counts + tme - 1) // tme) * tme
            ends = jnp.cumsum(padded)
            offs = ends - padded
            e1 = info[:, 0].astype(i32)
            e2 = info[:, 1].astype(i32)
            pos1 = offs[e1] + info[:, 4].astype(i32)
            pos2 = offs[e2] + info[:, 5].astype(i32)
            pos = jnp.concatenate([pos1.reshape(B * nb, 1, BLK), pos2.reshape(B * nb, 1, BLK)], axis=2)
            n_real = B * (seq + N_META) * 2
            T = -(-n_real // tme) + N_EXPERTS
            starts = jnp.arange(T, dtype=i32) * tme
            te = jnp.minimum(jnp.sum((ends[None, :] <= starts[:, None]).astype(i32), axis=1), N_EXPERTS - 1)
            tv = (starts < ends[-1]).astype(i32)

            xs = _dispatch(xn, pos, jnp.zeros((T * tme, D), f32), B, nb)
            ys = _experts(xs, te, tv, w_exp_gate[j].astype(bf16), w_exp_up[j].astype(bf16),
                          w_exp_down[j].astype(bf16), tme, tf_e)
            out = _combine(h, info, pos, ys, B, nb)
    if out is None:
        out = _strip_meta(h, B, nb)
    return out.reshape(B, seq, D)
```

```python
import functools
import math

import jax
import jax.numpy as jnp
import numpy as np
from jax import lax
from jax.experimental import pallas as pl
from jax.experimental.pallas import tpu as pltpu

f32 = jnp.float32
bf16 = jnp.bfloat16
i32 = jnp.int32

D = 1024
BLK = 128
N_META = 16
PAD = BLK - N_META
A_HEADS = 8
B_HEADS = 4
N_BUCKETS = 32
MAX_DISTANCE = 128
N_EXPERTS = 8
D_FF_EXPERT = 3584
EPS = 1e-6
NEG = -1e30
LANES = 128
C_QA, C_KA, C_VA, C_KB, C_UC = 0, 4, 5, 6, 8
D_MAIN = 1280
D_T = 512

VMEM_LIMIT = 56 * 1024 * 1024


def _cparams(sem):
    return pltpu.CompilerParams(dimension_semantics=sem, vmem_limit_bytes=VMEM_LIMIT)


def _t5_bucket(n):
    max_exact = N_BUCKETS // 2
    nf = jnp.maximum(n, 1).astype(f32)
    large = max_exact + (jnp.log(nf / max_exact) / math.log(MAX_DISTANCE / max_exact)
                         * (N_BUCKETS - max_exact)).astype(i32)
    large = jnp.minimum(large, N_BUCKETS - 1)
    return jnp.where(n < max_exact, n, large)


def _dot(a, b):
    return jnp.dot(a, b, preferred_element_type=f32)


def _dot_nt(a, b):
    return lax.dot_general(a, b, (((1,), (1,)), ((), ())), preferred_element_type=f32)


def _swap_halves(x):
    return jnp.concatenate([x[:, 64:], x[:, :64]], axis=1)


def _embed_body(x_ref, m_ref, o_ref):
    i = pl.program_id(1)

    @pl.when(i == 0)
    def _():
        o_ref[0:PAD, :] = jnp.zeros((PAD, D), f32)
        o_ref[PAD:, :] = m_ref[...]

    @pl.when(i > 0)
    def _():
        o_ref[...] = x_ref[...]


def _embed(x, meta, B, nb):
    seq = x.shape[1]
    x2 = x.reshape(B * seq, D)
    return pl.pallas_call(
        _embed_body,
        grid=(B, nb),
        in_specs=[pl.BlockSpec((BLK, D), lambda b, i: (b * (nb - 1) + jnp.maximum(i - 1, 0), 0)),
                  pl.BlockSpec((N_META, D), lambda b, i: (0, 0))],
        out_specs=pl.BlockSpec((BLK, D), lambda b, i: (b * nb + i, 0)),
        out_shape=jax.ShapeDtypeStruct((B * nb * BLK, D), f32),
        compiler_params=_cparams(("parallel", "arbitrary")),
        name="embed",
    )(x2, meta)


def _split_bf16(x):
    hi = x.astype(bf16)
    return hi, (x - hi.astype(f32)).astype(bf16)


def _inproj_body(tm, h_ref, g_ref, w_ref, wt_ref, cs_ref, cst_ref, g64_ref, g32_ref, g32t_ref,
                 z_ref, qt_ref, vt_ref):
    x = h_ref[...]
    ms = jnp.mean(x * x, axis=-1, keepdims=True)
    xn = (x * lax.rsqrt(ms + EPS) * g_ref[...]).astype(bf16)
    for c2 in range(D_MAIN // 256):
        zc2 = _dot(xn, w_ref[:, c2 * 256:(c2 + 1) * 256])
        for half in range(2):
            c = 2 * c2 + half
            zc = zc2[:, half * LANES:(half + 1) * LANES]
            if c < C_VA:
                gm, inv = g64_ref, 1.0 / 64
            elif C_KB <= c < C_UC:
                gm, inv = g32_ref, 1.0 / 32
            else:
                gm = None
            if gm is not None:
                hi, lo = _split_bf16(zc * zc)
                ss = _dot(hi, gm[...]) + _dot(lo, gm[...])
                zc = zc * lax.rsqrt(ss * inv + EPS)
            z_ref[:, c * LANES:(c + 1) * LANES] = (zc * cs_ref[:, c * LANES:(c + 1) * LANES]).astype(bf16)

    zt = _dot_nt(wt_ref[...], xn)
    zq = zt[0:256]
    hi, lo = _split_bf16(zq * zq)
    ss = _dot(g32t_ref[...], hi) + _dot(g32t_ref[...], lo)
    zq = zq * lax.rsqrt(ss * (1.0 / 32) + EPS) * cst_ref[...]
    zv = zt[256:512]
    for k in range(tm // BLK):
        qt_ref[k] = zq[:, k * BLK:(k + 1) * BLK].astype(bf16)
        vt_ref[k] = zv[:, k * BLK:(k + 1) * BLK].astype(bf16)


def _inproj(h, g, w, wt, cs, cst, g64, g32, g32t, tm):
    n = h.shape[0]
    const = lambda i: (0, 0)
    kb = tm // BLK
    return pl.pallas_call(
        functools.partial(_inproj_body, tm),
        grid=(n // tm,),
        in_specs=[pl.BlockSpec((tm, D), lambda i: (i, 0)),
                  pl.BlockSpec((1, D), const),
                  pl.BlockSpec((D, D_MAIN), const),
                  pl.BlockSpec((D_T, D), const),
                  pl.BlockSpec((1, D_MAIN), const),
                  pl.BlockSpec((256, 1), const),
                  pl.BlockSpec((LANES, LANES), const),
                  pl.BlockSpec((LANES, LANES), const),
                  pl.BlockSpec((256, 256), const)],
        out_specs=[pl.BlockSpec((tm, D_MAIN), lambda i: (i, 0)),
                   pl.BlockSpec((kb, 256, BLK), lambda i: (i, 0, 0)),
                   pl.BlockSpec((kb, 256, BLK), lambda i: (i, 0, 0))],
        out_shape=[jax.ShapeDtypeStruct((n, D_MAIN), bf16),
                   jax.ShapeDtypeStruct((n // BLK, 256, BLK), bf16),
                   jax.ShapeDtypeStruct((n // BLK, 256, BLK), bf16)],
        compiler_params=_cparams(("parallel",)),
        name="norm_inproj",
    )(h, g, w, wt, cs, cst, g64, g32, g32t)


def _swa_pool_body(q_ref, kp_ref, kc_ref, vp_ref, vc_ref, up_ref, uc_ref, bias_ref, sink_ref,
                   wp_ref, ps_ref, oa_ref, oc_ref):
    i = pl.program_id(1)
    lane = lax.broadcasted_iota(i32, (1, LANES), 1)
    lo = lane < 64
    k_cat = jnp.concatenate([kp_ref[...], kc_ref[...]], axis=0)
    v_cat = jnp.concatenate([vp_ref[...], vc_ref[...]], axis=0)
    k_sw = _swap_halves(k_cat)
    v_sw = _swap_halves(v_cat)
    zero = jnp.zeros((), bf16)
    for pb in range(4):
        qpb = q_ref[:, pb * LANES:(pb + 1) * LANES]
        grp = pb // 2
        outs = []
        for e in range(2):
            h = 2 * pb + e
            qm = jnp.where(lo if e == 0 else jnp.logical_not(lo), qpb, zero)
            kk = k_cat if e == grp else k_sw
            vv = v_cat if e == grp else v_sw
            s = _dot_nt(qm, kk) + bias_ref[h]
            sink = sink_ref[h]
            m = jnp.maximum(jnp.max(s, axis=-1, keepdims=True), sink)
            p = jnp.exp(s - m)
            l = jnp.sum(p, axis=-1, keepdims=True) + jnp.exp(sink - m)
            outs.append(_dot(p.astype(bf16), vv) * (1.0 / l))
        oa_ref[:, pb * LANES:(pb + 1) * LANES] = jnp.where(lo, outs[0], outs[1]).astype(bf16)

    row = lax.broadcasted_iota(i32, (BLK, 1), 0)
    t = i * BLK + row - PAD
    u_cur = jnp.where(t >= 0, uc_ref[...].astype(f32), 0.0)
    u_prev = jnp.where(i > 0, up_ref[BLK - 16:, :].astype(f32), 0.0)
    ext = jnp.concatenate([u_prev, u_cur], axis=0)
    s2 = ext + pltpu.roll(ext, 1, 0)
    s4 = s2 + pltpu.roll(s2, 2, 0)
    s8 = s4 + pltpu.roll(s4, 4, 0)
    s16 = s8 + pltpu.roll(s8, 8, 0)
    lane2 = lax.broadcasted_iota(i32, (1, 256), 1)
    grp2 = lane2 // 64
    sums = jnp.where(grp2 == 0, s2, jnp.where(grp2 == 1, s4, jnp.where(grp2 == 2, s8, s16)))[16:, :]
    win = jnp.where(grp2 == 0, 2, jnp.where(grp2 == 1, 4, jnp.where(grp2 == 2, 8, 16)))
    cnt = jnp.maximum(jnp.minimum(t + 1, win), 1).astype(f32)
    dlt = (sums / cnt - u_cur).astype(bf16)
    oc_ref[...] = (_dot(dlt, wp_ref[...]) * ps_ref[...]).astype(bf16)


def _swa_pool(z, bias_a, sinks, wpool_bd, pool_scale, B, nb):
    n = z.shape[0]
    cur = lambda c: (lambda b, i: (b * nb + i, c))
    prev = lambda c: (lambda b, i: (b * nb + jnp.maximum(i - 1, 0), c))
    return pl.pallas_call(
        _swa_pool_body,
        grid=(B, nb),
        in_specs=[pl.BlockSpec((BLK, 512), cur(0)),
                  pl.BlockSpec((BLK, LANES), prev(C_KA)),
                  pl.BlockSpec((BLK, LANES), cur(C_KA)),
                  pl.BlockSpec((BLK, LANES), prev(C_VA)),
                  pl.BlockSpec((BLK, LANES), cur(C_VA)),
                  pl.BlockSpec((BLK, 256), prev(C_UC // 2)),
                  pl.BlockSpec((BLK, 256), cur(C_UC // 2)),
                  pl.BlockSpec((None, A_HEADS, BLK, 2 * BLK), lambda b, i: (jnp.minimum(i, 2), 0, 0, 0)),
                  pl.BlockSpec(memory_space=pltpu.SMEM),
                  pl.BlockSpec((256, 256), lambda b, i: (0, 0)),
                  pl.BlockSpec((1, 256), lambda b, i: (0, 0))],
        out_specs=[pl.BlockSpec((BLK, 512), lambda b, i: (b * nb + i, 0)),
                   pl.BlockSpec((BLK, 256), lambda b, i: (b * nb + i, 0))],
        out_shape=[jax.ShapeDtypeStruct((n, 512), bf16), jax.ShapeDtypeStruct((n, 256), bf16)],
        compiler_params=_cparams(("parallel", "arbitrary")),
        name="swa_pool",
    )(z, z, z, z, z, z, z, bias_a, sinks, wpool_bd, pool_scale)


def _diff_attn_body(lam_init, nb, kb_step, qt_ref, k_ref, vt_ref, bias_ref, dl_ref, go_ref, o_ref,
                    acc_ref, m_ref, l_ref, sa_ref, sb_ref):
    i = pl.program_id(1)
    row = lax.broadcasted_iota(i32, (BLK, 1), 0)
    top = row < 64
    zero = jnp.zeros((), bf16)
    qcat = []
    for p in range(2):
        qp = qt_ref[p * BLK:(p + 1) * BLK, :]
        qcat.append([jnp.concatenate(
            [jnp.where((row >= 64 * e + 32 * c) & (row < 64 * e + 32 * c + 32), qp, zero) for c in range(2)],
            axis=1) for e in range(2)])
    rows_k = lax.broadcasted_iota(i32, (kb_step * BLK, 1), 0)
    padmask = jnp.where(rows_k < PAD, NEG, 0.0).astype(f32)

    m_ref[...] = jnp.full(m_ref.shape, 3 * NEG, f32)
    l_ref[...] = jnp.zeros(l_ref.shape, f32)
    acc_ref[...] = jnp.zeros(acc_ref.shape, f32)

    def blocks(s):
        j0 = s * kb_step
        return [jnp.minimum(j0 + kb, nb - 1) for kb in range(kb_step)], j0

    def scores(s, first, s_ref):
        blks, j0 = blocks(s)
        dix = [jnp.where(i - (j0 + kb) >= 0, i - (j0 + kb), nb) for kb in range(kb_step)]
        kparts = [k_ref[pl.ds(pl.multiple_of(b * BLK, BLK), BLK), :] for b in blks]
        for p in range(2):
            kcat = jnp.concatenate([kp[:, p * BLK:(p + 1) * BLK] for kp in kparts], axis=0)
            for e in range(2):
                h = 2 * p + e
                st2 = _dot(kcat, qcat[p][e])
                bias = jnp.concatenate([bias_ref[h, dd] for dd in dix], axis=0)
                if first:
                    bias = bias + padmask
                for c in range(2):
                    s_ref[2 * h + c] = st2[:, c * BLK:(c + 1) * BLK] + bias

    def consume(s, s_ref):
        blks, _ = blocks(s)
        vparts = [vt_ref[b] for b in blks]
        for p in range(2):
            vtc = jnp.concatenate([vp[p * BLK:(p + 1) * BLK, :] for vp in vparts], axis=1)
            lhs = jnp.concatenate([jnp.where(top, vtc, zero), jnp.where(top, zero, vtc)], axis=1)
            for c in range(2):
                pts, alphas = [], []
                for e in range(2):
                    idx = 2 * (2 * p + e) + c
                    st = s_ref[idx]
                    m_old = m_ref[idx:idx + 1, :]
                    m_new = jnp.maximum(m_old, jnp.max(st, axis=0, keepdims=True))
                    alpha = jnp.exp(m_old - m_new)
                    pt = jnp.exp(st - m_new)
                    l_ref[idx:idx + 1, :] = alpha * l_ref[idx:idx + 1, :] + jnp.sum(pt, axis=0, keepdims=True)
                    m_ref[idx:idx + 1, :] = m_new
                    pts.append(pt.astype(bf16))
                    alphas.append(alpha)
                upd = _dot(lhs, jnp.concatenate(pts, axis=0))
                a = 2 * p + c
                acc_ref[a, 0:64, :] = acc_ref[a, 0:64, :] * alphas[0] + upd[0:64]
                acc_ref[a, 64:128, :] = acc_ref[a, 64:128, :] * alphas[1] + upd[64:128]

    n_steps = (i + kb_step) // kb_step
    scores(0, True, sa_ref)

    def body(s, carry):
        @pl.when(s % 2 == 0)
        def _():
            scores(s + 1, False, sb_ref)
            consume(s, sa_ref)

        @pl.when(s % 2 == 1)
        def _():
            scores(s + 1, False, sa_ref)
            consume(s, sb_ref)

        return carry

    lax.fori_loop(0, n_steps - 1, body, 0)
    last = n_steps - 1

    @pl.when(last % 2 == 0)
    def _():
        consume(last, sa_ref)

    @pl.when(last % 2 == 1)
    def _():
        consume(last, sb_ref)

    dl = dl_ref[...]
    lam = (jnp.exp(jnp.sum(dl[0:1] * dl[1:2], axis=-1, keepdims=True))
           - jnp.exp(jnp.sum(dl[2:3] * dl[3:4], axis=-1, keepdims=True)) + lam_init)
    for p in range(2):
        parts = []
        for e in range(2):
            h = 2 * p + e
            rows = slice(64 * e, 64 * e + 64)
            o = (acc_ref[2 * p, rows, :] / l_ref[2 * h:2 * h + 1, :]
                 - lam * (acc_ref[2 * p + 1, rows, :] / l_ref[2 * h + 1:2 * h + 2, :]))
            ssq = jnp.sum(o * o, axis=0, keepdims=True)
            parts.append(o * lax.rsqrt(ssq * (1.0 / 64) + EPS))
        o = jnp.concatenate(parts, axis=0) * go_ref[...] * (1.0 - lam_init)
        o_ref[:, p * BLK:(p + 1) * BLK] = o.T.astype(bf16)


def _diff_attn(qt, z3, vt, bias_bt, dl, go, lam_init, B, nb, kb_step):
    L = nb * BLK
    return pl.pallas_call(
        functools.partial(_diff_attn_body, lam_init, nb, kb_step),
        grid=(B, nb),
        in_specs=[pl.BlockSpec((None, 256, BLK), lambda b, i: (b * nb + i, 0, 0)),
                  pl.BlockSpec((None, L, 256), lambda b, i: (b, 0, C_KB // 2)),
                  pl.BlockSpec((nb, 256, BLK), lambda b, i: (b, 0, 0)),
                  pl.BlockSpec((B_HEADS, nb + 1, BLK, BLK), lambda b, i: (0, 0, 0, 0)),
                  pl.BlockSpec((4, 32), lambda b, i: (0, 0)),
                  pl.BlockSpec((BLK, 1), lambda b, i: (0, 0))],
        out_specs=pl.BlockSpec((BLK, 256), lambda b, i: (b * nb + i, 0)),
        out_shape=jax.ShapeDtypeStruct((B * L, 256), bf16),
        scratch_shapes=[pltpu.VMEM((4, BLK, BLK), f32), pltpu.VMEM((8, BLK), f32), pltpu.VMEM((8, BLK), f32),
                        pltpu.VMEM((8, kb_step * BLK, BLK), f32), pltpu.VMEM((8, kb_step * BLK, BLK), f32)],
        compiler_params=_cparams(("parallel", "arbitrary")),
        name="diff_attn",
    )(qt, z3, vt, bias_bt, dl, go)


def _outproj_common(oa_ref, ob_ref, oc_ref, w_ref, h_ref, g_ref):
    y = (_dot(oa_ref[...], w_ref[0:512, :]) + _dot(ob_ref[...], w_ref[512:768, :])
         + _dot(oc_ref[...], w_ref[768:1024, :]))
    hn = h_ref[...] + y
    ms = jnp.mean(hn * hn, axis=-1, keepdims=True)
    return hn, hn * lax.rsqrt(ms + EPS) * g_ref[...]


def _outproj_body(oa_ref, ob_ref, oc_ref, w_ref, h_ref, g_ref, ho_ref, xn_ref):
    hn, xn = _outproj_common(oa_ref, ob_ref, oc_ref, w_ref, h_ref, g_ref)
    ho_ref[...] = hn
    xn_ref[...] = xn.astype(bf16)


def _outproj_router_body(nb, tm, oa_ref, ob_ref, oc_ref, w_ref, h_ref, g_ref, wrh_ref, wrl_ref, tri_ref,
                         ho_ref, xn_ref, info_ref, cnt_ref):
    step = pl.program_id(0)
    hn, xn = _outproj_common(oa_ref, ob_ref, oc_ref, w_ref, h_ref, g_ref)
    ho_ref[...] = hn
    xn_ref[...] = xn.astype(bf16)

    hi, lo = _split_bf16(xn)
    logits = _dot(hi, wrh_ref[...]) + _dot(lo, wrh_ref[...]) + _dot(hi, wrl_ref[...])
    lane = lax.broadcasted_iota(i32, (1, LANES), 1)
    ninf = -jnp.inf
    lg = jnp.where(lane < N_EXPERTS, logits, ninf)
    m1 = jnp.max(lg, axis=-1, keepdims=True)
    i1 = jnp.min(jnp.where(lg == m1, lane, LANES), axis=-1, keepdims=True)
    lg2 = jnp.where(lane == i1, ninf, lg)
    m2 = jnp.max(lg2, axis=-1, keepdims=True)
    i2 = jnp.min(jnp.where(lg2 == m2, lane, LANES), axis=-1, keepdims=True)
    e21 = jnp.exp(m2 - m1)
    g1 = 1.0 / (1.0 + e21)
    g2 = e21 * g1

    rowin = lax.broadcasted_iota(i32, (BLK, 1), 0)
    valids = []
    for k in range(tm // BLK):
        blk = step * (tm // BLK) + k
        valids.append(jnp.logical_or(blk % nb != 0, rowin >= PAD))
    valid = jnp.concatenate(valids, axis=0)
    oh = jnp.where(valid & ((lane == i1) | (lane == i2)), 1.0, 0.0)

    @pl.when(step == 0)
    def _():
        cnt_ref[...] = jnp.zeros_like(cnt_ref)

    base = cnt_ref[...]
    rank = _dot(tri_ref[...], oh.astype(bf16)) + base
    r1 = jnp.sum(jnp.where(lane == i1, rank, 0.0), axis=-1, keepdims=True)
    r2 = jnp.sum(jnp.where(lane == i2, rank, 0.0), axis=-1, keepdims=True)
    cnt_ref[...] = base + jnp.sum(oh, axis=0, keepdims=True)
    info = jnp.where(lane == 0, i1.astype(f32),
           jnp.where(lane == 1, i2.astype(f32),
           jnp.where(lane == 2, g1,
           jnp.where(lane == 3, g2,
           jnp.where(lane == 4, r1,
           jnp.where(lane == 5, r2, 0.0))))))
    info_ref[...] = info


def _outproj(oa, ob, oc, w, h, g, tm, router=None, nb=None):
    n = h.shape[0]
    row = lambda c: pl.BlockSpec((tm, c), lambda i: (i, 0))
    const = lambda r, c: pl.BlockSpec((r, c), lambda i: (0, 0))
    in_specs = [row(512), row(256), row(256), const(D, D), row(D), const(1, D)]
    if router is None:
        return pl.pallas_call(
            _outproj_body,
            grid=(n // tm,),
            in_specs=in_specs,
            out_specs=[row(D), row(D)],
            out_shape=[jax.ShapeDtypeStruct((n, D), f32), jax.ShapeDtypeStruct((n, D), bf16)],
            compiler_params=_cparams(("parallel",)),
            name="outproj",
        )(oa, ob, oc, w, h, g)
    wrh, wrl, tri = router
    return pl.pallas_call(
        functools.partial(_outproj_router_body, nb, tm),
        grid=(n // tm,),
        in_specs=in_specs + [const(D, LANES), const(D, LANES), const(tm, tm)],
        out_specs=[row(D), row(D), row(LANES), const(1, LANES)],
        out_shape=[jax.ShapeDtypeStruct((n, D), f32), jax.ShapeDtypeStruct((n, D), bf16),
                   jax.ShapeDtypeStruct((n, LANES), f32), jax.ShapeDtypeStruct((1, LANES), f32)],
        compiler_params=_cparams(("arbitrary",)),
        name="outproj_router",
    )(oa, ob, oc, w, h, g, wrh, wrl, tri)


def _ffn_body(tf, x_ref, h_ref, wg_ref, wu_ref, wd_ref, o_ref):
    x = x_ref[...]
    acc = h_ref[...]
    for c in range(wg_ref.shape[1] // tf):
        g = _dot(x, wg_ref[:, c * tf:(c + 1) * tf])
        u = _dot(x, wu_ref[:, c * tf:(c + 1) * tf])
        a = (g * jax.nn.sigmoid(g) * u).astype(bf16)
        acc = acc + _dot(a, wd_ref[c * tf:(c + 1) * tf, :])
    o_ref[...] = acc


def _ffn(xn, h, wg, wu, wd, tm, tf):
    n = h.shape[0]
    F = wg.shape[1]
    const = lambda i: (0, 0)
    return pl.pallas_call(
        functools.partial(_ffn_body, tf),
        grid=(n // tm,),
        in_specs=[pl.BlockSpec((tm, D), lambda i: (i, 0)),
                  pl.BlockSpec((tm, D), lambda i: (i, 0)),
                  pl.BlockSpec((D, F), const),
                  pl.BlockSpec((D, F), const),
                  pl.BlockSpec((F, D), const)],
        out_specs=pl.BlockSpec((tm, D), lambda i: (i, 0)),
        out_shape=jax.ShapeDtypeStruct((n, D), f32),
        compiler_params=_cparams(("parallel",)),
        name="ffn_dense",
    )(xn, h, wg, wu, wd)


HALF = D // 2
ROW_ALIGN = 8
TAB_BASE, TAB_LO, TAB_CNT, TAB_W = 0, N_EXPERTS, 2 * N_EXPERTS, 4 * N_EXPERTS


def _sorted_rows(tb):
    return -(-(2 * tb + N_EXPERTS * (ROW_ALIGN - 1)) // LANES) * LANES


def _pack_pairs(a, b):
    return lax.shift_right_logical(pltpu.bitcast(a, i32), 16) | (pltpu.bitcast(b, i32) & jnp.int32(-65536))


def _unpack_pairs(w):
    return pltpu.bitcast(w << 16, f32), pltpu.bitcast(w & jnp.int32(-65536), f32)


def _for_each_run(tab_ref, tb, fn):
    bits = []
    b = pl.next_power_of_2(2 * tb)
    while b >= ROW_ALIGN:
        bits.append(b)
        b //= 2
    for e in range(N_EXPERTS):
        base = tab_ref[0, 0, TAB_BASE + e]
        lo = tab_ref[0, 0, TAB_LO + e]
        cnt = tab_ref[0, 0, TAB_CNT + e]
        off = 0
        for bit in bits:
            piece = cnt & bit

            @pl.when(piece != 0)
            def _(base=base, lo=lo, off=off, bit=bit):
                fn(pl.multiple_of(base + off, ROW_ALIGN), pl.multiple_of(lo + off, ROW_ALIGN), bit)

            off = off + piece


def _dispatch_body(nblk, tb, tab_ref, tabp_ref, lp_ref, x_ref, xs_in_ref, xs_ref, buf_ref, sem):
    del xs_in_ref
    s = pl.program_id(0)
    slot = s % 2
    lp = lp_ref[0]
    r = lax.broadcasted_iota(i32, (buf_ref.shape[1], 1), 0)
    perm = jnp.where((lp[0:1, :] == r) | (lp[1:2, :] == r), 1.0, 0.0).astype(bf16)
    srt = _dot(perm, x_ref[...])
    buf_ref[slot] = _pack_pairs(srt[:, :HALF], srt[:, HALF:])

    def copy(sl):
        return lambda base, lo, rows: pltpu.make_async_copy(
            buf_ref.at[sl, pl.ds(lo, rows)], xs_ref.at[pl.ds(base, rows)], sem.at[sl])

    _for_each_run(tab_ref, tb, lambda *a: copy(slot)(*a).start())

    @pl.when(s > 0)
    def _():
        _for_each_run(tabp_ref, tb, lambda *a: copy(1 - slot)(*a).wait())

    @pl.when(s == nblk - 1)
    def _():
        _for_each_run(tab_ref, tb, lambda *a: copy(slot)(*a).wait())


def _dispatch(xn, tab, lp_rows, xs0, tb):
    nblk = tab.shape[0]
    return pl.pallas_call(
        functools.partial(_dispatch_body, nblk, tb),
        grid=(nblk,),
        in_specs=[pl.BlockSpec((1, 1, TAB_W), lambda s: (s, 0, 0), memory_space=pltpu.SMEM),
                  pl.BlockSpec((1, 1, TAB_W), lambda s: (jnp.maximum(s - 1, 0), 0, 0), memory_space=pltpu.SMEM),
                  pl.BlockSpec((1, 8, tb), lambda s: (s, 0, 0)),
                  pl.BlockSpec((tb, D), lambda s: (s, 0)),
                  pl.BlockSpec(memory_space=pl.ANY)],
        out_specs=pl.BlockSpec(memory_space=pl.ANY),
        out_shape=jax.ShapeDtypeStruct(xs0.shape, i32),
        scratch_shapes=[pltpu.VMEM((2, _sorted_rows(tb), HALF), i32), pltpu.SemaphoreType.DMA((2,))],
        input_output_aliases={4: 0},
        compiler_params=pltpu.CompilerParams(dimension_semantics=("arbitrary",), has_side_effects=True,
                                             vmem_limit_bytes=VMEM_LIMIT),
        name="moe_dispatch",
    )(tab, tab, lp_rows, xn, xs0)


def _experts_body(nf, te_ref, tv_ref, x_ref, wg_ref, wu_ref, wd_ref, y_ref, xb_ref, acc_ref):
    del te_ref
    t = pl.program_id(0)
    f = pl.program_id(1)
    valid = tv_ref[t] == 1

    @pl.when(valid & (f == 0))
    def _():
        a, b = _unpack_pairs(x_ref[...])
        xb_ref[:, :HALF] = a.astype(bf16)
        xb_ref[:, HALF:] = b.astype(bf16)
        acc_ref[...] = jnp.zeros_like(acc_ref)

    @pl.when(valid)
    def _():
        x = xb_ref[...]
        g = _dot(x, wg_ref[...])
        u = _dot(x, wu_ref[...])
        a = (g * jax.nn.sigmoid(g) * u).astype(bf16)
        acc_ref[...] += _dot(a, wd_ref[...])

    @pl.when(valid & (f == nf - 1))
    def _():
        y = acc_ref[...].astype(bf16).astype(f32)
        y_ref[...] = _pack_pairs(y[:, :HALF], y[:, HALF:])

    @pl.when(jnp.logical_not(valid) & (f == nf - 1))
    def _():
        y_ref[...] = jnp.zeros_like(y_ref)


def _experts(xs, te, tv, wg, wu, wd, tme, tf):
    rows = xs.shape[0]
    T = rows // tme
    nf = D_FF_EXPERT // tf
    grid_spec = pltpu.PrefetchScalarGridSpec(
        num_scalar_prefetch=2,
        grid=(T, nf),
        in_specs=[pl.BlockSpec((tme, HALF), lambda t, f, te, tv: (t, 0)),
                  pl.BlockSpec((None, D, tf), lambda t, f, te, tv: (te[t], 0, f)),
                  pl.BlockSpec((None, D, tf), lambda t, f, te, tv: (te[t], 0, f)),
                  pl.BlockSpec((None, tf, D), lambda t, f, te, tv: (te[t], f, 0))],
        out_specs=pl.BlockSpec((tme, HALF), lambda t, f, te, tv: (t, 0)),
        scratch_shapes=[pltpu.VMEM((tme, D), bf16), pltpu.VMEM((tme, D), f32)],
    )
    return pl.pallas_call(
        functools.partial(_experts_body, nf),
        grid_spec=grid_spec,
        out_shape=jax.ShapeDtypeStruct((rows, HALF), i32),
        compiler_params=_cparams(("arbitrary", "arbitrary")),
        name="moe_experts",
    )(te, tv, xs, wg, wu, wd)


def _combine_body(nblk, tb, nb, tab_ref, tabn_ref, aux_ref, h_ref, ys_ref, o_ref, buf_ref, obuf_ref, sem, osem):
    s = pl.program_id(0)
    slot = s % 2
    sub = tb // BLK

    def copy(sl):
        return lambda base, lo, rows: pltpu.make_async_copy(
            ys_ref.at[pl.ds(base, rows)], buf_ref.at[sl, pl.ds(lo, rows)], sem.at[sl])

    @pl.when(s == 0)
    def _():
        _for_each_run(tab_ref, tb, lambda *a: copy(slot)(*a).start())

    @pl.when(s + 1 < nblk)
    def _():
        _for_each_run(tabn_ref, tb, lambda *a: copy(1 - slot)(*a).start())

    _for_each_run(tab_ref, tb, lambda *a: copy(slot)(*a).wait())

    total = tab_ref[0, 0, TAB_LO + N_EXPERTS - 1] + tab_ref[0, 0, TAB_CNT + N_EXPERTS - 1]
    rows = lax.broadcasted_iota(i32, (buf_ref.shape[1], 1), 0)
    a, b = _unpack_pairs(jnp.where(rows < total, buf_ref[slot], 0))
    ys = jnp.concatenate([a.astype(bf16), b.astype(bf16)], axis=1)
    aux = aux_ref[...]
    lane = lax.broadcasted_iota(i32, (1, buf_ref.shape[1]), 1)
    w = (jnp.where(lane == aux[:, 0:1].astype(i32), aux[:, 2:3], 0.0)
         + jnp.where(lane == aux[:, 1:2].astype(i32), aux[:, 3:4], 0.0))
    hi, lo = _split_bf16(w)
    obuf_ref[slot] = h_ref[...] + _dot(hi, ys) + _dot(lo, ys)

    def out_copies(step, sl, fn):
        for q in range(sub):
            pb = step * sub + q

            @pl.when(pb % nb != 0)
            def _(pb=pb, q=q):
                orow = pl.multiple_of(((pb // nb) * (nb - 1) + pb % nb - 1) * BLK, BLK)
                fn(pltpu.make_async_copy(obuf_ref.at[sl, pl.ds(q * BLK, BLK)], o_ref.at[pl.ds(orow, BLK)],
                                         osem.at[sl]))

    out_copies(s, slot, lambda c: c.start())

    @pl.when(s > 0)
    def _():
        out_copies(s - 1, 1 - slot, lambda c: c.wait())

    @pl.when(s == nblk - 1)
    def _():
        out_copies(s, slot, lambda c: c.wait())


def _combine(h, aux, tab, ys, B, nb, tb):
    nblk = tab.shape[0]
    return pl.pallas_call(
        functools.partial(_combine_body, nblk, tb, nb),
        grid=(nblk,),
        in_specs=[pl.BlockSpec((1, 1, TAB_W), lambda s: (s, 0, 0), memory_space=pltpu.SMEM),
                  pl.BlockSpec((1, 1, TAB_W), lambda s: (jnp.minimum(s + 1, nblk - 1), 0, 0),
                               memory_space=pltpu.SMEM),
                  pl.BlockSpec((tb, LANES), lambda s: (s, 0)),
                  pl.BlockSpec((tb, D), lambda s: (s, 0)),
                  pl.BlockSpec(memory_space=pl.ANY)],
        out_specs=pl.BlockSpec(memory_space=pl.ANY),
        out_shape=jax.ShapeDtypeStruct((B * (nb - 1) * BLK, D), f32),
        scratch_shapes=[pltpu.VMEM((2, _sorted_rows(tb), HALF), i32), pltpu.VMEM((2, tb, D), f32),
                        pltpu.SemaphoreType.DMA((2,)), pltpu.SemaphoreType.DMA((2,))],
        compiler_params=_cparams(("arbitrary",)),
        name="moe_combine",
    )(tab, tab, aux, h, ys)


def _routing_tables(info, cnt_rows, B, nb, tb, tme):
    n = info.shape[0]
    nblk = n // tb
    ar = jnp.arange(N_EXPERTS, dtype=i32)
    e1 = info[:, 0].astype(i32)
    e2 = info[:, 1].astype(i32)
    blk = jnp.arange(n // BLK, dtype=i32)[:, None]
    rowin = jnp.arange(BLK, dtype=i32)[None, :]
    real = ((blk % nb != 0) | (rowin >= PAD)).reshape(n)
    oh1 = (e1[:, None] == ar) & real[:, None]
    oh2 = (e2[:, None] == ar) & real[:, None]
    blk_cnt = (oh1 | oh2).astype(i32).reshape(nblk, tb, N_EXPERTS).sum(axis=1)
    blk_rank0 = jnp.cumsum(blk_cnt, axis=0) - blk_cnt
    run = ((blk_cnt + ROW_ALIGN - 1) // ROW_ALIGN) * ROW_ALIGN
    run0 = jnp.cumsum(run, axis=0) - run
    seg = ((jnp.sum(run, axis=0) + tme - 1) // tme) * tme
    ends = jnp.cumsum(seg)
    offs = ends - seg
    base = offs[None, :] + run0
    lo = jnp.cumsum(run, axis=1) - run
    tab = jnp.concatenate([base, lo, run, jnp.zeros_like(base)], axis=1).reshape(nblk, 1, TAB_W)

    shift = jnp.repeat(lo - blk_rank0, tb, axis=0)
    lp1 = jnp.where(real, jnp.sum(jnp.where(oh1, shift, 0), axis=1) + info[:, 4].astype(i32), -1)
    lp2 = jnp.where(real, jnp.sum(jnp.where(oh2, shift, 0), axis=1) + info[:, 5].astype(i32), -1)
    lp_rows = jnp.concatenate([lp1.reshape(nblk, 1, tb), lp2.reshape(nblk, 1, tb),
                               jnp.full((nblk, 6, tb), -1, i32)], axis=1)
    aux = jnp.concatenate([lp1.astype(f32)[:, None], lp2.astype(f32)[:, None], info[:, 2:4],
                           jnp.zeros((n, LANES - 4), f32)], axis=1)
    del cnt_rows
    return tab, lp_rows, aux, ends


def _strip_body(h_ref, o_ref):
    o_ref[...] = h_ref[...]


def _strip_meta(h, B, nb):
    nt = nb - 1
    return pl.pallas_call(
        _strip_body,
        grid=(B, nt),
        in_specs=[pl.BlockSpec((BLK, D), lambda b, i: (b * nb + i + 1, 0))],
        out_specs=pl.BlockSpec((BLK, D), lambda b, i: (b * nt + i, 0)),
        out_shape=jax.ShapeDtypeStruct((B * nt * BLK, D), f32),
        compiler_params=_cparams(("parallel", "arbitrary")),
        name="strip_meta",
    )(h)


def _bucket_lookup(table, bucket):
    tshape = (table.shape[1],) + (1,) * bucket.ndim
    out = jnp.zeros((table.shape[1],) + bucket.shape, f32)
    for k in range(N_BUCKETS):
        out = out + jnp.where((bucket == k)[None], table[k].astype(f32).reshape(tshape), 0.0)
    return out


def _bias_tables(rel, nb):
    table_a = rel[:, :A_HEADS]
    table_b = rel[:, A_HEADS:]
    a = jnp.arange(BLK)[:, None]
    bb = jnp.arange(2 * BLK)[None, :]
    n = BLK + a - bb
    in_win = (n >= 0) & (n < BLK)
    bias = _bucket_lookup(table_a, _t5_bucket(jnp.maximum(n, 0)))
    variants = []
    for blk in range(3):
        key_ok = (blk - 1) * BLK + bb >= PAD
        variants.append(jnp.where((in_win & key_ok)[None], bias, NEG))
    bias_a = jnp.stack(variants, axis=0)

    d = jnp.arange(nb + 1)[:, None, None]
    kk = jnp.arange(BLK)[None, :, None]
    qq = jnp.arange(BLK)[None, None, :]
    nn = jnp.where(d < nb, d * BLK + qq - kk, -1)
    bias_bt = _bucket_lookup(table_b, _t5_bucket(jnp.maximum(nn, 0)))
    bias_bt = jnp.where((nn >= 0)[None], bias_bt, NEG)
    return bias_a, bias_bt


def _block_diag_ones(gs, size=LANES):
    idx = np.arange(size) // gs
    return jnp.asarray((idx[:, None] == idx[None, :]).astype(np.float32), dtype=bf16)


def _col_scale(gain_a, gain_b):
    one = jnp.ones((LANES,), f32)
    parts = [jnp.tile(gain_a[0], 8) * (64 ** -0.5), jnp.tile(gain_a[1], 2), one,
             jnp.tile(gain_b[1], 8), one, one]
    cs = jnp.concatenate(parts).reshape(1, D_MAIN).astype(f32)
    cst = (jnp.tile(gain_b[0], 8) * (32 ** -0.5)).reshape(256, 1).astype(f32)
    return cs, cst


def _split_in_weight(w):
    main = jnp.concatenate([w[:, 0:768], w[:, 1024:1280], w[:, 1536:1792]], axis=1)
    tr = jnp.concatenate([w[:, 768:1024], w[:, 1280:1536]], axis=1).T
    return main.astype(bf16), tr.astype(bf16)


def _pool_weight(w_pool):
    out = jnp.zeros((256, 256), f32)
    for gi in range(4):
        out = out.at[gi * 64:(gi + 1) * 64, gi * 64:(gi + 1) * 64].set(w_pool[gi])
    return out.astype(bf16)


def _row_tile(n):
    for tm in (512, 256, 128):
        if n % tm == 0:
            return tm
    raise ValueError(n)


def kernel(x, meta_tokens, rel_bias_table, g_mix, w_in, qk_gain_a, sinks, qk_gain_b, diff_lambda, g_diff_out,
           w_pool, pool_scale, w_out, g_ffn, w_ffn_gate, w_ffn_up, w_ffn_down, w_router, w_exp_gate, w_exp_up,
           w_exp_down):
    B, seq, _ = x.shape
    assert seq % BLK == 0
    depth = g_mix.shape[0]
    nb = seq // BLK + 1
    L = nb * BLK
    n = B * L
    tm = _row_tile(n)
    tme = 512
    tf_e = 896
    tf_d = 256
    kb_d = 4

    bias_a, bias_bt = _bias_tables(rel_bias_table, nb)
    g64 = _block_diag_ones(64)
    g32 = _block_diag_ones(32)
    g32t = _block_diag_ones(32, 256)

    h = _embed(x, meta_tokens, B, nb)
    out = None
    for l in range(depth):
        lam_init = 0.8 - 0.6 * math.exp(-0.3 * l)
        cs, cst = _col_scale(qk_gain_a[l], qk_gain_b[l])
        w_main, w_tr = _split_in_weight(w_in[l])
        z, qt, vt = _inproj(h, g_mix[l].reshape(1, D), w_main, w_tr, cs, cst, g64, g32, g32t, tm)
        oa, oc = _swa_pool(z, bias_a, sinks[l], _pool_weight(w_pool[l]), pool_scale[l].reshape(1, 256), B, nb)
        ob = _diff_attn(qt, z.reshape(B, L, D_MAIN), vt, bias_bt, diff_lambda[l],
                        jnp.tile(g_diff_out[l], 2).reshape(BLK, 1), lam_init, B, nb, kb_d)
        wo = w_out[l].astype(bf16)
        gf = g_ffn[l].reshape(1, D)
        j = l // 2
        if l % 2 == 0:
            h, xn = _outproj(oa, ob, oc, wo, h, gf, tm)
            h = _ffn(xn, h, w_ffn_gate[j].astype(bf16), w_ffn_up[j].astype(bf16), w_ffn_down[j].astype(bf16),
                     tm, tf_d)
            out = None
        else:
            assert l == depth - 1, "the expert layer must be the last layer"
            wr = jnp.pad(w_router[j], ((0, 0), (0, LANES - N_EXPERTS)))
            wrh = wr.astype(bf16)
            wrl = (wr - wrh.astype(f32)).astype(bf16)
            tri = jnp.asarray(np.tril(np.ones((tm, tm), np.float32), -1), dtype=bf16)
            h, xn, info, cnt = _outproj(oa, ob, oc, wo, h, gf, tm, router=(wrh, wrl, tri), nb=nb)

            tab, lp_rows, aux, ends = _routing_tables(info, cnt, B, nb, tm, tme)
            rows_max = B * (seq + N_META) * 2 + (n // tm) * N_EXPERTS * (ROW_ALIGN - 1) + N_EXPERTS * (tme - 1)
            T = -(-rows_max // tme)
            starts = jnp.arange(T, dtype=i32) * tme
            te = jnp.minimum(jnp.sum((ends[None, :] <= starts[:, None]).astype(i32), axis=1), N_EXPERTS - 1)
            tv = (starts < ends[-1]).astype(i32)

            xs = _dispatch(xn, tab, lp_rows, jnp.zeros((T * tme, HALF), i32), tm)
            ys = _experts(xs, te, tv, w_exp_gate[j].astype(bf16), w_exp_up[j].astype(bf16),
                          w_exp_down[j].astype(bf16), tme, tf_e)
            out = _combine(h, aux, tab, ys, B, nb, tm)
    if out is None:
        out = _strip_meta(h, B, nb)
    return out.reshape(B, seq, D)
```

```python
import functools
import math

import jax
import jax.numpy as jnp
import numpy as np
from jax import lax
from jax.experimental import pallas as pl
from jax.experimental.pallas import tpu as pltpu

f32 = jnp.float32
bf16 = jnp.bfloat16
i32 = jnp.int32

D = 1024
BLK = 128
N_META = 16
PAD = BLK - N_META
A_HEADS = 8
B_HEADS = 4
N_BUCKETS = 32
MAX_DISTANCE = 128
N_EXPERTS = 8
D_FF_EXPERT = 3584
EPS = 1e-6
NEG = -1e30
LANES = 128
D_ROW = 640
D_FM = 1152
R_QB, R_VB, R_VA = 512, 768, 1024
LOG2E = 1.4426950408889634

VMEM_LIMIT = 56 * 1024 * 1024


def _cparams(sem):
    return pltpu.CompilerParams(dimension_semantics=sem, vmem_limit_bytes=VMEM_LIMIT)


def _t5_bucket(n):
    max_exact = N_BUCKETS // 2
    large = max_exact
    for j in range(1, N_BUCKETS - max_exact):
        thr = math.ceil(max_exact * (MAX_DISTANCE / max_exact) ** (j / (N_BUCKETS - max_exact)))
        large = large + (n >= thr).astype(i32)
    return jnp.where(n < max_exact, n, large)


def _dot(a, b):
    return jnp.dot(a, b, preferred_element_type=f32)


def _dot_nt(a, b):
    return lax.dot_general(a, b, (((1,), (1,)), ((), ())), preferred_element_type=f32)


def _embed_body(x_ref, m_ref, o_ref):
    o_ref[0:PAD, :] = jnp.zeros((PAD, D), f32)
    o_ref[PAD:BLK, :] = m_ref[...]
    o_ref[BLK:, :] = x_ref[...]


def _embed(x, meta, B, nb):
    seq = x.shape[1]
    return pl.pallas_call(
        _embed_body,
        grid=(B,),
        in_specs=[pl.BlockSpec((None, seq, D), lambda b: (b, 0, 0)),
                  pl.BlockSpec((N_META, D), lambda b: (0, 0))],
        out_specs=pl.BlockSpec((nb * BLK, D), lambda b: (b, 0)),
        out_shape=jax.ShapeDtypeStruct((B * nb * BLK, D), f32),
        compiler_params=_cparams(("parallel",)),
        name="embed",
    )(x, meta)


def _split_bf16(x):
    hi = x.astype(bf16)
    return hi, (x - hi.astype(f32)).astype(bf16)


def _inproj_body(tm, h_ref, g_ref, w_ref, wt_ref, cs_ref, cst_ref, g64_ref, g32_ref, z_ref, ft_ref):
    x = h_ref[...]
    ms = jnp.mean(x * x, axis=-1, keepdims=True)
    xn = (x * lax.rsqrt(ms + EPS) * g_ref[...]).astype(bf16)

    for c in range(D_ROW // LANES):
        zc = _dot(xn, w_ref[:, c * LANES:(c + 1) * LANES])
        if c >= 2:
            gm, inv = (g32_ref, 1.0 / 32) if c < 4 else (g64_ref, 1.0 / 64)
            ss = _dot((zc * zc).astype(bf16), gm[...])
            zc = zc * lax.rsqrt(ss * inv + EPS)
        z_ref[:, c * LANES:(c + 1) * LANES] = (zc * cs_ref[:, c * LANES:(c + 1) * LANES]).astype(bf16)

    zt_all = _dot_nt(wt_ref[...], xn)
    for c in range(D_FM // LANES):
        zt = zt_all[c * LANES:(c + 1) * LANES]
        if c < R_VB // LANES:
            gm, inv = (g64_ref, 1.0 / 64) if c < R_QB // LANES else (g32_ref, 1.0 / 32)
            ss = _dot(gm[...], (zt * zt).astype(bf16))
            zt = zt * lax.rsqrt(ss * inv + EPS) * cst_ref[c * LANES:(c + 1) * LANES, :]
        for k in range(tm // BLK):
            ft_ref[k, c * LANES:(c + 1) * LANES, :] = zt[:, k * BLK:(k + 1) * BLK].astype(bf16)


def _inproj(h, g, w, wt, cs, cst, g64, g32, tm):
    n = h.shape[0]
    const = lambda i: (0, 0)
    kb = tm // BLK
    return pl.pallas_call(
        functools.partial(_inproj_body, tm),
        grid=(n // tm,),
        in_specs=[pl.BlockSpec((tm, D), lambda i: (i, 0)),
                  pl.BlockSpec((1, D), const),
                  pl.BlockSpec((D, D_ROW), const),
                  pl.BlockSpec((D_FM, D), const),
                  pl.BlockSpec((1, D_ROW), const),
                  pl.BlockSpec((R_VB, 1), const),
                  pl.BlockSpec((LANES, LANES), const),
                  pl.BlockSpec((LANES, LANES), const)],
        out_specs=[pl.BlockSpec((tm, D_ROW), lambda i: (i, 0)),
                   pl.BlockSpec((kb, D_FM, BLK), lambda i: (i, 0, 0))],
        out_shape=[jax.ShapeDtypeStruct((n, D_ROW), bf16),
                   jax.ShapeDtypeStruct((n // BLK, D_FM, BLK), bf16)],
        compiler_params=_cparams(("parallel",)),
        name="norm_inproj",
    )(h, g, w, wt, cs, cst, g64, g32)


def _swa_pool_body(qt_ref, kp_ref, kc_ref, vtp_ref, vtc_ref, up_ref, uc_ref, bias_ref, sink_ref,
                   wp_ref, ps_ref, oa_ref, oc_ref):
    i = pl.program_id(1)
    kcat = jnp.concatenate([kp_ref[...], kc_ref[...]], axis=0)
    vt = jnp.concatenate([vtp_ref[...], vtc_ref[...]], axis=1)
    zeros = jnp.zeros((64, BLK), bf16)
    for grp in range(2):
        pts, inv_ls = [], []
        for hh in range(4):
            h = 4 * grp + hh
            qrows = qt_ref[64 * h:64 * h + 64, :]
            qmt = jnp.concatenate([qrows, zeros] if grp == 0 else [zeros, qrows], axis=0)
            st = _dot(kcat, qmt) + bias_ref[h]
            sink = sink_ref[h] * LOG2E
            m = jnp.maximum(jnp.max(st, axis=0, keepdims=True), sink)
            p = jnp.exp2(st - m)
            inv_ls.append(1.0 / (jnp.sum(p, axis=0, keepdims=True) + jnp.exp2(sink - m)))
            pts.append(p.astype(bf16))
        ot = _dot(vt[64 * grp:64 * grp + 64, :], jnp.concatenate(pts, axis=1))
        for pair in range(2):
            o2 = jnp.concatenate([ot[:, (2 * pair + k) * BLK:(2 * pair + k + 1) * BLK] * inv_ls[2 * pair + k]
                                  for k in range(2)], axis=0)
            pb = 2 * grp + pair
            oa_ref[:, pb * LANES:(pb + 1) * LANES] = o2.T.astype(bf16)

    row = lax.broadcasted_iota(i32, (BLK, 1), 0)
    t = i * BLK + row - PAD
    u_cur = jnp.where(t >= 0, uc_ref[...].astype(f32), 0.0)
    u_prev = jnp.where(i > 0, up_ref[BLK - 16:, :].astype(f32), 0.0)
    ext = jnp.concatenate([u_prev, u_cur], axis=0)
    s2 = ext + pltpu.roll(ext, 1, 0)
    s4 = s2 + pltpu.roll(s2, 2, 0)
    s8 = s4 + pltpu.roll(s4, 4, 0)
    s16 = s8 + pltpu.roll(s8, 8, 0)
    lane2 = lax.broadcasted_iota(i32, (1, 256), 1)
    grp2 = lane2 // 64
    sums = jnp.where(grp2 == 0, s2, jnp.where(grp2 == 1, s4, jnp.where(grp2 == 2, s8, s16)))[16:, :]
    win = jnp.where(grp2 == 0, 2, jnp.where(grp2 == 1, 4, jnp.where(grp2 == 2, 8, 16)))
    cnt = jnp.maximum(jnp.minimum(t + 1, win), 1).astype(f32)
    dlt = (sums / cnt - u_cur).astype(bf16)
    oc_ref[...] = (_dot(dlt, wp_ref[...]) * ps_ref[...]).astype(bf16)


def _swa_pool(z, ft, bias_at, sinks, wpool_bd, pool_scale, B, nb):
    n = z.shape[0]
    cur = lambda c: (lambda b, i: (b * nb + i, c))
    prev = lambda c: (lambda b, i: (b * nb + jnp.maximum(i - 1, 0), c))
    return pl.pallas_call(
        _swa_pool_body,
        grid=(B, nb),
        in_specs=[pl.BlockSpec((None, 512, BLK), lambda b, i: (b * nb + i, 0, 0)),
                  pl.BlockSpec((BLK, LANES), prev(4)),
                  pl.BlockSpec((BLK, LANES), cur(4)),
                  pl.BlockSpec((None, BLK, BLK), lambda b, i: (b * nb + jnp.maximum(i - 1, 0), R_VA // BLK, 0)),
                  pl.BlockSpec((None, BLK, BLK), lambda b, i: (b * nb + i, R_VA // BLK, 0)),
                  pl.BlockSpec((BLK, 256), prev(0)),
                  pl.BlockSpec((BLK, 256), cur(0)),
                  pl.BlockSpec((None, A_HEADS, 2 * BLK, BLK), lambda b, i: (jnp.minimum(i, 2), 0, 0, 0)),
                  pl.BlockSpec(memory_space=pltpu.SMEM),
                  pl.BlockSpec((256, 256), lambda b, i: (0, 0)),
                  pl.BlockSpec((1, 256), lambda b, i: (0, 0))],
        out_specs=[pl.BlockSpec((BLK, 512), lambda b, i: (b * nb + i, 0)),
                   pl.BlockSpec((BLK, 256), lambda b, i: (b * nb + i, 0))],
        out_shape=[jax.ShapeDtypeStruct((n, 512), bf16), jax.ShapeDtypeStruct((n, 256), bf16)],
        compiler_params=_cparams(("parallel", "arbitrary")),
        name="swa_pool",
    )(ft, z, z, ft, ft, z, z, bias_at, sinks, wpool_bd, pool_scale)


def _diff_attn_body(lam_init, nb, kb_step, qt_ref, k_ref, vt_ref, bias_ref, dl_ref, go_ref, o_ref,
                    acc_ref, m_ref, l_ref, sa_ref, sb_ref):
    i = pl.program_id(1)
    row = lax.broadcasted_iota(i32, (BLK, 1), 0)
    top = row < 64
    zero = jnp.zeros((), bf16)
    qcat = []
    for p in range(2):
        qp = qt_ref[p * BLK:(p + 1) * BLK, :]
        qcat.append([jnp.concatenate(
            [jnp.where((row >= 64 * e + 32 * c) & (row < 64 * e + 32 * c + 32), qp, zero) for c in range(2)],
            axis=1) for e in range(2)])
    rows_k = lax.broadcasted_iota(i32, (kb_step * BLK, 1), 0)
    padmask = jnp.where(rows_k < PAD, NEG, 0.0).astype(f32)

    m_ref[...] = jnp.full(m_ref.shape, 3 * NEG, f32)
    l_ref[...] = jnp.zeros(l_ref.shape, f32)
    acc_ref[...] = jnp.zeros(acc_ref.shape, f32)

    def blocks(s):
        j0 = s * kb_step
        return [jnp.minimum(j0 + kb, nb - 1) for kb in range(kb_step)], j0

    def scores(s, first, s_ref):
        blks, j0 = blocks(s)
        dix = [jnp.where(i - (j0 + kb) >= 0, i - (j0 + kb), nb) for kb in range(kb_step)]
        kparts = [k_ref[pl.ds(pl.multiple_of(b * BLK, BLK), BLK), :] for b in blks]
        for p in range(2):
            kcat = jnp.concatenate([kp[:, p * BLK:(p + 1) * BLK] for kp in kparts], axis=0)
            for e in range(2):
                h = 2 * p + e
                st2 = _dot(kcat, qcat[p][e])
                bias = jnp.concatenate([bias_ref[h, dd] for dd in dix], axis=0)
                if first:
                    bias = bias + padmask
                for c in range(2):
                    s_ref[2 * h + c] = st2[:, c * BLK:(c + 1) * BLK] + bias

    def consume(s, s_ref):
        blks, _ = blocks(s)
        vparts = [vt_ref[b] for b in blks]
        for p in range(2):
            vtc = jnp.concatenate([vp[p * BLK:(p + 1) * BLK, :] for vp in vparts], axis=1)
            lhs = jnp.concatenate([jnp.where(top, vtc, zero), jnp.where(top, zero, vtc)], axis=1)
            for c in range(2):
                pts, alphas = [], []
                for e in range(2):
                    idx = 2 * (2 * p + e) + c
                    st = s_ref[idx]
                    m_old = m_ref[idx:idx + 1, :]
                    m_new = jnp.maximum(m_old, jnp.max(st, axis=0, keepdims=True))
                    alpha = jnp.exp2(m_old - m_new)
                    pt = jnp.exp2(st - m_new)
                    l_ref[idx:idx + 1, :] = alpha * l_ref[idx:idx + 1, :] + jnp.sum(pt, axis=0, keepdims=True)
                    m_ref[idx:idx + 1, :] = m_new
                    pts.append(pt.astype(bf16))
                    alphas.append(alpha)
                upd = _dot(lhs, jnp.concatenate(pts, axis=0))
                a = 2 * p + c
                acc_ref[a, 0:64, :] = acc_ref[a, 0:64, :] * alphas[0] + upd[0:64]
                acc_ref[a, 64:128, :] = acc_ref[a, 64:128, :] * alphas[1] + upd[64:128]

    n_steps = (i + kb_step) // kb_step
    scores(0, True, sa_ref)

    def body(s, carry):
        @pl.when(s % 2 == 0)
        def _():
            scores(s + 1, False, sb_ref)
            consume(s, sa_ref)

        @pl.when(s % 2 == 1)
        def _():
            scores(s + 1, False, sa_ref)
            consume(s, sb_ref)

        return carry

    lax.fori_loop(0, n_steps - 1, body, 0)
    last = n_steps - 1

    @pl.when(last % 2 == 0)
    def _():
        consume(last, sa_ref)

    @pl.when(last % 2 == 1)
    def _():
        consume(last, sb_ref)

    dl = dl_ref[...]
    lam = (jnp.exp(jnp.sum(dl[0:1] * dl[1:2], axis=-1, keepdims=True))
           - jnp.exp(jnp.sum(dl[2:3] * dl[3:4], axis=-1, keepdims=True)) + lam_init)
    for p in range(2):
        parts = []
        for e in range(2):
            h = 2 * p + e
            rows = slice(64 * e, 64 * e + 64)
            o = (acc_ref[2 * p, rows, :] / l_ref[2 * h:2 * h + 1, :]
                 - lam * (acc_ref[2 * p + 1, rows, :] / l_ref[2 * h + 1:2 * h + 2, :]))
            ssq = jnp.sum(o * o, axis=0, keepdims=True)
            parts.append(o * lax.rsqrt(ssq * (1.0 / 64) + EPS))
        o = jnp.concatenate(parts, axis=0) * go_ref[...] * (1.0 - lam_init)
        o_ref[:, p * BLK:(p + 1) * BLK] = o.T.astype(bf16)


def _diff_attn(ft, z3, bias_bt, dl, go, lam_init, B, nb, kb_step):
    L = nb * BLK
    return pl.pallas_call(
        functools.partial(_diff_attn_body, lam_init, nb, kb_step),
        grid=(B, nb),
        in_specs=[pl.BlockSpec((None, 256, BLK), lambda b, i: (b * nb + i, R_QB // 256, 0)),
                  pl.BlockSpec((None, L, 256), lambda b, i: (b, 0, 1)),
                  pl.BlockSpec((nb, 256, BLK), lambda b, i: (b, R_VB // 256, 0)),
                  pl.BlockSpec((B_HEADS, nb + 1, BLK, BLK), lambda b, i: (0, 0, 0, 0)),
                  pl.BlockSpec((4, 32), lambda b, i: (0, 0)),
                  pl.BlockSpec((BLK, 1), lambda b, i: (0, 0))],
        out_specs=pl.BlockSpec((BLK, 256), lambda b, i: (b * nb + i, 0)),
        out_shape=jax.ShapeDtypeStruct((B * L, 256), bf16),
        scratch_shapes=[pltpu.VMEM((4, BLK, BLK), f32), pltpu.VMEM((8, BLK), f32), pltpu.VMEM((8, BLK), f32),
                        pltpu.VMEM((8, kb_step * BLK, BLK), f32), pltpu.VMEM((8, kb_step * BLK, BLK), f32)],
        compiler_params=_cparams(("parallel", "arbitrary")),
        name="diff_attn",
    )(ft, z3, ft, bias_bt, dl, go)


def _outproj_common(oa_ref, ob_ref, oc_ref, w_ref, h_ref, g_ref):
    y = (_dot(oa_ref[...], w_ref[0:512, :]) + _dot(ob_ref[...], w_ref[512:768, :])
         + _dot(oc_ref[...], w_ref[768:1024, :]))
    hn = h_ref[...] + y
    ms = jnp.mean(hn * hn, axis=-1, keepdims=True)
    return hn, hn * lax.rsqrt(ms + EPS) * g_ref[...]


def _outproj_body(oa_ref, ob_ref, oc_ref, w_ref, h_ref, g_ref, ho_ref, xn_ref):
    hn, xn = _outproj_common(oa_ref, ob_ref, oc_ref, w_ref, h_ref, g_ref)
    ho_ref[...] = hn
    xn_ref[...] = xn.astype(bf16)


def _outproj_router_body(nb, tm, oa_ref, ob_ref, oc_ref, w_ref, h_ref, g_ref, wrh_ref, wrl_ref, tri_ref,
                         ho_ref, xn_ref, info_ref, cnt_ref):
    step = pl.program_id(0)
    hn, xn = _outproj_common(oa_ref, ob_ref, oc_ref, w_ref, h_ref, g_ref)
    ho_ref[...] = hn
    xn_ref[...] = xn.astype(bf16)

    hi, lo = _split_bf16(xn)
    logits = _dot(hi, wrh_ref[...]) + _dot(lo, wrh_ref[...]) + _dot(hi, wrl_ref[...])
    lane = lax.broadcasted_iota(i32, (1, LANES), 1)
    ninf = -jnp.inf
    lg = jnp.where(lane < N_EXPERTS, logits, ninf)
    m1 = jnp.max(lg, axis=-1, keepdims=True)
    i1 = jnp.min(jnp.where(lg == m1, lane, LANES), axis=-1, keepdims=True)
    lg2 = jnp.where(lane == i1, ninf, lg)
    m2 = jnp.max(lg2, axis=-1, keepdims=True)
    i2 = jnp.min(jnp.where(lg2 == m2, lane, LANES), axis=-1, keepdims=True)
    e21 = jnp.exp(m2 - m1)
    g1 = 1.0 / (1.0 + e21)
    g2 = e21 * g1

    rowin = lax.broadcasted_iota(i32, (BLK, 1), 0)
    valids = []
    for k in range(tm // BLK):
        blk = step * (tm // BLK) + k
        valids.append(jnp.logical_or(blk % nb != 0, rowin >= PAD))
    valid = jnp.concatenate(valids, axis=0)
    oh = jnp.where(valid & ((lane == i1) | (lane == i2)), 1.0, 0.0)

    @pl.when(step == 0)
    def _():
        cnt_ref[...] = jnp.zeros_like(cnt_ref)

    base = cnt_ref[...]
    rank = _dot(tri_ref[...], oh.astype(bf16)) + base
    r1 = jnp.sum(jnp.where(lane == i1, rank, 0.0), axis=-1, keepdims=True)
    r2 = jnp.sum(jnp.where(lane == i2, rank, 0.0), axis=-1, keepdims=True)
    cnt_ref[...] = base + jnp.sum(oh, axis=0, keepdims=True)
    info = jnp.where(lane == 0, i1.astype(f32),
           jnp.where(lane == 1, i2.astype(f32),
           jnp.where(lane == 2, g1,
           jnp.where(lane == 3, g2,
           jnp.where(lane == 4, r1,
           jnp.where(lane == 5, r2, 0.0))))))
    info_ref[...] = info


def _outproj(oa, ob, oc, w, h, g, tm, router=None, nb=None):
    n = h.shape[0]
    row = lambda c: pl.BlockSpec((tm, c), lambda i: (i, 0))
    const = lambda r, c: pl.BlockSpec((r, c), lambda i: (0, 0))
    in_specs = [row(512), row(256), row(256), const(D, D), row(D), const(1, D)]
    if router is None:
        return pl.pallas_call(
            _outproj_body,
            grid=(n // tm,),
            in_specs=in_specs,
            out_specs=[row(D), row(D)],
            out_shape=[jax.ShapeDtypeStruct((n, D), f32), jax.ShapeDtypeStruct((n, D), bf16)],
            compiler_params=_cparams(("parallel",)),
            name="outproj",
        )(oa, ob, oc, w, h, g)
    wrh, wrl, tri = router
    return pl.pallas_call(
        functools.partial(_outproj_router_body, nb, tm),
        grid=(n // tm,),
        in_specs=in_specs + [const(D, LANES), const(D, LANES), const(tm, tm)],
        out_specs=[row(D), row(D), row(LANES), const(1, LANES)],
        out_shape=[jax.ShapeDtypeStruct((n, D), f32), jax.ShapeDtypeStruct((n, D), bf16),
                   jax.ShapeDtypeStruct((n, LANES), f32), jax.ShapeDtypeStruct((1, LANES), f32)],
        compiler_params=_cparams(("arbitrary",)),
        name="outproj_router",
    )(oa, ob, oc, w, h, g, wrh, wrl, tri)


def _ffn_body(tf, x_ref, h_ref, wg_ref, wu_ref, wd_ref, o_ref):
    x = x_ref[...]
    acc = h_ref[...]
    for c in range(wg_ref.shape[1] // tf):
        g = _dot(x, wg_ref[:, c * tf:(c + 1) * tf])
        u = _dot(x, wu_ref[:, c * tf:(c + 1) * tf])
        a = (g * jax.nn.sigmoid(g) * u).astype(bf16)
        acc = acc + _dot(a, wd_ref[c * tf:(c + 1) * tf, :])
    o_ref[...] = acc


def _ffn(xn, h, wg, wu, wd, tm, tf):
    n = h.shape[0]
    F = wg.shape[1]
    const = lambda i: (0, 0)
    return pl.pallas_call(
        functools.partial(_ffn_body, tf),
        grid=(n // tm,),
        in_specs=[pl.BlockSpec((tm, D), lambda i: (i, 0)),
                  pl.BlockSpec((tm, D), lambda i: (i, 0)),
                  pl.BlockSpec((D, F), const),
                  pl.BlockSpec((D, F), const),
                  pl.BlockSpec((F, D), const)],
        out_specs=pl.BlockSpec((tm, D), lambda i: (i, 0)),
        out_shape=jax.ShapeDtypeStruct((n, D), f32),
        compiler_params=_cparams(("parallel",)),
        name="ffn_dense",
    )(xn, h, wg, wu, wd)


ROW_ALIGN = 16
TAB_BASE, TAB_LO, TAB_CNT, TAB_W = 0, N_EXPERTS, 2 * N_EXPERTS, 4 * N_EXPERTS


def _sorted_rows(tb):
    return -(-(2 * tb + N_EXPERTS * (ROW_ALIGN - 1)) // LANES) * LANES


def _for_each_run(tab_ref, tb, fn):
    bits = []
    b = pl.next_power_of_2(2 * tb)
    while b >= ROW_ALIGN:
        bits.append(b)
        b //= 2
    for e in range(N_EXPERTS):
        base = tab_ref[0, 0, TAB_BASE + e]
        lo = tab_ref[0, 0, TAB_LO + e]
        cnt = tab_ref[0, 0, TAB_CNT + e]
        off = 0
        for bit in bits:
            piece = cnt & bit

            @pl.when(piece != 0)
            def _(base=base, lo=lo, off=off, bit=bit):
                fn(pl.multiple_of(base + off, ROW_ALIGN), pl.multiple_of(lo + off, ROW_ALIGN), bit)

            off = off + piece


def _dispatch_body(nblk, tb, tab_ref, tabp_ref, lp_ref, x_ref, xs_in_ref, xs_ref, buf_ref, sem):
    del xs_in_ref
    s = pl.program_id(0)
    slot = s % 2
    lp = lp_ref[0]
    r = lax.broadcasted_iota(i32, (buf_ref.shape[1], 1), 0)
    perm = jnp.where((lp[0:1, :] == r) | (lp[1:2, :] == r), 1.0, 0.0).astype(bf16)
    buf_ref[slot] = _dot(perm, x_ref[...]).astype(bf16)

    def copy(sl):
        return lambda base, lo, rows: pltpu.make_async_copy(
            buf_ref.at[sl, pl.ds(lo, rows)], xs_ref.at[pl.ds(base, rows)], sem.at[sl])

    _for_each_run(tab_ref, tb, lambda *a: copy(slot)(*a).start())

    @pl.when(s > 0)
    def _():
        _for_each_run(tabp_ref, tb, lambda *a: copy(1 - slot)(*a).wait())

    @pl.when(s == nblk - 1)
    def _():
        _for_each_run(tab_ref, tb, lambda *a: copy(slot)(*a).wait())


def _dispatch(xn, tab, lp_rows, xs0, tb):
    nblk = tab.shape[0]
    return pl.pallas_call(
        functools.partial(_dispatch_body, nblk, tb),
        grid=(nblk,),
        in_specs=[pl.BlockSpec((1, 1, TAB_W), lambda s: (s, 0, 0), memory_space=pltpu.SMEM),
                  pl.BlockSpec((1, 1, TAB_W), lambda s: (jnp.maximum(s - 1, 0), 0, 0), memory_space=pltpu.SMEM),
                  pl.BlockSpec((1, 8, tb), lambda s: (s, 0, 0)),
                  pl.BlockSpec((tb, D), lambda s: (s, 0)),
                  pl.BlockSpec(memory_space=pl.ANY)],
        out_specs=pl.BlockSpec(memory_space=pl.ANY),
        out_shape=jax.ShapeDtypeStruct(xs0.shape, bf16),
        scratch_shapes=[pltpu.VMEM((2, _sorted_rows(tb), D), bf16), pltpu.SemaphoreType.DMA((2,))],
        input_output_aliases={4: 0},
        compiler_params=pltpu.CompilerParams(dimension_semantics=("arbitrary",), has_side_effects=True,
                                             vmem_limit_bytes=VMEM_LIMIT),
        name="moe_dispatch",
    )(tab, tab, lp_rows, xn, xs0)


def _experts_body(nf, tf, te_ref, tv_ref, x_ref, wgu_ref, wd_ref, y_ref, acc_ref):
    del te_ref
    t = pl.program_id(0)
    f = pl.program_id(1)
    valid = tv_ref[t] == 1

    @pl.when(valid & (f == 0))
    def _():
        acc_ref[...] = jnp.zeros_like(acc_ref)

    @pl.when(valid)
    def _():
        gu = _dot(x_ref[...], wgu_ref[...])
        g = gu[:, :tf]
        a = (g * jax.nn.sigmoid(g) * gu[:, tf:]).astype(bf16)
        acc_ref[...] += _dot(a, wd_ref[...])

    @pl.when(valid & (f == nf - 1))
    def _():
        y_ref[...] = acc_ref[...].astype(bf16)

    @pl.when(jnp.logical_not(valid) & (f == nf - 1))
    def _():
        y_ref[...] = jnp.zeros_like(y_ref)


def _experts(xs, te, tv, wgu, wd, tme):
    rows = xs.shape[0]
    T = rows // tme
    nf, tf = wgu.shape[1], wgu.shape[3] // 2
    grid_spec = pltpu.PrefetchScalarGridSpec(
        num_scalar_prefetch=2,
        grid=(T, nf),
        in_specs=[pl.BlockSpec((tme, D), lambda t, f, te, tv: (t, 0)),
                  pl.BlockSpec((None, None, D, 2 * tf), lambda t, f, te, tv: (te[t], f, 0, 0)),
                  pl.BlockSpec((None, tf, D), lambda t, f, te, tv: (te[t], f, 0))],
        out_specs=pl.BlockSpec((tme, D), lambda t, f, te, tv: (t, 0)),
        scratch_shapes=[pltpu.VMEM((tme, D), f32)],
    )
    return pl.pallas_call(
        functools.partial(_experts_body, nf, tf),
        grid_spec=grid_spec,
        out_shape=jax.ShapeDtypeStruct((rows, D), bf16),
        compiler_params=_cparams(("arbitrary", "arbitrary")),
        name="moe_experts",
    )(te, tv, xs, wgu, wd)


def _combine_body(nblk, tb, nb, tab_ref, tabn_ref, aux_ref, h_ref, ys_ref, o_ref, buf_ref, obuf_ref, sem, osem):
    s = pl.program_id(0)
    slot = s % 2
    sub = tb // BLK

    def copy(sl):
        return lambda base, lo, rows: pltpu.make_async_copy(
            ys_ref.at[pl.ds(base, rows)], buf_ref.at[sl, pl.ds(lo, rows)], sem.at[sl])

    @pl.when(s == 0)
    def _():
        _for_each_run(tab_ref, tb, lambda *a: copy(slot)(*a).start())

    @pl.when(s + 1 < nblk)
    def _():
        _for_each_run(tabn_ref, tb, lambda *a: copy(1 - slot)(*a).start())

    _for_each_run(tab_ref, tb, lambda *a: copy(slot)(*a).wait())

    total = tab_ref[0, 0, TAB_LO + N_EXPERTS - 1] + tab_ref[0, 0, TAB_CNT + N_EXPERTS - 1]
    rows = lax.broadcasted_iota(i32, (buf_ref.shape[1], 1), 0)
    ys = jnp.where(rows < total, buf_ref[slot], jnp.zeros((), bf16))
    aux = aux_ref[...]
    lane = lax.broadcasted_iota(i32, (1, buf_ref.shape[1]), 1)
    w = (jnp.where(lane == aux[:, 0:1].astype(i32), aux[:, 2:3], 0.0)
         + jnp.where(lane == aux[:, 1:2].astype(i32), aux[:, 3:4], 0.0))
    hi, lo = _split_bf16(w)
    obuf_ref[slot] = h_ref[...] + _dot(hi, ys) + _dot(lo, ys)

    def out_copies(step, sl, fn):
        for q in range(sub):
            pb = step * sub + q

            @pl.when(pb % nb != 0)
            def _(pb=pb, q=q):
                orow = pl.multiple_of(((pb // nb) * (nb - 1) + pb % nb - 1) * BLK, BLK)
                fn(pltpu.make_async_copy(obuf_ref.at[sl, pl.ds(q * BLK, BLK)], o_ref.at[pl.ds(orow, BLK)],
                                         osem.at[sl]))

    out_copies(s, slot, lambda c: c.start())

    @pl.when(s > 0)
    def _():
        out_copies(s - 1, 1 - slot, lambda c: c.wait())

    @pl.when(s == nblk - 1)
    def _():
        out_copies(s, slot, lambda c: c.wait())


def _combine(h, aux, tab, ys, B, nb, tb):
    nblk = tab.shape[0]
    return pl.pallas_call(
        functools.partial(_combine_body, nblk, tb, nb),
        grid=(nblk,),
        in_specs=[pl.BlockSpec((1, 1, TAB_W), lambda s: (s, 0, 0), memory_space=pltpu.SMEM),
                  pl.BlockSpec((1, 1, TAB_W), lambda s: (jnp.minimum(s + 1, nblk - 1), 0, 0),
                               memory_space=pltpu.SMEM),
                  pl.BlockSpec((tb, LANES), lambda s: (s, 0)),
                  pl.BlockSpec((tb, D), lambda s: (s, 0)),
                  pl.BlockSpec(memory_space=pl.ANY)],
        out_specs=pl.BlockSpec(memory_space=pl.ANY),
        out_shape=jax.ShapeDtypeStruct((B * (nb - 1) * BLK, D), f32),
        scratch_shapes=[pltpu.VMEM((2, _sorted_rows(tb), D), bf16), pltpu.VMEM((2, tb, D), f32),
                        pltpu.SemaphoreType.DMA((2,)), pltpu.SemaphoreType.DMA((2,))],
        compiler_params=_cparams(("arbitrary",)),
        name="moe_combine",
    )(tab, tab, aux, h, ys)


def _routing_tables(info, cnt_rows, B, nb, tb, tme):
    n = info.shape[0]
    nblk = n // tb
    ar = jnp.arange(N_EXPERTS, dtype=i32)
    e1 = info[:, 0].astype(i32)
    e2 = info[:, 1].astype(i32)
    blk = jnp.arange(n // BLK, dtype=i32)[:, None]
    rowin = jnp.arange(BLK, dtype=i32)[None, :]
    real = ((blk % nb != 0) | (rowin >= PAD)).reshape(n)
    oh1 = (e1[:, None] == ar) & real[:, None]
    oh2 = (e2[:, None] == ar) & real[:, None]
    blk_cnt = (oh1 | oh2).astype(i32).reshape(nblk, tb, N_EXPERTS).sum(axis=1)
    blk_rank0 = jnp.cumsum(blk_cnt, axis=0) - blk_cnt
    run = ((blk_cnt + ROW_ALIGN - 1) // ROW_ALIGN) * ROW_ALIGN
    run0 = jnp.cumsum(run, axis=0) - run
    seg = ((jnp.sum(run, axis=0) + tme - 1) // tme) * tme
    ends = jnp.cumsum(seg)
    offs = ends - seg
    base = offs[None, :] + run0
    lo = jnp.cumsum(run, axis=1) - run
    tab = jnp.concatenate([base, lo, run, jnp.zeros_like(base)], axis=1).reshape(nblk, 1, TAB_W)

    shift = jnp.repeat(lo - blk_rank0, tb, axis=0)
    lp1 = jnp.where(real, jnp.sum(jnp.where(oh1, shift, 0), axis=1) + info[:, 4].astype(i32), -1)
    lp2 = jnp.where(real, jnp.sum(jnp.where(oh2, shift, 0), axis=1) + info[:, 5].astype(i32), -1)
    lp_rows = jnp.concatenate([lp1.reshape(nblk, 1, tb), lp2.reshape(nblk, 1, tb),
                               jnp.full((nblk, 6, tb), -1, i32)], axis=1)
    aux = jnp.concatenate([lp1.astype(f32)[:, None], lp2.astype(f32)[:, None], info[:, 2:4],
                           jnp.zeros((n, LANES - 4), f32)], axis=1)
    del cnt_rows
    return tab, lp_rows, aux, ends


def _strip_body(h_ref, o_ref):
    o_ref[...] = h_ref[...]


def _strip_meta(h, B, nb):
    nt = nb - 1
    return pl.pallas_call(
        _strip_body,
        grid=(B, nt),
        in_specs=[pl.BlockSpec((BLK, D), lambda b, i: (b * nb + i + 1, 0))],
        out_specs=pl.BlockSpec((BLK, D), lambda b, i: (b * nt + i, 0)),
        out_shape=jax.ShapeDtypeStruct((B * nt * BLK, D), f32),
        compiler_params=_cparams(("parallel", "arbitrary")),
        name="strip_meta",
    )(h)


def _bucket_lookup(table, bucket):
    tshape = (table.shape[1],) + (1,) * bucket.ndim
    out = jnp.zeros((table.shape[1],) + bucket.shape, f32)
    for k in range(N_BUCKETS):
        out = out + jnp.where((bucket == k)[None], table[k].astype(f32).reshape(tshape), 0.0)
    return out


def _bias_tables(rel, nb):
    table_a = rel[:, :A_HEADS] * LOG2E
    table_b = rel[:, A_HEADS:] * LOG2E
    kk = jnp.arange(2 * BLK)[:, None]
    qq = jnp.arange(BLK)[None, :]
    n = BLK + qq - kk
    in_win = (n >= 0) & (n < BLK)
    bias = _bucket_lookup(table_a, _t5_bucket(jnp.maximum(n, 0)))
    variants = []
    for blk in range(3):
        key_ok = (blk - 1) * BLK + kk >= PAD
        variants.append(jnp.where((in_win & key_ok)[None], bias, NEG))
    bias_at = jnp.stack(variants, axis=0)

    d = jnp.arange(nb + 1)[:, None, None]
    kb = jnp.arange(BLK)[None, :, None]
    qb = jnp.arange(BLK)[None, None, :]
    nn = jnp.where(d < nb, d * BLK + qb - kb, -1)
    bias_bt = _bucket_lookup(table_b, _t5_bucket(jnp.maximum(nn, 0)))
    bias_bt = jnp.where((nn >= 0)[None], bias_bt, NEG)
    return bias_at, bias_bt


def _block_diag_ones(gs):
    idx = np.arange(LANES) // gs
    return jnp.asarray((idx[:, None] == idx[None, :]).astype(np.float32), dtype=bf16)


def _col_scale(gain_a, gain_b):
    one = jnp.ones((256,), f32)
    cs = jnp.concatenate([one, jnp.tile(gain_b[1], 8), jnp.tile(gain_a[1], 2)]).reshape(1, D_ROW)
    cst = jnp.concatenate([jnp.tile(gain_a[0], 8) * (64 ** -0.5 * LOG2E),
                           jnp.tile(gain_b[0], 8) * (32 ** -0.5 * LOG2E)]).reshape(R_VB, 1)
    return cs.astype(f32), cst.astype(f32)


def _split_in_weight(w):
    main = jnp.concatenate([w[:, 1536:1792], w[:, 1024:1280], w[:, 512:640]], axis=1)
    tr = jnp.concatenate([w[:, 0:512], w[:, 768:1024], w[:, 1280:1536], w[:, 640:768]], axis=1).T
    return main.astype(bf16), tr.astype(bf16)


def _expert_weights(wg, wu, wd, tf):
    E, _, F = wg.shape
    nf = F // tf
    blocks = lambda w: w.reshape(E, D, nf, tf).transpose(0, 2, 1, 3)
    return jnp.concatenate([blocks(wg), blocks(wu)], axis=-1).astype(bf16), wd.astype(bf16)


def _pool_weight(w_pool):
    out = jnp.zeros((256, 256), f32)
    for gi in range(4):
        out = out.at[gi * 64:(gi + 1) * 64, gi * 64:(gi + 1) * 64].set(w_pool[gi])
    return out.astype(bf16)


def _row_tile(n):
    for tm in (512, 256, 128):
        if n % tm == 0:
            return tm
    raise ValueError(n)


def kernel(x, meta_tokens, rel_bias_table, g_mix, w_in, qk_gain_a, sinks, qk_gain_b, diff_lambda, g_diff_out,
           w_pool, pool_scale, w_out, g_ffn, w_ffn_gate, w_ffn_up, w_ffn_down, w_router, w_exp_gate, w_exp_up,
           w_exp_down):
    B, seq, _ = x.shape
    assert seq % BLK == 0
    depth = g_mix.shape[0]
    nb = seq // BLK + 1
    L = nb * BLK
    n = B * L
    tm = _row_tile(n)
    tme = 768
    tf_e = 896
    tf_d = 256
    kb_d = 4

    bias_at, bias_bt = _bias_tables(rel_bias_table, nb)
    g64 = _block_diag_ones(64)
    g32 = _block_diag_ones(32)

    h = _embed(x, meta_tokens, B, nb)
    out = None
    for l in range(depth):
        lam_init = 0.8 - 0.6 * math.exp(-0.3 * l)
        cs, cst = _col_scale(qk_gain_a[l], qk_gain_b[l])
        w_main, w_tr = _split_in_weight(w_in[l])
        z, ft = _inproj(h, g_mix[l].reshape(1, D), w_main, w_tr, cs, cst, g64, g32, tm)
        oa, oc = _swa_pool(z, ft, bias_at, sinks[l], _pool_weight(w_pool[l]), pool_scale[l].reshape(1, 256), B, nb)
        ob = _diff_attn(ft, z.reshape(B, L, D_ROW), bias_bt, diff_lambda[l],
                        jnp.tile(g_diff_out[l], 2).reshape(BLK, 1), lam_init, B, nb, kb_d)
        wo = w_out[l].astype(bf16)
        gf = g_ffn[l].reshape(1, D)
        j = l // 2
        if l % 2 == 0:
            h, xn = _outproj(oa, ob, oc, wo, h, gf, tm)
            h = _ffn(xn, h, w_ffn_gate[j].astype(bf16), w_ffn_up[j].astype(bf16), w_ffn_down[j].astype(bf16),
                     tm, tf_d)
            out = None
        else:
            assert l == depth - 1, "the expert layer must be the last layer"
            wr = jnp.pad(w_router[j], ((0, 0), (0, LANES - N_EXPERTS)))
            wrh = wr.astype(bf16)
            wrl = (wr - wrh.astype(f32)).astype(bf16)
            tri = jnp.asarray(np.tril(np.ones((tm, tm), np.float32), -1), dtype=bf16)
            h, xn, info, cnt = _outproj(oa, ob, oc, wo, h, gf, tm, router=(wrh, wrl, tri), nb=nb)

            tab, lp_rows, aux, ends = _routing_tables(info, cnt, B, nb, tm, tme)
            rows_max = B * (seq + N_META) * 2 + (n // tm) * N_EXPERTS * (ROW_ALIGN - 1) + N_EXPERTS * (tme - 1)
            T = -(-rows_max // tme)
            starts = jnp.arange(T, dtype=i32) * tme
            te = jnp.minimum(jnp.sum((ends[None, :] <= starts[:, None]).astype(i32), axis=1), N_EXPERTS - 1)
            tv = (starts < ends[-1]).astype(i32)

            xs = _dispatch(xn, tab, lp_rows, jnp.zeros((T * tme, D), bf16), tm)
            wgu, wd = _expert_weights(w_exp_gate[j], w_exp_up[j], w_exp_down[j], tf_e)
            ys = _experts(xs, te, tv, wgu, wd, tme)
            out = _combine(h, aux, tab, ys, B, nb, tm)
    if out is None:
        out = _strip_meta(h, B, nb)
    return out.reshape(B, seq, D)
```

```python
import functools
import math

import jax
import jax.numpy as jnp
import numpy as np
from jax import lax
from jax.experimental import pallas as pl
from jax.experimental.pallas import tpu as pltpu

f32 = jnp.float32
bf16 = jnp.bfloat16
i32 = jnp.int32

D = 1024
BLK = 128
N_META = 16
PAD = BLK - N_META
A_HEADS = 8
B_HEADS = 4
N_BUCKETS = 32
MAX_DISTANCE = 128
N_EXPERTS = 8
D_FF_EXPERT = 3584
EPS = 1e-6
NEG = -1e30
LANES = 128
D_ROW = 640
D_FM = 1152
R_QB, R_VB, R_VA = 512, 768, 1024
LOG2E = 1.4426950408889634

VMEM_LIMIT = 56 * 1024 * 1024


def _cparams(sem):
    return pltpu.CompilerParams(dimension_semantics=sem, vmem_limit_bytes=VMEM_LIMIT)


def _t5_bucket(n):
    max_exact = N_BUCKETS // 2
    large = max_exact
    for j in range(1, N_BUCKETS - max_exact):
        thr = math.ceil(max_exact * (MAX_DISTANCE / max_exact) ** (j / (N_BUCKETS - max_exact)))
        large = large + (n >= thr).astype(i32)
    return jnp.where(n < max_exact, n, large)


def _dot(a, b):
    return jnp.dot(a, b, preferred_element_type=f32)


def _dot_nt(a, b):
    return lax.dot_general(a, b, (((1,), (1,)), ((), ())), preferred_element_type=f32)


def _embed_body(x_ref, m_ref, o_ref):
    o_ref[0:PAD, :] = jnp.zeros((PAD, D), f32)
    o_ref[PAD:BLK, :] = m_ref[...]
    o_ref[BLK:, :] = x_ref[...]


def _embed(x, meta, B, nb):
    seq = x.shape[1]
    return pl.pallas_call(
        _embed_body,
        grid=(B,),
        in_specs=[pl.BlockSpec((None, seq, D), lambda b: (b, 0, 0)),
                  pl.BlockSpec((N_META, D), lambda b: (0, 0))],
        out_specs=pl.BlockSpec((nb * BLK, D), lambda b: (b, 0)),
        out_shape=jax.ShapeDtypeStruct((B * nb * BLK, D), f32),
        compiler_params=_cparams(("parallel",)),
        name="embed",
    )(x, meta)


def _split_bf16(x):
    hi = x.astype(bf16)
    return hi, (x - hi.astype(f32)).astype(bf16)


def _inproj_body(tm, h_ref, g_ref, w_ref, wt_ref, cs_ref, cst_ref, g64_ref, g32_ref, z_ref, ft_ref):
    x = h_ref[...]
    ms = jnp.mean(x * x, axis=-1, keepdims=True)
    xn = (x * lax.rsqrt(ms + EPS) * g_ref[...]).astype(bf16)

    for c in range(D_ROW // LANES):
        zc = _dot(xn, w_ref[:, c * LANES:(c + 1) * LANES])
        if c >= 2:
            gm, inv = (g32_ref, 1.0 / 32) if c < 4 else (g64_ref, 1.0 / 64)
            ss = _dot((zc * zc).astype(bf16), gm[...])
            zc = zc * lax.rsqrt(ss * inv + EPS)
        z_ref[:, c * LANES:(c + 1) * LANES] = (zc * cs_ref[:, c * LANES:(c + 1) * LANES]).astype(bf16)

    zt_all = _dot_nt(wt_ref[...], xn)
    for c in range(D_FM // LANES):
        zt = zt_all[c * LANES:(c + 1) * LANES]
        if c < R_VB // LANES:
            gm, inv = (g64_ref, 1.0 / 64) if c < R_QB // LANES else (g32_ref, 1.0 / 32)
            ss = _dot(gm[...], (zt * zt).astype(bf16))
            zt = zt * lax.rsqrt(ss * inv + EPS) * cst_ref[c * LANES:(c + 1) * LANES, :]
        for k in range(tm // BLK):
            ft_ref[k, c * LANES:(c + 1) * LANES, :] = zt[:, k * BLK:(k + 1) * BLK].astype(bf16)


def _inproj(h, g, w, wt, cs, cst, g64, g32, tm):
    n = h.shape[0]
    const = lambda i: (0, 0)
    kb = tm // BLK
    return pl.pallas_call(
        functools.partial(_inproj_body, tm),
        grid=(n // tm,),
        in_specs=[pl.BlockSpec((tm, D), lambda i: (i, 0)),
                  pl.BlockSpec((1, D), const),
                  pl.BlockSpec((D, D_ROW), const),
                  pl.BlockSpec((D_FM, D), const),
                  pl.BlockSpec((1, D_ROW), const),
                  pl.BlockSpec((R_VB, 1), const),
                  pl.BlockSpec((LANES, LANES), const),
                  pl.BlockSpec((LANES, LANES), const)],
        out_specs=[pl.BlockSpec((tm, D_ROW), lambda i: (i, 0)),
                   pl.BlockSpec((kb, D_FM, BLK), lambda i: (i, 0, 0))],
        out_shape=[jax.ShapeDtypeStruct((n, D_ROW), bf16),
                   jax.ShapeDtypeStruct((n // BLK, D_FM, BLK), bf16)],
        compiler_params=_cparams(("parallel",)),
        name="norm_inproj",
    )(h, g, w, wt, cs, cst, g64, g32)


def _swa_pool_body(qt_ref, kp_ref, kc_ref, vtp_ref, vtc_ref, up_ref, uc_ref, bias_ref, sink_ref,
                   wp_ref, ps_ref, oa_ref, oc_ref):
    i = pl.program_id(1)
    kcat = jnp.concatenate([kp_ref[...], kc_ref[...]], axis=0)
    vt = jnp.concatenate([vtp_ref[...], vtc_ref[...]], axis=1)
    zeros = jnp.zeros((64, BLK), bf16)
    for grp in range(2):
        pts, inv_ls = [], []
        for hh in range(4):
            h = 4 * grp + hh
            qrows = qt_ref[64 * h:64 * h + 64, :]
            qmt = jnp.concatenate([qrows, zeros] if grp == 0 else [zeros, qrows], axis=0)
            st = _dot(kcat, qmt) + bias_ref[h]
            sink = sink_ref[h] * LOG2E
            m = jnp.maximum(jnp.max(st, axis=0, keepdims=True), sink)
            p = jnp.exp2(st - m)
            inv_ls.append(1.0 / (jnp.sum(p, axis=0, keepdims=True) + jnp.exp2(sink - m)))
            pts.append(p.astype(bf16))
        ot = _dot(vt[64 * grp:64 * grp + 64, :], jnp.concatenate(pts, axis=1))
        for pair in range(2):
            o2 = jnp.concatenate([ot[:, (2 * pair + k) * BLK:(2 * pair + k + 1) * BLK] * inv_ls[2 * pair + k]
                                  for k in range(2)], axis=0)
            pb = 2 * grp + pair
            oa_ref[:, pb * LANES:(pb + 1) * LANES] = o2.T.astype(bf16)

    row = lax.broadcasted_iota(i32, (BLK, 1), 0)
    t = i * BLK + row - PAD
    u_cur = jnp.where(t >= 0, uc_ref[...].astype(f32), 0.0)
    u_prev = jnp.where(i > 0, up_ref[BLK - 16:, :].astype(f32), 0.0)
    ext = jnp.concatenate([u_prev, u_cur], axis=0)
    s2 = ext + pltpu.roll(ext, 1, 0)
    s4 = s2 + pltpu.roll(s2, 2, 0)
    s8 = s4 + pltpu.roll(s4, 4, 0)
    s16 = s8 + pltpu.roll(s8, 8, 0)
    lane2 = lax.broadcasted_iota(i32, (1, 256), 1)
    grp2 = lane2 // 64
    sums = jnp.where(grp2 == 0, s2, jnp.where(grp2 == 1, s4, jnp.where(grp2 == 2, s8, s16)))[16:, :]
    win = jnp.where(grp2 == 0, 2, jnp.where(grp2 == 1, 4, jnp.where(grp2 == 2, 8, 16)))
    cnt = jnp.maximum(jnp.minimum(t + 1, win), 1).astype(f32)
    dlt = (sums / cnt - u_cur).astype(bf16)
    oc_ref[...] = (_dot(dlt, wp_ref[...]) * ps_ref[...]).astype(bf16)


def _swa_pool(z, ft, bias_at, sinks, wpool_bd, pool_scale, B, nb):
    n = z.shape[0]
    cur = lambda c: (lambda b, i: (b * nb + i, c))
    prev = lambda c: (lambda b, i: (b * nb + jnp.maximum(i - 1, 0), c))
    return pl.pallas_call(
        _swa_pool_body,
        grid=(B, nb),
        in_specs=[pl.BlockSpec((None, 512, BLK), lambda b, i: (b * nb + i, 0, 0)),
                  pl.BlockSpec((BLK, LANES), prev(4)),
                  pl.BlockSpec((BLK, LANES), cur(4)),
                  pl.BlockSpec((None, BLK, BLK), lambda b, i: (b * nb + jnp.maximum(i - 1, 0), R_VA // BLK, 0)),
                  pl.BlockSpec((None, BLK, BLK), lambda b, i: (b * nb + i, R_VA // BLK, 0)),
                  pl.BlockSpec((BLK, 256), prev(0)),
                  pl.BlockSpec((BLK, 256), cur(0)),
                  pl.BlockSpec((None, A_HEADS, 2 * BLK, BLK), lambda b, i: (jnp.minimum(i, 2), 0, 0, 0)),
                  pl.BlockSpec(memory_space=pltpu.SMEM),
                  pl.BlockSpec((256, 256), lambda b, i: (0, 0)),
                  pl.BlockSpec((1, 256), lambda b, i: (0, 0))],
        out_specs=[pl.BlockSpec((BLK, 512), lambda b, i: (b * nb + i, 0)),
                   pl.BlockSpec((BLK, 256), lambda b, i: (b * nb + i, 0))],
        out_shape=[jax.ShapeDtypeStruct((n, 512), bf16), jax.ShapeDtypeStruct((n, 256), bf16)],
        compiler_params=_cparams(("parallel", "arbitrary")),
        name="swa_pool",
    )(ft, z, z, ft, ft, z, z, bias_at, sinks, wpool_bd, pool_scale)


def _diff_attn_body(lam_init, nb, kb_step, qt_ref, k_ref, vt_ref, bias_ref, dl_ref, go_ref, o_ref,
                    acc_ref, m_ref, l_ref, sa_ref, sb_ref, xa_ref, xb_ref, p_ref):
    i = pl.program_id(1)
    row = lax.broadcasted_iota(i32, (BLK, 1), 0)
    top = row < 64
    zero = jnp.zeros((), bf16)
    qcat = []
    for p in range(2):
        qp = qt_ref[p * BLK:(p + 1) * BLK, :]
        qcat.append([jnp.concatenate(
            [jnp.where((row >= 64 * e + 32 * c) & (row < 64 * e + 32 * c + 32), qp, zero) for c in range(2)],
            axis=1) for e in range(2)])
    rows_k = lax.broadcasted_iota(i32, (kb_step * BLK, 1), 0)
    padmask = jnp.where(rows_k < PAD, NEG, 0.0).astype(f32)

    m_ref[...] = jnp.full(m_ref.shape, 3 * NEG, f32)
    l_ref[...] = jnp.zeros(l_ref.shape, f32)
    acc_ref[...] = jnp.zeros(acc_ref.shape, f32)

    def blocks(s):
        j0 = s * kb_step
        return [jnp.minimum(j0 + kb, nb - 1) for kb in range(kb_step)], j0

    def scores(s, first, s_ref, x_ref):
        blks, j0 = blocks(s)
        dix = [jnp.where(i - (j0 + kb) >= 0, i - (j0 + kb), nb) for kb in range(kb_step)]
        kparts = [k_ref[pl.ds(pl.multiple_of(b * BLK, BLK), BLK), :] for b in blks]
        for p in range(2):
            kcat = jnp.concatenate([kp[:, p * BLK:(p + 1) * BLK] for kp in kparts], axis=0)
            for e in range(2):
                h = 2 * p + e
                st2 = _dot(kcat, qcat[p][e])
                bias = jnp.concatenate([bias_ref[h, dd] for dd in dix], axis=0)
                if first:
                    bias = bias + padmask
                for c in range(2):
                    tile = st2[:, c * BLK:(c + 1) * BLK] + bias
                    s_ref[2 * h + c] = tile
                    x_ref[2 * h + c:2 * h + c + 1, :] = jnp.max(tile, axis=0, keepdims=True)

    def consume(s, s_ref, x_ref):
        blks, _ = blocks(s)
        vparts = [vt_ref[b] for b in blks]
        for p in range(2):
            vtc = jnp.concatenate([vp[p * BLK:(p + 1) * BLK, :] for vp in vparts], axis=1)
            lhs = jnp.concatenate([jnp.where(top, vtc, zero), jnp.where(top, zero, vtc)], axis=1)
            for c in range(2):
                alphas = []
                for e in range(2):
                    idx = 2 * (2 * p + e) + c
                    m_old = m_ref[idx:idx + 1, :]
                    m_new = jnp.maximum(m_old, x_ref[idx:idx + 1, :])
                    alpha = jnp.exp2(m_old - m_new)
                    lsum = alpha * l_ref[idx:idx + 1, :]
                    for kb in range(kb_step):
                        pt = jnp.exp2(s_ref[idx, kb * BLK:(kb + 1) * BLK, :] - m_new)
                        lsum = lsum + jnp.sum(pt, axis=0, keepdims=True)
                        p_ref[c, (e * kb_step + kb) * BLK:(e * kb_step + kb + 1) * BLK, :] = pt.astype(bf16)
                    l_ref[idx:idx + 1, :] = lsum
                    m_ref[idx:idx + 1, :] = m_new
                    alphas.append(alpha)
                upd = _dot(lhs, p_ref[c])
                a = 2 * p + c
                acc_ref[a, 0:64, :] = acc_ref[a, 0:64, :] * alphas[0] + upd[0:64]
                acc_ref[a, 64:128, :] = acc_ref[a, 64:128, :] * alphas[1] + upd[64:128]

    n_steps = (i + kb_step) // kb_step
    scores(0, True, sa_ref, xa_ref)

    def body(s, carry):
        @pl.when(s % 2 == 0)
        def _():
            scores(s + 1, False, sb_ref, xb_ref)
            consume(s, sa_ref, xa_ref)

        @pl.when(s % 2 == 1)
        def _():
            scores(s + 1, False, sa_ref, xa_ref)
            consume(s, sb_ref, xb_ref)

        return carry

    lax.fori_loop(0, n_steps - 1, body, 0)
    last = n_steps - 1

    @pl.when(last % 2 == 0)
    def _():
        consume(last, sa_ref, xa_ref)

    @pl.when(last % 2 == 1)
    def _():
        consume(last, sb_ref, xb_ref)

    dl = dl_ref[...]
    lam = (jnp.exp(jnp.sum(dl[0:1] * dl[1:2], axis=-1, keepdims=True))
           - jnp.exp(jnp.sum(dl[2:3] * dl[3:4], axis=-1, keepdims=True)) + lam_init)
    for p in range(2):
        parts = []
        for e in range(2):
            h = 2 * p + e
            rows = slice(64 * e, 64 * e + 64)
            o = (acc_ref[2 * p, rows, :] / l_ref[2 * h:2 * h + 1, :]
                 - lam * (acc_ref[2 * p + 1, rows, :] / l_ref[2 * h + 1:2 * h + 2, :]))
            ssq = jnp.sum(o * o, axis=0, keepdims=True)
            parts.append(o * lax.rsqrt(ssq * (1.0 / 64) + EPS))
        o = jnp.concatenate(parts, axis=0) * go_ref[...] * (1.0 - lam_init)
        o_ref[:, p * BLK:(p + 1) * BLK] = o.T.astype(bf16)


def _diff_attn(ft, z3, bias_bt, dl, go, lam_init, B, nb, kb_step):
    L = nb * BLK
    return pl.pallas_call(
        functools.partial(_diff_attn_body, lam_init, nb, kb_step),
        grid=(B, nb),
        in_specs=[pl.BlockSpec((None, 256, BLK), lambda b, i: (b * nb + i, R_QB // 256, 0)),
                  pl.BlockSpec((None, L, 256), lambda b, i: (b, 0, 1)),
                  pl.BlockSpec((nb, 256, BLK), lambda b, i: (b, R_VB // 256, 0)),
                  pl.BlockSpec((B_HEADS, nb + 1, BLK, BLK), lambda b, i: (0, 0, 0, 0)),
                  pl.BlockSpec((4, 32), lambda b, i: (0, 0)),
                  pl.BlockSpec((BLK, 1), lambda b, i: (0, 0))],
        out_specs=pl.BlockSpec((BLK, 256), lambda b, i: (b * nb + i, 0)),
        out_shape=jax.ShapeDtypeStruct((B * L, 256), bf16),
        scratch_shapes=[pltpu.VMEM((4, BLK, BLK), f32), pltpu.VMEM((8, BLK), f32), pltpu.VMEM((8, BLK), f32),
                        pltpu.VMEM((8, kb_step * BLK, BLK), f32), pltpu.VMEM((8, kb_step * BLK, BLK), f32),
                        pltpu.VMEM((8, BLK), f32), pltpu.VMEM((8, BLK), f32),
                        pltpu.VMEM((2, 2 * kb_step * BLK, BLK), bf16)],
        compiler_params=_cparams(("parallel", "arbitrary")),
        name="diff_attn",
    )(ft, z3, ft, bias_bt, dl, go)


def _outproj_common(oa_ref, ob_ref, oc_ref, w_ref, h_ref, g_ref):
    y = (_dot(oa_ref[...], w_ref[0:512, :]) + _dot(ob_ref[...], w_ref[512:768, :])
         + _dot(oc_ref[...], w_ref[768:1024, :]))
    hn = h_ref[...] + y
    ms = jnp.mean(hn * hn, axis=-1, keepdims=True)
    return hn, hn * lax.rsqrt(ms + EPS) * g_ref[...]


def _outproj_body(oa_ref, ob_ref, oc_ref, w_ref, h_ref, g_ref, ho_ref, xn_ref):
    hn, xn = _outproj_common(oa_ref, ob_ref, oc_ref, w_ref, h_ref, g_ref)
    ho_ref[...] = hn
    xn_ref[...] = xn.astype(bf16)


def _outproj_router_body(nb, tm, oa_ref, ob_ref, oc_ref, w_ref, h_ref, g_ref, wrh_ref, wrl_ref, tri_ref,
                         ho_ref, xn_ref, info_ref, cnt_ref):
    step = pl.program_id(0)
    hn, xn = _outproj_common(oa_ref, ob_ref, oc_ref, w_ref, h_ref, g_ref)
    ho_ref[...] = hn
    xn_ref[...] = xn.astype(bf16)

    hi, lo = _split_bf16(xn)
    logits = _dot(hi, wrh_ref[...]) + _dot(lo, wrh_ref[...]) + _dot(hi, wrl_ref[...])
    lane = lax.broadcasted_iota(i32, (1, LANES), 1)
    ninf = -jnp.inf
    lg = jnp.where(lane < N_EXPERTS, logits, ninf)
    m1 = jnp.max(lg, axis=-1, keepdims=True)
    i1 = jnp.min(jnp.where(lg == m1, lane, LANES), axis=-1, keepdims=True)
    lg2 = jnp.where(lane == i1, ninf, lg)
    m2 = jnp.max(lg2, axis=-1, keepdims=True)
    i2 = jnp.min(jnp.where(lg2 == m2, lane, LANES), axis=-1, keepdims=True)
    e21 = jnp.exp(m2 - m1)
    g1 = 1.0 / (1.0 + e21)
    g2 = e21 * g1

    rowin = lax.broadcasted_iota(i32, (BLK, 1), 0)
    valids = []
    for k in range(tm // BLK):
        blk = step * (tm // BLK) + k
        valids.append(jnp.logical_or(blk % nb != 0, rowin >= PAD))
    valid = jnp.concatenate(valids, axis=0)
    oh = jnp.where(valid & ((lane == i1) | (lane == i2)), 1.0, 0.0)

    @pl.when(step == 0)
    def _():
        cnt_ref[...] = jnp.zeros_like(cnt_ref)

    base = cnt_ref[...]
    rank = _dot(tri_ref[...], oh.astype(bf16)) + base
    r1 = jnp.sum(jnp.where(lane == i1, rank, 0.0), axis=-1, keepdims=True)
    r2 = jnp.sum(jnp.where(lane == i2, rank, 0.0), axis=-1, keepdims=True)
    cnt_ref[...] = base + jnp.sum(oh, axis=0, keepdims=True)
    info = jnp.where(lane == 0, i1.astype(f32),
           jnp.where(lane == 1, i2.astype(f32),
           jnp.where(lane == 2, g1,
           jnp.where(lane == 3, g2,
           jnp.where(lane == 4, r1,
           jnp.where(lane == 5, r2, 0.0))))))
    info_ref[...] = info


def _outproj(oa, ob, oc, w, h, g, tm, router=None, nb=None):
    n = h.shape[0]
    row = lambda c: pl.BlockSpec((tm, c), lambda i: (i, 0))
    const = lambda r, c: pl.BlockSpec((r, c), lambda i: (0, 0))
    in_specs = [row(512), row(256), row(256), const(D, D), row(D), const(1, D)]
    if router is None:
        return pl.pallas_call(
            _outproj_body,
            grid=(n // tm,),
            in_specs=in_specs,
            out_specs=[row(D), row(D)],
            out_shape=[jax.ShapeDtypeStruct((n, D), f32), jax.ShapeDtypeStruct((n, D), bf16)],
            compiler_params=_cparams(("parallel",)),
            name="outproj",
        )(oa, ob, oc, w, h, g)
    wrh, wrl, tri = router
    return pl.pallas_call(
        functools.partial(_outproj_router_body, nb, tm),
        grid=(n // tm,),
        in_specs=in_specs + [const(D, LANES), const(D, LANES), const(tm, tm)],
        out_specs=[row(D), row(D), row(LANES), const(1, LANES)],
        out_shape=[jax.ShapeDtypeStruct((n, D), f32), jax.ShapeDtypeStruct((n, D), bf16),
                   jax.ShapeDtypeStruct((n, LANES), f32), jax.ShapeDtypeStruct((1, LANES), f32)],
        compiler_params=_cparams(("arbitrary",)),
        name="outproj_router",
    )(oa, ob, oc, w, h, g, wrh, wrl, tri)


def _ffn_body(tf, x_ref, h_ref, wg_ref, wu_ref, wd_ref, o_ref):
    x = x_ref[...]
    acc = h_ref[...]
    for c in range(wg_ref.shape[1] // tf):
        g = _dot(x, wg_ref[:, c * tf:(c + 1) * tf])
        u = _dot(x, wu_ref[:, c * tf:(c + 1) * tf])
        a = (g * jax.nn.sigmoid(g) * u).astype(bf16)
        acc = acc + _dot(a, wd_ref[c * tf:(c + 1) * tf, :])
    o_ref[...] = acc


def _ffn(xn, h, wg, wu, wd, tm, tf):
    n = h.shape[0]
    F = wg.shape[1]
    const = lambda i: (0, 0)
    return pl.pallas_call(
        functools.partial(_ffn_body, tf),
        grid=(n // tm,),
        in_specs=[pl.BlockSpec((tm, D), lambda i: (i, 0)),
                  pl.BlockSpec((tm, D), lambda i: (i, 0)),
                  pl.BlockSpec((D, F), const),
                  pl.BlockSpec((D, F), const),
                  pl.BlockSpec((F, D), const)],
        out_specs=pl.BlockSpec((tm, D), lambda i: (i, 0)),
        out_shape=jax.ShapeDtypeStruct((n, D), f32),
        compiler_params=_cparams(("parallel",)),
        name="ffn_dense",
    )(xn, h, wg, wu, wd)


ROW_ALIGN = 16
TAB_BASE, TAB_LO, TAB_CNT, TAB_W = 0, N_EXPERTS, 2 * N_EXPERTS, 4 * N_EXPERTS


def _sorted_rows(tb):
    return -(-(2 * tb + N_EXPERTS * (ROW_ALIGN - 1)) // LANES) * LANES


def _for_each_run(tab_ref, tb, fn):
    bits = []
    b = pl.next_power_of_2(2 * tb)
    while b >= ROW_ALIGN:
        bits.append(b)
        b //= 2
    for e in range(N_EXPERTS):
        base = tab_ref[0, 0, TAB_BASE + e]
        lo = tab_ref[0, 0, TAB_LO + e]
        cnt = tab_ref[0, 0, TAB_CNT + e]
        off = 0
        for bit in bits:
            piece = cnt & bit

            @pl.when(piece != 0)
            def _(base=base, lo=lo, off=off, bit=bit):
                fn(pl.multiple_of(base + off, ROW_ALIGN), pl.multiple_of(lo + off, ROW_ALIGN), bit)

            off = off + piece


def _dispatch_body(nblk, tb, tab_ref, tabp_ref, lp_ref, x_ref, xs_in_ref, xs_ref, buf_ref, sem):
    del xs_in_ref
    s = pl.program_id(0)
    slot = s % 2
    lp = lp_ref[0]
    r = lax.broadcasted_iota(i32, (buf_ref.shape[1], 1), 0)
    perm = jnp.where((lp[0:1, :] == r) | (lp[1:2, :] == r), 1.0, 0.0).astype(bf16)
    buf_ref[slot] = _dot(perm, x_ref[...]).astype(bf16)

    def copy(sl):
        return lambda base, lo, rows: pltpu.make_async_copy(
            buf_ref.at[sl, pl.ds(lo, rows)], xs_ref.at[pl.ds(base, rows)], sem.at[sl])

    _for_each_run(tab_ref, tb, lambda *a: copy(slot)(*a).start())

    @pl.when(s > 0)
    def _():
        _for_each_run(tabp_ref, tb, lambda *a: copy(1 - slot)(*a).wait())

    @pl.when(s == nblk - 1)
    def _():
        _for_each_run(tab_ref, tb, lambda *a: copy(slot)(*a).wait())


def _dispatch(xn, tab, lp_rows, xs0, tb):
    nblk = tab.shape[0]
    return pl.pallas_call(
        functools.partial(_dispatch_body, nblk, tb),
        grid=(nblk,),
        in_specs=[pl.BlockSpec((1, 1, TAB_W), lambda s: (s, 0, 0), memory_space=pltpu.SMEM),
                  pl.BlockSpec((1, 1, TAB_W), lambda s: (jnp.maximum(s - 1, 0), 0, 0), memory_space=pltpu.SMEM),
                  pl.BlockSpec((1, 8, tb), lambda s: (s, 0, 0)),
                  pl.BlockSpec((tb, D), lambda s: (s, 0)),
                  pl.BlockSpec(memory_space=pl.ANY)],
        out_specs=pl.BlockSpec(memory_space=pl.ANY),
        out_shape=jax.ShapeDtypeStruct(xs0.shape, bf16),
        scratch_shapes=[pltpu.VMEM((2, _sorted_rows(tb), D), bf16), pltpu.SemaphoreType.DMA((2,))],
        input_output_aliases={4: 0},
        compiler_params=pltpu.CompilerParams(dimension_semantics=("arbitrary",), has_side_effects=True,
                                             vmem_limit_bytes=VMEM_LIMIT),
        name="moe_dispatch",
    )(tab, tab, lp_rows, xn, xs0)


def _experts_body(nf, sub, te_ref, tv_ref, x_ref, wg_ref, wu_ref, wd_ref, y_ref, acc_ref):
    del te_ref
    t = pl.program_id(0)
    f = pl.program_id(1)
    valid = tv_ref[t] == 1

    @pl.when(valid & (f == 0))
    def _():
        acc_ref[...] = jnp.zeros_like(acc_ref)

    @pl.when(valid)
    def _():
        x = x_ref[...]
        acc = acc_ref[...]
        for k in range(wg_ref.shape[1] // sub):
            g = _dot(x, wg_ref[:, k * sub:(k + 1) * sub])
            u = _dot(x, wu_ref[:, k * sub:(k + 1) * sub])
            a = (g * jax.nn.sigmoid(g) * u).astype(bf16)
            acc = acc + _dot(a, wd_ref[k * sub:(k + 1) * sub, :])
        acc_ref[...] = acc

    @pl.when(valid & (f == nf - 1))
    def _():
        y_ref[...] = acc_ref[...].astype(bf16)

    @pl.when(jnp.logical_not(valid) & (f == nf - 1))
    def _():
        y_ref[...] = jnp.zeros_like(y_ref)


def _experts(xs, te, tv, wg, wu, wd, tme, tf, sub):
    rows = xs.shape[0]
    T = rows // tme
    nf = D_FF_EXPERT // tf
    grid_spec = pltpu.PrefetchScalarGridSpec(
        num_scalar_prefetch=2,
        grid=(T, nf),
        in_specs=[pl.BlockSpec((tme, D), lambda t, f, te, tv: (t, 0)),
                  pl.BlockSpec((None, D, tf), lambda t, f, te, tv: (te[t], 0, f)),
                  pl.BlockSpec((None, D, tf), lambda t, f, te, tv: (te[t], 0, f)),
                  pl.BlockSpec((None, tf, D), lambda t, f, te, tv: (te[t], f, 0))],
        out_specs=pl.BlockSpec((tme, D), lambda t, f, te, tv: (t, 0)),
        scratch_shapes=[pltpu.VMEM((tme, D), f32)],
    )
    return pl.pallas_call(
        functools.partial(_experts_body, nf, sub),
        grid_spec=grid_spec,
        out_shape=jax.ShapeDtypeStruct((rows, D), bf16),
        compiler_params=_cparams(("arbitrary", "arbitrary")),
        name="moe_experts",
    )(te, tv, xs, wg, wu, wd)


def _combine_body(nblk, tb, nb, tab_ref, tabn_ref, aux_ref, h_ref, ys_ref, o_ref, buf_ref, obuf_ref, sem, osem):
    s = pl.program_id(0)
    slot = s % 2
    sub = tb // BLK

    def copy(sl):
        return lambda base, lo, rows: pltpu.make_async_copy(
            ys_ref.at[pl.ds(base, rows)], buf_ref.at[sl, pl.ds(lo, rows)], sem.at[sl])

    @pl.when(s == 0)
    def _():
        _for_each_run(tab_ref, tb, lambda *a: copy(slot)(*a).start())

    @pl.when(s + 1 < nblk)
    def _():
        _for_each_run(tabn_ref, tb, lambda *a: copy(1 - slot)(*a).start())

    _for_each_run(tab_ref, tb, lambda *a: copy(slot)(*a).wait())

    total = tab_ref[0, 0, TAB_LO + N_EXPERTS - 1] + tab_ref[0, 0, TAB_CNT + N_EXPERTS - 1]
    rows = lax.broadcasted_iota(i32, (buf_ref.shape[1], 1), 0)
    ys = jnp.where(rows < total, buf_ref[slot], jnp.zeros((), bf16))
    aux = aux_ref[...]
    lane = lax.broadcasted_iota(i32, (1, buf_ref.shape[1]), 1)
    w = (jnp.where(lane == aux[:, 0:1].astype(i32), aux[:, 2:3], 0.0)
         + jnp.where(lane == aux[:, 1:2].astype(i32), aux[:, 3:4], 0.0))
    hi, lo = _split_bf16(w)
    obuf_ref[slot] = h_ref[...] + _dot(hi, ys) + _dot(lo, ys)

    def out_copies(step, sl, fn):
        for q in range(sub):
            pb = step * sub + q

            @pl.when(pb % nb != 0)
            def _(pb=pb, q=q):
                orow = pl.multiple_of(((pb // nb) * (nb - 1) + pb % nb - 1) * BLK, BLK)
                fn(pltpu.make_async_copy(obuf_ref.at[sl, pl.ds(q * BLK, BLK)], o_ref.at[pl.ds(orow, BLK)],
                                         osem.at[sl]))

    out_copies(s, slot, lambda c: c.start())

    @pl.when(s > 0)
    def _():
        out_copies(s - 1, 1 - slot, lambda c: c.wait())

    @pl.when(s == nblk - 1)
    def _():
        out_copies(s, slot, lambda c: c.wait())


def _combine(h, aux, tab, ys, B, nb, tb):
    nblk = tab.shape[0]
    return pl.pallas_call(
        functools.partial(_combine_body, nblk, tb, nb),
        grid=(nblk,),
        in_specs=[pl.BlockSpec((1, 1, TAB_W), lambda s: (s, 0, 0), memory_space=pltpu.SMEM),
                  pl.BlockSpec((1, 1, TAB_W), lambda s: (jnp.minimum(s + 1, nblk - 1), 0, 0),
                               memory_space=pltpu.SMEM),
                  pl.BlockSpec((tb, LANES), lambda s: (s, 0)),
                  pl.BlockSpec((tb, D), lambda s: (s, 0)),
                  pl.BlockSpec(memory_space=pl.ANY)],
        out_specs=pl.BlockSpec(memory_space=pl.ANY),
        out_shape=jax.ShapeDtypeStruct((B * (nb - 1) * BLK, D), f32),
        scratch_shapes=[pltpu.VMEM((2, _sorted_rows(tb), D), bf16), pltpu.VMEM((2, tb, D), f32),
                        pltpu.SemaphoreType.DMA((2,)), pltpu.SemaphoreType.DMA((2,))],
        compiler_params=_cparams(("arbitrary",)),
        name="moe_combine",
    )(tab, tab, aux, h, ys)


def _routing_tables(info, cnt_rows, B, nb, tb, tme):
    n = info.shape[0]
    nblk = n // tb
    ar = jnp.arange(N_EXPERTS, dtype=i32)
    e1 = info[:, 0].astype(i32)
    e2 = info[:, 1].astype(i32)
    blk = jnp.arange(n // BLK, dtype=i32)[:, None]
    rowin = jnp.arange(BLK, dtype=i32)[None, :]
    real = ((blk % nb != 0) | (rowin >= PAD)).reshape(n)
    oh1 = (e1[:, None] == ar) & real[:, None]
    oh2 = (e2[:, None] == ar) & real[:, None]
    blk_cnt = (oh1 | oh2).astype(i32).reshape(nblk, tb, N_EXPERTS).sum(axis=1)
    blk_rank0 = jnp.cumsum(blk_cnt, axis=0) - blk_cnt
    run = ((blk_cnt + ROW_ALIGN - 1) // ROW_ALIGN) * ROW_ALIGN
    run0 = jnp.cumsum(run, axis=0) - run
    seg = ((jnp.sum(run, axis=0) + tme - 1) // tme) * tme
    ends = jnp.cumsum(seg)
    offs = ends - seg
    base = offs[None, :] + run0
    lo = jnp.cumsum(run, axis=1) - run
    tab = jnp.concatenate([base, lo, run, jnp.zeros_like(base)], axis=1).reshape(nblk, 1, TAB_W)

    shift = jnp.repeat(lo - blk_rank0, tb, axis=0)
    lp1 = jnp.where(real, jnp.sum(jnp.where(oh1, shift, 0), axis=1) + info[:, 4].astype(i32), -1)
    lp2 = jnp.where(real, jnp.sum(jnp.where(oh2, shift, 0), axis=1) + info[:, 5].astype(i32), -1)
    lp_rows = jnp.concatenate([lp1.reshape(nblk, 1, tb), lp2.reshape(nblk, 1, tb),
                               jnp.full((nblk, 6, tb), -1, i32)], axis=1)
    aux = jnp.concatenate([lp1.astype(f32)[:, None], lp2.astype(f32)[:, None], info[:, 2:4],
                           jnp.zeros((n, LANES - 4), f32)], axis=1)
    del cnt_rows
    return tab, lp_rows, aux, ends


def _strip_body(h_ref, o_ref):
    o_ref[...] = h_ref[...]


def _strip_meta(h, B, nb):
    nt = nb - 1
    return pl.pallas_call(
        _strip_body,
        grid=(B, nt),
        in_specs=[pl.BlockSpec((BLK, D), lambda b, i: (b * nb + i + 1, 0))],
        out_specs=pl.BlockSpec((BLK, D), lambda b, i: (b * nt + i, 0)),
        out_shape=jax.ShapeDtypeStruct((B * nt * BLK, D), f32),
        compiler_params=_cparams(("parallel", "arbitrary")),
        name="strip_meta",
    )(h)


def _bucket_lookup(table, bucket):
    tshape = (table.shape[1],) + (1,) * bucket.ndim
    out = jnp.zeros((table.shape[1],) + bucket.shape, f32)
    for k in range(N_BUCKETS):
        out = out + jnp.where((bucket == k)[None], table[k].astype(f32).reshape(tshape), 0.0)
    return out


def _bias_tables(rel, nb):
    table_a = rel[:, :A_HEADS] * LOG2E
    table_b = rel[:, A_HEADS:] * LOG2E
    kk = jnp.arange(2 * BLK)[:, None]
    qq = jnp.arange(BLK)[None, :]
    n = BLK + qq - kk
    in_win = (n >= 0) & (n < BLK)
    bias = _bucket_lookup(table_a, _t5_bucket(jnp.maximum(n, 0)))
    variants = []
    for blk in range(3):
        key_ok = (blk - 1) * BLK + kk >= PAD
        variants.append(jnp.where((in_win & key_ok)[None], bias, NEG))
    bias_at = jnp.stack(variants, axis=0)

    d = jnp.arange(nb + 1)[:, None, None]
    kb = jnp.arange(BLK)[None, :, None]
    qb = jnp.arange(BLK)[None, None, :]
    nn = jnp.where(d < nb, d * BLK + qb - kb, -1)
    bias_bt = _bucket_lookup(table_b, _t5_bucket(jnp.maximum(nn, 0)))
    bias_bt = jnp.where((nn >= 0)[None], bias_bt, NEG)
    return bias_at, bias_bt


def _block_diag_ones(gs):
    idx = np.arange(LANES) // gs
    return jnp.asarray((idx[:, None] == idx[None, :]).astype(np.float32), dtype=bf16)


def _col_scale(gain_a, gain_b):
    one = jnp.ones((256,), f32)
    cs = jnp.concatenate([one, jnp.tile(gain_b[1], 8), jnp.tile(gain_a[1], 2)]).reshape(1, D_ROW)
    cst = jnp.concatenate([jnp.tile(gain_a[0], 8) * (64 ** -0.5 * LOG2E),
                           jnp.tile(gain_b[0], 8) * (32 ** -0.5 * LOG2E)]).reshape(R_VB, 1)
    return cs.astype(f32), cst.astype(f32)


def _split_in_weight(w):
    main = jnp.concatenate([w[:, 1536:1792], w[:, 1024:1280], w[:, 512:640]], axis=1)
    tr = jnp.concatenate([w[:, 0:512], w[:, 768:1024], w[:, 1280:1536], w[:, 640:768]], axis=1).T
    return main.astype(bf16), tr.astype(bf16)


def _pool_weight(w_pool):
    out = jnp.zeros((256, 256), f32)
    for gi in range(4):
        out = out.at[gi * 64:(gi + 1) * 64, gi * 64:(gi + 1) * 64].set(w_pool[gi])
    return out.astype(bf16)


def _row_tile(n):
    for tm in (512, 256, 128):
        if n % tm == 0:
            return tm
    raise ValueError(n)


def kernel(x, meta_tokens, rel_bias_table, g_mix, w_in, qk_gain_a, sinks, qk_gain_b, diff_lambda, g_diff_out,
           w_pool, pool_scale, w_out, g_ffn, w_ffn_gate, w_ffn_up, w_ffn_down, w_router, w_exp_gate, w_exp_up,
           w_exp_down):
    B, seq, _ = x.shape
    assert seq % BLK == 0
    depth = g_mix.shape[0]
    nb = seq // BLK + 1
    L = nb * BLK
    n = B * L
    tm = _row_tile(n)
    tme = 768
    tf_e = 1792
    tf_d = 256
    kb_d = 4

    bias_at, bias_bt = _bias_tables(rel_bias_table, nb)
    g64 = _block_diag_ones(64)
    g32 = _block_diag_ones(32)

    h = _embed(x, meta_tokens, B, nb)
    out = None
    for l in range(depth):
        lam_init = 0.8 - 0.6 * math.exp(-0.3 * l)
        cs, cst = _col_scale(qk_gain_a[l], qk_gain_b[l])
        w_main, w_tr = _split_in_weight(w_in[l])
        z, ft = _inproj(h, g_mix[l].reshape(1, D), w_main, w_tr, cs, cst, g64, g32, tm)
        oa, oc = _swa_pool(z, ft, bias_at, sinks[l], _pool_weight(w_pool[l]), pool_scale[l].reshape(1, 256), B, nb)
        ob = _diff_attn(ft, z.reshape(B, L, D_ROW), bias_bt, diff_lambda[l],
                        jnp.tile(g_diff_out[l], 2).reshape(BLK, 1), lam_init, B, nb, kb_d)
        wo = w_out[l].astype(bf16)
        gf = g_ffn[l].reshape(1, D)
        j = l // 2
        if l % 2 == 0:
            h, xn = _outproj(oa, ob, oc, wo, h, gf, tm)
            h = _ffn(xn, h, w_ffn_gate[j].astype(bf16), w_ffn_up[j].astype(bf16), w_ffn_down[j].astype(bf16),
                     tm, tf_d)
            out = None
        else:
            assert l == depth - 1, "the expert layer must be the last layer"
            wr = jnp.pad(w_router[j], ((0, 0), (0, LANES - N_EXPERTS)))
            wrh = wr.astype(bf16)
            wrl = (wr - wrh.astype(f32)).astype(bf16)
            tri = jnp.asarray(np.tril(np.ones((tm, tm), np.float32), -1), dtype=bf16)
            h, xn, info, cnt = _outproj(oa, ob, oc, wo, h, gf, tm, router=(wrh, wrl, tri), nb=nb)

            tab, lp_rows, aux, ends = _routing_tables(info, cnt, B, nb, tm, tme)
            rows_max = B * (seq + N_META) * 2 + (n // tm) * N_EXPERTS * (ROW_ALIGN - 1) + N_EXPERTS * (tme - 1)
            T = -(-rows_max // tme)
            starts = jnp.arange(T, dtype=i32) * tme
            te = jnp.minimum(jnp.sum((ends[None, :] <= starts[:, None]).astype(i32), axis=1), N_EXPERTS - 1)
            tv = (starts < ends[-1]).astype(i32)

            xs = _dispatch(xn, tab, lp_rows, jnp.zeros((T * tme, D), bf16), tm)
            ys = _experts(xs, te, tv, w_exp_gate[j].astype(bf16), w_exp_up[j].astype(bf16),
                          w_exp_down[j].astype(bf16), tme, tf_e, 256)
            out = _combine(h, aux, tab, ys, B, nb, tm)
    if out is None:
        out = _strip_meta(h, B, nb)
    return out.reshape(B, seq, D)
```

```python
import functools
import math

import jax
import jax.numpy as jnp
import numpy as np
from jax import lax
from jax.experimental import pallas as pl
from jax.experimental.pallas import tpu as pltpu

f32 = jnp.float32
bf16 = jnp.bfloat16
i32 = jnp.int32

D = 1024
BLK = 128
N_META = 16
PAD = BLK - N_META
A_HEADS = 8
B_HEADS = 4
N_BUCKETS = 32
MAX_DISTANCE = 128
N_EXPERTS = 8
D_FF_EXPERT = 3584
EPS = 1e-6
NEG = -1e30
LANES = 128
D_ROW = 640
D_FM = 1152
R_QB, R_VB, R_VA = 512, 768, 1024
LOG2E = 1.4426950408889634

VMEM_LIMIT = 56 * 1024 * 1024


def _cparams(sem):
    return pltpu.CompilerParams(dimension_semantics=sem, vmem_limit_bytes=VMEM_LIMIT)


def _t5_bucket(n):
    max_exact = N_BUCKETS // 2
    large = max_exact
    for j in range(1, N_BUCKETS - max_exact):
        thr = math.ceil(max_exact * (MAX_DISTANCE / max_exact) ** (j / (N_BUCKETS - max_exact)))
        large = large + (n >= thr).astype(i32)
    return jnp.where(n < max_exact, n, large)


def _dot(a, b):
    return jnp.dot(a, b, preferred_element_type=f32)


def _dot_nt(a, b):
    return lax.dot_general(a, b, (((1,), (1,)), ((), ())), preferred_element_type=f32)


def _embed_body(x_ref, m_ref, o_ref):
    o_ref[0:PAD, :] = jnp.zeros((PAD, D), f32)
    o_ref[PAD:BLK, :] = m_ref[...]
    o_ref[BLK:, :] = x_ref[...]


def _embed(x, meta, B, nb):
    seq = x.shape[1]
    return pl.pallas_call(
        _embed_body,
        grid=(B,),
        in_specs=[pl.BlockSpec((None, seq, D), lambda b: (b, 0, 0)),
                  pl.BlockSpec((N_META, D), lambda b: (0, 0))],
        out_specs=pl.BlockSpec((nb * BLK, D), lambda b: (b, 0)),
        out_shape=jax.ShapeDtypeStruct((B * nb * BLK, D), f32),
        compiler_params=_cparams(("parallel",)),
        name="embed",
    )(x, meta)


def _split_bf16(x):
    hi = x.astype(bf16)
    return hi, (x - hi.astype(f32)).astype(bf16)


def _inproj_body(tm, h_ref, g_ref, w_ref, wt_ref, cs_ref, cst_ref, g64_ref, g32_ref, z_ref, ft_ref):
    x = h_ref[...]
    ms = jnp.mean(x * x, axis=-1, keepdims=True)
    xn = (x * lax.rsqrt(ms + EPS) * g_ref[...]).astype(bf16)

    for c in range(D_ROW // LANES):
        zc = _dot(xn, w_ref[:, c * LANES:(c + 1) * LANES])
        if c >= 2:
            gm, inv = (g32_ref, 1.0 / 32) if c < 4 else (g64_ref, 1.0 / 64)
            ss = _dot((zc * zc).astype(bf16), gm[...])
            zc = zc * lax.rsqrt(ss * inv + EPS)
        z_ref[:, c * LANES:(c + 1) * LANES] = (zc * cs_ref[:, c * LANES:(c + 1) * LANES]).astype(bf16)

    zt_all = _dot_nt(wt_ref[...], xn)
    for c in range(D_FM // LANES):
        zt = zt_all[c * LANES:(c + 1) * LANES]
        if c < R_VB // LANES:
            gm, inv = (g64_ref, 1.0 / 64) if c < R_QB // LANES else (g32_ref, 1.0 / 32)
            ss = _dot(gm[...], (zt * zt).astype(bf16))
            zt = zt * lax.rsqrt(ss * inv + EPS) * cst_ref[c * LANES:(c + 1) * LANES, :]
        for k in range(tm // BLK):
            ft_ref[k, c * LANES:(c + 1) * LANES, :] = zt[:, k * BLK:(k + 1) * BLK].astype(bf16)


def _inproj(h, g, w, wt, cs, cst, g64, g32, tm):
    n = h.shape[0]
    const = lambda i: (0, 0)
    kb = tm // BLK
    return pl.pallas_call(
        functools.partial(_inproj_body, tm),
        grid=(n // tm,),
        in_specs=[pl.BlockSpec((tm, D), lambda i: (i, 0)),
                  pl.BlockSpec((1, D), const),
                  pl.BlockSpec((D, D_ROW), const),
                  pl.BlockSpec((D_FM, D), const),
                  pl.BlockSpec((1, D_ROW), const),
                  pl.BlockSpec((R_VB, 1), const),
                  pl.BlockSpec((LANES, LANES), const),
                  pl.BlockSpec((LANES, LANES), const)],
        out_specs=[pl.BlockSpec((tm, D_ROW), lambda i: (i, 0)),
                   pl.BlockSpec((kb, D_FM, BLK), lambda i: (i, 0, 0))],
        out_shape=[jax.ShapeDtypeStruct((n, D_ROW), bf16),
                   jax.ShapeDtypeStruct((n // BLK, D_FM, BLK), bf16)],
        compiler_params=_cparams(("parallel",)),
        name="norm_inproj",
    )(h, g, w, wt, cs, cst, g64, g32)


def _swa_pool_body(qt_ref, kp_ref, kc_ref, vtp_ref, vtc_ref, up_ref, uc_ref, bias_ref, sink_ref,
                   wp_ref, ps_ref, oa_ref, oc_ref):
    i = pl.program_id(1)
    kcat = jnp.concatenate([kp_ref[...], kc_ref[...]], axis=0)
    vt = jnp.concatenate([vtp_ref[...], vtc_ref[...]], axis=1)
    zeros = jnp.zeros((64, BLK), bf16)
    for grp in range(2):
        pts, inv_ls = [], []
        for hh in range(4):
            h = 4 * grp + hh
            qrows = qt_ref[64 * h:64 * h + 64, :]
            qmt = jnp.concatenate([qrows, zeros] if grp == 0 else [zeros, qrows], axis=0)
            st = _dot(kcat, qmt) + bias_ref[h]
            sink = sink_ref[h] * LOG2E
            m = jnp.maximum(jnp.max(st, axis=0, keepdims=True), sink)
            p = jnp.exp2(st - m)
            inv_ls.append(1.0 / (jnp.sum(p, axis=0, keepdims=True) + jnp.exp2(sink - m)))
            pts.append(p.astype(bf16))
        ot = _dot(vt[64 * grp:64 * grp + 64, :], jnp.concatenate(pts, axis=1))
        for pair in range(2):
            o2 = jnp.concatenate([ot[:, (2 * pair + k) * BLK:(2 * pair + k + 1) * BLK] * inv_ls[2 * pair + k]
                                  for k in range(2)], axis=0)
            pb = 2 * grp + pair
            oa_ref[:, pb * LANES:(pb + 1) * LANES] = o2.T.astype(bf16)

    row = lax.broadcasted_iota(i32, (BLK, 1), 0)
    t = i * BLK + row - PAD
    u_cur = jnp.where(t >= 0, uc_ref[...].astype(f32), 0.0)
    u_prev = jnp.where(i > 0, up_ref[BLK - 16:, :].astype(f32), 0.0)
    ext = jnp.concatenate([u_prev, u_cur], axis=0)
    s2 = ext + pltpu.roll(ext, 1, 0)
    s4 = s2 + pltpu.roll(s2, 2, 0)
    s8 = s4 + pltpu.roll(s4, 4, 0)
    s16 = s8 + pltpu.roll(s8, 8, 0)
    lane2 = lax.broadcasted_iota(i32, (1, 256), 1)
    grp2 = lane2 // 64
    sums = jnp.where(grp2 == 0, s2, jnp.where(grp2 == 1, s4, jnp.where(grp2 == 2, s8, s16)))[16:, :]
    win = jnp.where(grp2 == 0, 2, jnp.where(grp2 == 1, 4, jnp.where(grp2 == 2, 8, 16)))
    cnt = jnp.maximum(jnp.minimum(t + 1, win), 1).astype(f32)
    dlt = (sums / cnt - u_cur).astype(bf16)
    oc_ref[...] = (_dot(dlt, wp_ref[...]) * ps_ref[...]).astype(bf16)


def _swa_pool(z, ft, bias_at, sinks, wpool_bd, pool_scale, B, nb):
    n = z.shape[0]
    cur = lambda c: (lambda b, i: (b * nb + i, c))
    prev = lambda c: (lambda b, i: (b * nb + jnp.maximum(i - 1, 0), c))
    return pl.pallas_call(
        _swa_pool_body,
        grid=(B, nb),
        in_specs=[pl.BlockSpec((None, 512, BLK), lambda b, i: (b * nb + i, 0, 0)),
                  pl.BlockSpec((BLK, LANES), prev(4)),
                  pl.BlockSpec((BLK, LANES), cur(4)),
                  pl.BlockSpec((None, BLK, BLK), lambda b, i: (b * nb + jnp.maximum(i - 1, 0), R_VA // BLK, 0)),
                  pl.BlockSpec((None, BLK, BLK), lambda b, i: (b * nb + i, R_VA // BLK, 0)),
                  pl.BlockSpec((BLK, 256), prev(0)),
                  pl.BlockSpec((BLK, 256), cur(0)),
                  pl.BlockSpec((None, A_HEADS, 2 * BLK, BLK), lambda b, i: (jnp.minimum(i, 2), 0, 0, 0)),
                  pl.BlockSpec(memory_space=pltpu.SMEM),
                  pl.BlockSpec((256, 256), lambda b, i: (0, 0)),
                  pl.BlockSpec((1, 256), lambda b, i: (0, 0))],
        out_specs=[pl.BlockSpec((BLK, 512), lambda b, i: (b * nb + i, 0)),
                   pl.BlockSpec((BLK, 256), lambda b, i: (b * nb + i, 0))],
        out_shape=[jax.ShapeDtypeStruct((n, 512), bf16), jax.ShapeDtypeStruct((n, 256), bf16)],
        compiler_params=_cparams(("parallel", "arbitrary")),
        name="swa_pool",
    )(ft, z, z, ft, ft, z, z, bias_at, sinks, wpool_bd, pool_scale)


def _diff_attn_body(lam_init, nb, kb_step, qt_ref, k_ref, vt_ref, bias_ref, dl_ref, go_ref, o_ref,
                    acc_ref, m_ref, l_ref, sa_ref, sb_ref, xa_ref, xb_ref):
    i = pl.program_id(1)
    row = lax.broadcasted_iota(i32, (BLK, 1), 0)
    top = row < 64
    zero = jnp.zeros((), bf16)
    qcat = []
    for p in range(2):
        qp = qt_ref[p * BLK:(p + 1) * BLK, :]
        qcat.append([jnp.concatenate(
            [jnp.where((row >= 64 * e + 32 * c) & (row < 64 * e + 32 * c + 32), qp, zero) for c in range(2)],
            axis=1) for e in range(2)])
    rows_k = lax.broadcasted_iota(i32, (kb_step * BLK, 1), 0)
    padmask = jnp.where(rows_k < PAD, NEG, 0.0).astype(f32)

    m_ref[...] = jnp.full(m_ref.shape, 3 * NEG, f32)
    l_ref[...] = jnp.zeros(l_ref.shape, f32)
    acc_ref[...] = jnp.zeros(acc_ref.shape, f32)

    def blocks(s):
        j0 = s * kb_step
        return [jnp.minimum(j0 + kb, nb - 1) for kb in range(kb_step)], j0

    def scores(s, first, s_ref, x_ref):
        blks, j0 = blocks(s)
        dix = [jnp.where(i - (j0 + kb) >= 0, i - (j0 + kb), nb) for kb in range(kb_step)]
        kparts = [k_ref[pl.ds(pl.multiple_of(b * BLK, BLK), BLK), :] for b in blks]
        for p in range(2):
            kcat = jnp.concatenate([kp[:, p * BLK:(p + 1) * BLK] for kp in kparts], axis=0)
            for e in range(2):
                h = 2 * p + e
                st2 = _dot(kcat, qcat[p][e])
                bias = jnp.concatenate([bias_ref[h, dd] for dd in dix], axis=0)
                if first:
                    bias = bias + padmask
                for c in range(2):
                    tile = st2[:, c * BLK:(c + 1) * BLK] + bias
                    s_ref[2 * h + c] = tile
                    x_ref[2 * h + c:2 * h + c + 1, :] = jnp.max(tile, axis=0, keepdims=True)

    def consume(s, s_ref, x_ref):
        blks, _ = blocks(s)
        vparts = [vt_ref[b] for b in blks]
        for p in range(2):
            vtc = jnp.concatenate([vp[p * BLK:(p + 1) * BLK, :] for vp in vparts], axis=1)
            lhs = jnp.concatenate([jnp.where(top, vtc, zero), jnp.where(top, zero, vtc)], axis=1)
            for c in range(2):
                pts, alphas = [], []
                for e in range(2):
                    idx = 2 * (2 * p + e) + c
                    m_old = m_ref[idx:idx + 1, :]
                    m_new = jnp.maximum(m_old, x_ref[idx:idx + 1, :])
                    alpha = jnp.exp2(m_old - m_new)
                    pt = jnp.exp2(s_ref[idx] - m_new)
                    l_ref[idx:idx + 1, :] = alpha * l_ref[idx:idx + 1, :] + jnp.sum(pt, axis=0, keepdims=True)
                    m_ref[idx:idx + 1, :] = m_new
                    pts.append(pt.astype(bf16))
                    alphas.append(alpha)
                upd = _dot(lhs, jnp.concatenate(pts, axis=0))
                a = 2 * p + c
                acc_ref[a, 0:64, :] = acc_ref[a, 0:64, :] * alphas[0] + upd[0:64]
                acc_ref[a, 64:128, :] = acc_ref[a, 64:128, :] * alphas[1] + upd[64:128]

    n_steps = (i + kb_step) // kb_step
    scores(0, True, sa_ref, xa_ref)

    def body(s, carry):
        @pl.when(s % 2 == 0)
        def _():
            scores(s + 1, False, sb_ref, xb_ref)
            consume(s, sa_ref, xa_ref)

        @pl.when(s % 2 == 1)
        def _():
            scores(s + 1, False, sa_ref, xa_ref)
            consume(s, sb_ref, xb_ref)

        return carry

    lax.fori_loop(0, n_steps - 1, body, 0)
    last = n_steps - 1

    @pl.when(last % 2 == 0)
    def _():
        consume(last, sa_ref, xa_ref)

    @pl.when(last % 2 == 1)
    def _():
        consume(last, sb_ref, xb_ref)

    dl = dl_ref[...]
    lam = (jnp.exp(jnp.sum(dl[0:1] * dl[1:2], axis=-1, keepdims=True))
           - jnp.exp(jnp.sum(dl[2:3] * dl[3:4], axis=-1, keepdims=True)) + lam_init)
    for p in range(2):
        parts = []
        for e in range(2):
            h = 2 * p + e
            rows = slice(64 * e, 64 * e + 64)
            o = (acc_ref[2 * p, rows, :] / l_ref[2 * h:2 * h + 1, :]
                 - lam * (acc_ref[2 * p + 1, rows, :] / l_ref[2 * h + 1:2 * h + 2, :]))
            ssq = jnp.sum(o * o, axis=0, keepdims=True)
            parts.append(o * lax.rsqrt(ssq * (1.0 / 64) + EPS))
        o = jnp.concatenate(parts, axis=0) * go_ref[...] * (1.0 - lam_init)
        o_ref[:, p * BLK:(p + 1) * BLK] = o.T.astype(bf16)


def _diff_attn(ft, z3, bias_bt, dl, go, lam_init, B, nb, kb_step):
    L = nb * BLK
    return pl.pallas_call(
        functools.partial(_diff_attn_body, lam_init, nb, kb_step),
        grid=(B, nb),
        in_specs=[pl.BlockSpec((None, 256, BLK), lambda b, i: (b * nb + i, R_QB // 256, 0)),
                  pl.BlockSpec((None, L, 256), lambda b, i: (b, 0, 1)),
                  pl.BlockSpec((nb, 256, BLK), lambda b, i: (b, R_VB // 256, 0)),
                  pl.BlockSpec((B_HEADS, nb + 1, BLK, BLK), lambda b, i: (0, 0, 0, 0)),
                  pl.BlockSpec((4, 32), lambda b, i: (0, 0)),
                  pl.BlockSpec((BLK, 1), lambda b, i: (0, 0))],
        out_specs=pl.BlockSpec((BLK, 256), lambda b, i: (b * nb + i, 0)),
        out_shape=jax.ShapeDtypeStruct((B * L, 256), bf16),
        scratch_shapes=[pltpu.VMEM((4, BLK, BLK), f32), pltpu.VMEM((8, BLK), f32), pltpu.VMEM((8, BLK), f32),
                        pltpu.VMEM((8, kb_step * BLK, BLK), f32), pltpu.VMEM((8, kb_step * BLK, BLK), f32),
                        pltpu.VMEM((8, BLK), f32), pltpu.VMEM((8, BLK), f32)],
        compiler_params=_cparams(("parallel", "arbitrary")),
        name="diff_attn",
    )(ft, z3, ft, bias_bt, dl, go)


def _outproj_common(oa_ref, ob_ref, oc_ref, w_ref, h_ref, g_ref):
    y = (_dot(oa_ref[...], w_ref[0:512, :]) + _dot(ob_ref[...], w_ref[512:768, :])
         + _dot(oc_ref[...], w_ref[768:1024, :]))
    hn = h_ref[...] + y
    ms = jnp.mean(hn * hn, axis=-1, keepdims=True)
    return hn, hn * lax.rsqrt(ms + EPS) * g_ref[...]


def _outproj_ffn_body(tf, oa_ref, ob_ref, oc_ref, w_ref, h_ref, g_ref, wg_ref, wu_ref, wd_ref, o_ref):
    hn, xn = _outproj_common(oa_ref, ob_ref, oc_ref, w_ref, h_ref, g_ref)
    x = xn.astype(bf16)
    acc = hn
    for c in range(wg_ref.shape[1] // tf):
        g = _dot(x, wg_ref[:, c * tf:(c + 1) * tf])
        u = _dot(x, wu_ref[:, c * tf:(c + 1) * tf])
        a = (g * jax.nn.sigmoid(g) * u).astype(bf16)
        acc = acc + _dot(a, wd_ref[c * tf:(c + 1) * tf, :])
    o_ref[...] = acc


def _outproj_router_body(nb, tm, oa_ref, ob_ref, oc_ref, w_ref, h_ref, g_ref, wrh_ref, wrl_ref, tri_ref,
                         ho_ref, xn_ref, info_ref, cnt_ref):
    step = pl.program_id(0)
    hn, xn = _outproj_common(oa_ref, ob_ref, oc_ref, w_ref, h_ref, g_ref)
    ho_ref[...] = hn
    xn_ref[...] = xn.astype(bf16)

    hi, lo = _split_bf16(xn)
    logits = _dot(hi, wrh_ref[...]) + _dot(lo, wrh_ref[...]) + _dot(hi, wrl_ref[...])
    lane = lax.broadcasted_iota(i32, (1, LANES), 1)
    ninf = -jnp.inf
    lg = jnp.where(lane < N_EXPERTS, logits, ninf)
    m1 = jnp.max(lg, axis=-1, keepdims=True)
    i1 = jnp.min(jnp.where(lg == m1, lane, LANES), axis=-1, keepdims=True)
    lg2 = jnp.where(lane == i1, ninf, lg)
    m2 = jnp.max(lg2, axis=-1, keepdims=True)
    i2 = jnp.min(jnp.where(lg2 == m2, lane, LANES), axis=-1, keepdims=True)
    e21 = jnp.exp(m2 - m1)
    g1 = 1.0 / (1.0 + e21)
    g2 = e21 * g1

    rowin = lax.broadcasted_iota(i32, (BLK, 1), 0)
    valids = []
    for k in range(tm // BLK):
        blk = step * (tm // BLK) + k
        valids.append(jnp.logical_or(blk % nb != 0, rowin >= PAD))
    valid = jnp.concatenate(valids, axis=0)
    oh = jnp.where(valid & ((lane == i1) | (lane == i2)), 1.0, 0.0)

    @pl.when(step == 0)
    def _():
        cnt_ref[...] = jnp.zeros_like(cnt_ref)

    base = cnt_ref[...]
    rank = _dot(tri_ref[...], oh.astype(bf16)) + base
    r1 = jnp.sum(jnp.where(lane == i1, rank, 0.0), axis=-1, keepdims=True)
    r2 = jnp.sum(jnp.where(lane == i2, rank, 0.0), axis=-1, keepdims=True)
    cnt_ref[...] = base + jnp.sum(oh, axis=0, keepdims=True)
    info = jnp.where(lane == 0, i1.astype(f32),
           jnp.where(lane == 1, i2.astype(f32),
           jnp.where(lane == 2, g1,
           jnp.where(lane == 3, g2,
           jnp.where(lane == 4, r1,
           jnp.where(lane == 5, r2, 0.0))))))
    info_ref[...] = info


def _outproj_ffn(oa, ob, oc, w, h, g, wg, wu, wd, tm, tf):
    n = h.shape[0]
    F = wg.shape[1]
    row = lambda c: pl.BlockSpec((tm, c), lambda i: (i, 0))
    const = lambda r, c: pl.BlockSpec((r, c), lambda i: (0, 0), pipeline_mode=pl.Buffered(1))
    return pl.pallas_call(
        functools.partial(_outproj_ffn_body, tf),
        grid=(n // tm,),
        in_specs=[row(512), row(256), row(256), const(D, D), row(D), const(1, D),
                  const(D, F), const(D, F), const(F, D)],
        out_specs=row(D),
        out_shape=jax.ShapeDtypeStruct((n, D), f32),
        compiler_params=_cparams(("parallel",)),
        name="outproj_ffn",
    )(oa, ob, oc, w, h, g, wg, wu, wd)


def _outproj_router(oa, ob, oc, w, h, g, tm, router, nb):
    n = h.shape[0]
    row = lambda c: pl.BlockSpec((tm, c), lambda i: (i, 0))
    const = lambda r, c: pl.BlockSpec((r, c), lambda i: (0, 0))
    in_specs = [row(512), row(256), row(256), const(D, D), row(D), const(1, D)]
    wrh, wrl, tri = router
    return pl.pallas_call(
        functools.partial(_outproj_router_body, nb, tm),
        grid=(n // tm,),
        in_specs=in_specs + [const(D, LANES), const(D, LANES), const(tm, tm)],
        out_specs=[row(D), row(D), row(LANES), const(1, LANES)],
        out_shape=[jax.ShapeDtypeStruct((n, D), f32), jax.ShapeDtypeStruct((n, D), bf16),
                   jax.ShapeDtypeStruct((n, LANES), f32), jax.ShapeDtypeStruct((1, LANES), f32)],
        compiler_params=_cparams(("arbitrary",)),
        name="outproj_router",
    )(oa, ob, oc, w, h, g, wrh, wrl, tri)


ROW_ALIGN = 16
TAB_BASE, TAB_LO, TAB_CNT, TAB_W = 0, N_EXPERTS, 2 * N_EXPERTS, 4 * N_EXPERTS


def _sorted_rows(tb):
    return -(-(2 * tb + N_EXPERTS * (ROW_ALIGN - 1)) // LANES) * LANES


def _for_each_run(tab_ref, tb, fn):
    bits = []
    b = pl.next_power_of_2(2 * tb)
    while b >= ROW_ALIGN:
        bits.append(b)
        b //= 2
    for e in range(N_EXPERTS):
        base = tab_ref[0, 0, TAB_BASE + e]
        lo = tab_ref[0, 0, TAB_LO + e]
        cnt = tab_ref[0, 0, TAB_CNT + e]
        off = 0
        for bit in bits:
            piece = cnt & bit

            @pl.when(piece != 0)
            def _(base=base, lo=lo, off=off, bit=bit):
                fn(pl.multiple_of(base + off, ROW_ALIGN), pl.multiple_of(lo + off, ROW_ALIGN), bit)

            off = off + piece


def _dispatch_body(nblk, tb, tab_ref, tabp_ref, lp_ref, x_ref, xs_in_ref, xs_ref, buf_ref, sem):
    del xs_in_ref
    s = pl.program_id(0)
    slot = s % 2
    lp = lp_ref[0]
    r = lax.broadcasted_iota(i32, (buf_ref.shape[1], 1), 0)
    perm = jnp.where((lp[0:1, :] == r) | (lp[1:2, :] == r), 1.0, 0.0).astype(bf16)
    buf_ref[slot] = _dot(perm, x_ref[...]).astype(bf16)

    def copy(sl):
        return lambda base, lo, rows: pltpu.make_async_copy(
            buf_ref.at[sl, pl.ds(lo, rows)], xs_ref.at[pl.ds(base, rows)], sem.at[sl])

    _for_each_run(tab_ref, tb, lambda *a: copy(slot)(*a).start())

    @pl.when(s > 0)
    def _():
        _for_each_run(tabp_ref, tb, lambda *a: copy(1 - slot)(*a).wait())

    @pl.when(s == nblk - 1)
    def _():
        _for_each_run(tab_ref, tb, lambda *a: copy(slot)(*a).wait())


def _dispatch(xn, tab, lp_rows, xs0, tb):
    nblk = tab.shape[0]
    return pl.pallas_call(
        functools.partial(_dispatch_body, nblk, tb),
        grid=(nblk,),
        in_specs=[pl.BlockSpec((1, 1, TAB_W), lambda s: (s, 0, 0), memory_space=pltpu.SMEM),
                  pl.BlockSpec((1, 1, TAB_W), lambda s: (jnp.maximum(s - 1, 0), 0, 0), memory_space=pltpu.SMEM),
                  pl.BlockSpec((1, 8, tb), lambda s: (s, 0, 0)),
                  pl.BlockSpec((tb, D), lambda s: (s, 0)),
                  pl.BlockSpec(memory_space=pl.ANY)],
        out_specs=pl.BlockSpec(memory_space=pl.ANY),
        out_shape=jax.ShapeDtypeStruct(xs0.shape, bf16),
        scratch_shapes=[pltpu.VMEM((2, _sorted_rows(tb), D), bf16), pltpu.SemaphoreType.DMA((2,))],
        input_output_aliases={4: 0},
        compiler_params=pltpu.CompilerParams(dimension_semantics=("arbitrary",), has_side_effects=True,
                                             vmem_limit_bytes=VMEM_LIMIT),
        name="moe_dispatch",
    )(tab, tab, lp_rows, xn, xs0)


def _experts_body(nf, sub, te_ref, tv_ref, x_ref, wg_ref, wu_ref, wd_ref, y_ref, acc_ref):
    del te_ref
    t = pl.program_id(0)
    f = pl.program_id(1)
    valid = tv_ref[t] == 1

    @pl.when(valid & (f == 0))
    def _():
        acc_ref[...] = jnp.zeros_like(acc_ref)

    @pl.when(valid)
    def _():
        x = x_ref[...]
        acc = acc_ref[...]
        for k in range(wg_ref.shape[1] // sub):
            g = _dot(x, wg_ref[:, k * sub:(k + 1) * sub])
            u = _dot(x, wu_ref[:, k * sub:(k + 1) * sub])
            a = (g * jax.nn.sigmoid(g) * u).astype(bf16)
            acc = acc + _dot(a, wd_ref[k * sub:(k + 1) * sub, :])
        acc_ref[...] = acc

    @pl.when(valid & (f == nf - 1))
    def _():
        y_ref[...] = acc_ref[...].astype(bf16)

    @pl.when(jnp.logical_not(valid) & (f == nf - 1))
    def _():
        y_ref[...] = jnp.zeros_like(y_ref)


def _experts(xs, te, tv, wg, wu, wd, tme, tf, sub):
    rows = xs.shape[0]
    T = rows // tme
    nf = D_FF_EXPERT // tf
    grid_spec = pltpu.PrefetchScalarGridSpec(
        num_scalar_prefetch=2,
        grid=(T, nf),
        in_specs=[pl.BlockSpec((tme, D), lambda t, f, te, tv: (t, 0)),
                  pl.BlockSpec((None, D, tf), lambda t, f, te, tv: (te[t], 0, f)),
                  pl.BlockSpec((None, D, tf), lambda t, f, te, tv: (te[t], 0, f)),
                  pl.BlockSpec((None, tf, D), lambda t, f, te, tv: (te[t], f, 0))],
        out_specs=pl.BlockSpec((tme, D), lambda t, f, te, tv: (t, 0)),
        scratch_shapes=[pltpu.VMEM((tme, D), f32)],
    )
    return pl.pallas_call(
        functools.partial(_experts_body, nf, sub),
        grid_spec=grid_spec,
        out_shape=jax.ShapeDtypeStruct((rows, D), bf16),
        compiler_params=_cparams(("arbitrary", "arbitrary")),
        name="moe_experts",
    )(te, tv, xs, wg, wu, wd)


def _combine_body(nblk, tb, nb, tab_ref, tabn_ref, aux_ref, h_ref, ys_ref, o_ref, buf_ref, obuf_ref, sem, osem):
    s = pl.program_id(0)
    slot = s % 2
    sub = tb // BLK

    def copy(sl):
        return lambda base, lo, rows: pltpu.make_async_copy(
            ys_ref.at[pl.ds(base, rows)], buf_ref.at[sl, pl.ds(lo, rows)], sem.at[sl])

    @pl.when(s == 0)
    def _():
        _for_each_run(tab_ref, tb, lambda *a: copy(slot)(*a).start())

    @pl.when(s + 1 < nblk)
    def _():
        _for_each_run(tabn_ref, tb, lambda *a: copy(1 - slot)(*a).start())

    _for_each_run(tab_ref, tb, lambda *a: copy(slot)(*a).wait())

    total = tab_ref[0, 0, TAB_LO + N_EXPERTS - 1] + tab_ref[0, 0, TAB_CNT + N_EXPERTS - 1]
    rows = lax.broadcasted_iota(i32, (buf_ref.shape[1], 1), 0)
    ys = jnp.where(rows < total, buf_ref[slot], jnp.zeros((), bf16))
    aux = aux_ref[...]
    lane = lax.broadcasted_iota(i32, (1, buf_ref.shape[1]), 1)
    w = (jnp.where(lane == aux[:, 0:1].astype(i32), aux[:, 2:3], 0.0)
         + jnp.where(lane == aux[:, 1:2].astype(i32), aux[:, 3:4], 0.0))
    hi, lo = _split_bf16(w)
    obuf_ref[slot] = h_ref[...] + _dot(hi, ys) + _dot(lo, ys)

    def out_copies(step, sl, fn):
        for q in range(sub):
            pb = step * sub + q

            @pl.when(pb % nb != 0)
            def _(pb=pb, q=q):
                orow = pl.multiple_of(((pb // nb) * (nb - 1) + pb % nb - 1) * BLK, BLK)
                fn(pltpu.make_async_copy(obuf_ref.at[sl, pl.ds(q * BLK, BLK)], o_ref.at[pl.ds(orow, BLK)],
                                         osem.at[sl]))

    out_copies(s, slot, lambda c: c.start())

    @pl.when(s > 0)
    def _():
        out_copies(s - 1, 1 - slot, lambda c: c.wait())

    @pl.when(s == nblk - 1)
    def _():
        out_copies(s, slot, lambda c: c.wait())


def _combine(h, aux, tab, ys, B, nb, tb):
    nblk = tab.shape[0]
    return pl.pallas_call(
        functools.partial(_combine_body, nblk, tb, nb),
        grid=(nblk,),
        in_specs=[pl.BlockSpec((1, 1, TAB_W), lambda s: (s, 0, 0), memory_space=pltpu.SMEM),
                  pl.BlockSpec((1, 1, TAB_W), lambda s: (jnp.minimum(s + 1, nblk - 1), 0, 0),
                               memory_space=pltpu.SMEM),
                  pl.BlockSpec((tb, LANES), lambda s: (s, 0)),
                  pl.BlockSpec((tb, D), lambda s: (s, 0)),
                  pl.BlockSpec(memory_space=pl.ANY)],
        out_specs=pl.BlockSpec(memory_space=pl.ANY),
        out_shape=jax.ShapeDtypeStruct((B * (nb - 1) * BLK, D), f32),
        scratch_shapes=[pltpu.VMEM((2, _sorted_rows(tb), D), bf16), pltpu.VMEM((2, tb, D), f32),
                        pltpu.SemaphoreType.DMA((2,)), pltpu.SemaphoreType.DMA((2,))],
        compiler_params=_cparams(("arbitrary",)),
        name="moe_combine",
    )(tab, tab, aux, h, ys)


def _routing_tables(info, cnt_rows, B, nb, tb, tme):
    n = info.shape[0]
    nblk = n // tb
    ar = jnp.arange(N_EXPERTS, dtype=i32)
    e1 = info[:, 0].astype(i32)
    e2 = info[:, 1].astype(i32)
    blk = jnp.arange(n // BLK, dtype=i32)[:, None]
    rowin = jnp.arange(BLK, dtype=i32)[None, :]
    real = ((blk % nb != 0) | (rowin >= PAD)).reshape(n)
    oh1 = (e1[:, None] == ar) & real[:, None]
    oh2 = (e2[:, None] == ar) & real[:, None]
    blk_cnt = (oh1 | oh2).astype(i32).reshape(nblk, tb, N_EXPERTS).sum(axis=1)
    blk_rank0 = jnp.cumsum(blk_cnt, axis=0) - blk_cnt
    run = ((blk_cnt + ROW_ALIGN - 1) // ROW_ALIGN) * ROW_ALIGN
    run0 = jnp.cumsum(run, axis=0) - run
    seg = ((jnp.sum(run, axis=0) + tme - 1) // tme) * tme
    ends = jnp.cumsum(seg)
    offs = ends - seg
    base = offs[None, :] + run0
    lo = jnp.cumsum(run, axis=1) - run
    tab = jnp.concatenate([base, lo, run, jnp.zeros_like(base)], axis=1).reshape(nblk, 1, TAB_W)

    shift = jnp.repeat(lo - blk_rank0, tb, axis=0)
    lp1 = jnp.where(real, jnp.sum(jnp.where(oh1, shift, 0), axis=1) + info[:, 4].astype(i32), -1)
    lp2 = jnp.where(real, jnp.sum(jnp.where(oh2, shift, 0), axis=1) + info[:, 5].astype(i32), -1)
    lp_rows = jnp.concatenate([lp1.reshape(nblk, 1, tb), lp2.reshape(nblk, 1, tb),
                               jnp.full((nblk, 6, tb), -1, i32)], axis=1)
    aux = jnp.concatenate([lp1.astype(f32)[:, None], lp2.astype(f32)[:, None], info[:, 2:4],
                           jnp.zeros((n, LANES - 4), f32)], axis=1)
    del cnt_rows
    return tab, lp_rows, aux, ends


def _strip_body(h_ref, o_ref):
    o_ref[...] = h_ref[...]


def _strip_meta(h, B, nb):
    nt = nb - 1
    return pl.pallas_call(
        _strip_body,
        grid=(B, nt),
        in_specs=[pl.BlockSpec((BLK, D), lambda b, i: (b * nb + i + 1, 0))],
        out_specs=pl.BlockSpec((BLK, D), lambda b, i: (b * nt + i, 0)),
        out_shape=jax.ShapeDtypeStruct((B * nt * BLK, D), f32),
        compiler_params=_cparams(("parallel", "arbitrary")),
        name="strip_meta",
    )(h)


def _bucket_lookup(table, bucket):
    tshape = (table.shape[1],) + (1,) * bucket.ndim
    out = jnp.zeros((table.shape[1],) + bucket.shape, f32)
    for k in range(N_BUCKETS):
        out = out + jnp.where((bucket == k)[None], table[k].astype(f32).reshape(tshape), 0.0)
    return out


def _bias_tables(rel, nb):
    table_a = rel[:, :A_HEADS] * LOG2E
    table_b = rel[:, A_HEADS:] * LOG2E
    kk = jnp.arange(2 * BLK)[:, None]
    qq = jnp.arange(BLK)[None, :]
    n = BLK + qq - kk
    in_win = (n >= 0) & (n < BLK)
    bias = _bucket_lookup(table_a, _t5_bucket(jnp.maximum(n, 0)))
    variants = []
    for blk in range(3):
        key_ok = (blk - 1) * BLK + kk >= PAD
        variants.append(jnp.where((in_win & key_ok)[None], bias, NEG))
    bias_at = jnp.stack(variants, axis=0)

    d = jnp.arange(nb + 1)[:, None, None]
    kb = jnp.arange(BLK)[None, :, None]
    qb = jnp.arange(BLK)[None, None, :]
    nn = jnp.where(d < nb, d * BLK + qb - kb, -1)
    bias_bt = _bucket_lookup(table_b, _t5_bucket(jnp.maximum(nn, 0)))
    bias_bt = jnp.where((nn >= 0)[None], bias_bt, NEG)
    return bias_at, bias_bt


def _block_diag_ones(gs):
    idx = np.arange(LANES) // gs
    return jnp.asarray((idx[:, None] == idx[None, :]).astype(np.float32), dtype=bf16)


def _col_scale(gain_a, gain_b):
    one = jnp.ones((256,), f32)
    cs = jnp.concatenate([one, jnp.tile(gain_b[1], 8), jnp.tile(gain_a[1], 2)]).reshape(1, D_ROW)
    cst = jnp.concatenate([jnp.tile(gain_a[0], 8) * (64 ** -0.5 * LOG2E),
                           jnp.tile(gain_b[0], 8) * (32 ** -0.5 * LOG2E)]).reshape(R_VB, 1)
    return cs.astype(f32), cst.astype(f32)


def _split_in_weight(w):
    main = jnp.concatenate([w[:, 1536:1792], w[:, 1024:1280], w[:, 512:640]], axis=1)
    tr = jnp.concatenate([w[:, 0:512], w[:, 768:1024], w[:, 1280:1536], w[:, 640:768]], axis=1).T
    return main.astype(bf16), tr.astype(bf16)


def _pool_weight(w_pool):
    out = jnp.zeros((256, 256), f32)
    for gi in range(4):
        out = out.at[gi * 64:(gi + 1) * 64, gi * 64:(gi + 1) * 64].set(w_pool[gi])
    return out.astype(bf16)


def _row_tile(n):
    for tm in (512, 256, 128):
        if n % tm == 0:
            return tm
    raise ValueError(n)


def kernel(x, meta_tokens, rel_bias_table, g_mix, w_in, qk_gain_a, sinks, qk_gain_b, diff_lambda, g_diff_out,
           w_pool, pool_scale, w_out, g_ffn, w_ffn_gate, w_ffn_up, w_ffn_down, w_router, w_exp_gate, w_exp_up,
           w_exp_down):
    B, seq, _ = x.shape
    assert seq % BLK == 0
    depth = g_mix.shape[0]
    nb = seq // BLK + 1
    L = nb * BLK
    n = B * L
    tm = _row_tile(n)
    tm_in = 2 * tm if n % (2 * tm) == 0 else tm
    tme = 1024
    tf_e = 1792
    tf_d = 256
    kb_d = 4

    bias_at, bias_bt = _bias_tables(rel_bias_table, nb)
    g64 = _block_diag_ones(64)
    g32 = _block_diag_ones(32)

    h = _embed(x, meta_tokens, B, nb)
    out = None
    for l in range(depth):
        lam_init = 0.8 - 0.6 * math.exp(-0.3 * l)
        cs, cst = _col_scale(qk_gain_a[l], qk_gain_b[l])
        w_main, w_tr = _split_in_weight(w_in[l])
        z, ft = _inproj(h, g_mix[l].reshape(1, D), w_main, w_tr, cs, cst, g64, g32, tm_in)
        oa, oc = _swa_pool(z, ft, bias_at, sinks[l], _pool_weight(w_pool[l]), pool_scale[l].reshape(1, 256), B, nb)
        ob = _diff_attn(ft, z.reshape(B, L, D_ROW), bias_bt, diff_lambda[l],
                        jnp.tile(g_diff_out[l], 2).reshape(BLK, 1), lam_init, B, nb, kb_d)
        wo = w_out[l].astype(bf16)
        gf = g_ffn[l].reshape(1, D)
        j = l // 2
        if l % 2 == 0:
            h = _outproj_ffn(oa, ob, oc, wo, h, gf, w_ffn_gate[j].astype(bf16), w_ffn_up[j].astype(bf16),
                             w_ffn_down[j].astype(bf16), tm, tf_d)
            out = None
        else:
            assert l == depth - 1, "the expert layer must be the last layer"
            wr = jnp.pad(w_router[j], ((0, 0), (0, LANES - N_EXPERTS)))
            wrh = wr.astype(bf16)
            wrl = (wr - wrh.astype(f32)).astype(bf16)
            tri = jnp.asarray(np.tril(np.ones((tm, tm), np.float32), -1), dtype=bf16)
            h, xn, info, cnt = _outproj_router(oa, ob, oc, wo, h, gf, tm, (wrh, wrl, tri), nb)

            tab, lp_rows, aux, ends = _routing_tables(info, cnt, B, nb, tm, tme)
            rows_max = B * (seq + N_META) * 2 + (n // tm) * N_EXPERTS * (ROW_ALIGN - 1) + N_EXPERTS * (tme - 1)
            T = -(-rows_max // tme)
            starts = jnp.arange(T, dtype=i32) * tme
            te = jnp.minimum(jnp.sum((ends[None, :] <= starts[:, None]).astype(i32), axis=1), N_EXPERTS - 1)
            tv = (starts < ends[-1]).astype(i32)

            xs = _dispatch(xn, tab, lp_rows, jnp.zeros((T * tme, D), bf16), tm)
            ys = _experts(xs, te, tv, w_exp_gate[j].astype(bf16), w_exp_up[j].astype(bf16),
                          w_exp_down[j].astype(bf16), tme, tf_e, 256)
            out = _combine(h, aux, tab, ys, B, nb, tm)
    if out is None:
        out = _strip_meta(h, B, nb)
    return out.reshape(B, seq, D)
```

```python
import functools
import math

import jax
import jax.numpy as jnp
import numpy as np
from jax import lax
from jax.experimental import pallas as pl
from jax.experimental.pallas import tpu as pltpu

f32 = jnp.float32
bf16 = jnp.bfloat16
i32 = jnp.int32

D = 1024
BLK = 128
N_META = 16
PAD = BLK - N_META
A_HEADS = 8
B_HEADS = 4
N_BUCKETS = 32
MAX_DISTANCE = 128
N_EXPERTS = 8
D_FF_EXPERT = 3584
EPS = 1e-6
NEG = -1e30
LANES = 128
D_ROW = 640
D_FM = 1152
R_QB, R_VB, R_VA = 512, 768, 1024
LOG2E = 1.4426950408889634

VMEM_LIMIT = 56 * 1024 * 1024


def _cparams(sem):
    return pltpu.CompilerParams(dimension_semantics=sem, vmem_limit_bytes=VMEM_LIMIT)


def _t5_bucket(n):
    max_exact = N_BUCKETS // 2
    large = max_exact
    for j in range(1, N_BUCKETS - max_exact):
        thr = math.ceil(max_exact * (MAX_DISTANCE / max_exact) ** (j / (N_BUCKETS - max_exact)))
        large = large + (n >= thr).astype(i32)
    return jnp.where(n < max_exact, n, large)


def _dot(a, b):
    return jnp.dot(a, b, preferred_element_type=f32)


def _dot_nt(a, b):
    return lax.dot_general(a, b, (((1,), (1,)), ((), ())), preferred_element_type=f32)


def _embed_body(x_ref, m_ref, o_ref):
    o_ref[0:PAD, :] = jnp.zeros((PAD, D), f32)
    o_ref[PAD:BLK, :] = m_ref[...]
    o_ref[BLK:, :] = x_ref[...]


def _embed(x, meta, B, nb):
    seq = x.shape[1]
    return pl.pallas_call(
        _embed_body,
        grid=(B,),
        in_specs=[pl.BlockSpec((None, seq, D), lambda b: (b, 0, 0)),
                  pl.BlockSpec((N_META, D), lambda b: (0, 0))],
        out_specs=pl.BlockSpec((nb * BLK, D), lambda b: (b, 0)),
        out_shape=jax.ShapeDtypeStruct((B * nb * BLK, D), f32),
        compiler_params=_cparams(("parallel",)),
        name="embed",
    )(x, meta)


def _split_bf16(x):
    hi = x.astype(bf16)
    return hi, (x - hi.astype(f32)).astype(bf16)


def _inproj_body(tm, h_ref, g_ref, w_ref, wt_ref, cs_ref, cst_ref, g64_ref, g32_ref, z_ref, ft_ref):
    sub = tm
    for r0 in range(0, tm, sub):
        rows = slice(r0, r0 + sub)
        x = h_ref[rows, :]
        ms = jnp.mean(x * x, axis=-1, keepdims=True)
        xn = (x * lax.rsqrt(ms + EPS) * g_ref[...]).astype(bf16)

        for c in range(D_ROW // LANES):
            zc = _dot(xn, w_ref[:, c * LANES:(c + 1) * LANES])
            if c >= 2:
                gm, inv = (g32_ref, 1.0 / 32) if c < 4 else (g64_ref, 1.0 / 64)
                ss = _dot((zc * zc).astype(bf16), gm[...])
                zc = zc * lax.rsqrt(ss * inv + EPS)
            z_ref[rows, c * LANES:(c + 1) * LANES] = (zc * cs_ref[:, c * LANES:(c + 1) * LANES]).astype(bf16)

        zt_all = _dot_nt(wt_ref[...], xn)
        for c in range(D_FM // LANES):
            zt = zt_all[c * LANES:(c + 1) * LANES]
            if c < R_VB // LANES:
                gm, inv = (g64_ref, 1.0 / 64) if c < R_QB // LANES else (g32_ref, 1.0 / 32)
                ss = _dot(gm[...], (zt * zt).astype(bf16))
                zt = zt * lax.rsqrt(ss * inv + EPS) * cst_ref[c * LANES:(c + 1) * LANES, :]
            for k in range(sub // BLK):
                ft_ref[r0 // BLK + k, c * LANES:(c + 1) * LANES, :] = zt[:, k * BLK:(k + 1) * BLK].astype(bf16)


def _inproj(h, g, w, wt, cs, cst, g64, g32, tm):
    n = h.shape[0]
    const = lambda i: (0, 0)
    kb = tm // BLK
    return pl.pallas_call(
        functools.partial(_inproj_body, tm),
        grid=(n // tm,),
        in_specs=[pl.BlockSpec((tm, D), lambda i: (i, 0)),
                  pl.BlockSpec((1, D), const),
                  pl.BlockSpec((D, D_ROW), const),
                  pl.BlockSpec((D_FM, D), const),
                  pl.BlockSpec((1, D_ROW), const),
                  pl.BlockSpec((R_VB, 1), const),
                  pl.BlockSpec((LANES, LANES), const),
                  pl.BlockSpec((LANES, LANES), const)],
        out_specs=[pl.BlockSpec((tm, D_ROW), lambda i: (i, 0)),
                   pl.BlockSpec((kb, D_FM, BLK), lambda i: (i, 0, 0))],
        out_shape=[jax.ShapeDtypeStruct((n, D_ROW), bf16),
                   jax.ShapeDtypeStruct((n // BLK, D_FM, BLK), bf16)],
        compiler_params=_cparams(("parallel",)),
        name="norm_inproj",
    )(h, g, w, wt, cs, cst, g64, g32)


def _swa_pool_body(lanes, *refs):
    ins = [refs[7 * s:7 * s + 7] for s in range(lanes)]
    bias_ref, sink_ref, wp_ref, ps_ref, oa_ref, oc_ref = refs[7 * lanes:7 * lanes + 6]
    i = pl.program_id(1)
    zeros = jnp.zeros((64, BLK), bf16)
    row = lax.broadcasted_iota(i32, (BLK, 1), 0)
    t = i * BLK + row - PAD
    lane2 = lax.broadcasted_iota(i32, (1, 256), 1)
    grp2 = lane2 // 64
    win = jnp.where(grp2 == 0, 2, jnp.where(grp2 == 1, 4, jnp.where(grp2 == 2, 8, 16)))
    cnt = jnp.maximum(jnp.minimum(t + 1, win), 1).astype(f32)
    for s in range(lanes):
        qt_ref, kp_ref, kc_ref, vtp_ref, vtc_ref, up_ref, uc_ref = ins[s]
        kcat = jnp.concatenate([kp_ref[...], kc_ref[...]], axis=0)
        vt = jnp.concatenate([vtp_ref[...], vtc_ref[...]], axis=1)
        for grp in range(2):
            pts, inv_ls = [], []
            for hh in range(4):
                h = 4 * grp + hh
                qrows = qt_ref[64 * h:64 * h + 64, :]
                qmt = jnp.concatenate([qrows, zeros] if grp == 0 else [zeros, qrows], axis=0)
                st = _dot(kcat, qmt) + bias_ref[h]
                sink = sink_ref[h] * LOG2E
                m = jnp.maximum(jnp.max(st, axis=0, keepdims=True), sink)
                p = jnp.exp2(st - m)
                inv_ls.append(1.0 / (jnp.sum(p, axis=0, keepdims=True) + jnp.exp2(sink - m)))
                pts.append(p.astype(bf16))
            ot = _dot(vt[64 * grp:64 * grp + 64, :], jnp.concatenate(pts, axis=1))
            for pair in range(2):
                o2 = jnp.concatenate([ot[:, (2 * pair + k) * BLK:(2 * pair + k + 1) * BLK] * inv_ls[2 * pair + k]
                                      for k in range(2)], axis=0)
                pb = 2 * grp + pair
                oa_ref[s, :, pb * LANES:(pb + 1) * LANES] = o2.T.astype(bf16)

        u_cur = jnp.where(t >= 0, uc_ref[...].astype(f32), 0.0)
        u_prev = jnp.where(i > 0, up_ref[BLK - 16:, :].astype(f32), 0.0)
        ext = jnp.concatenate([u_prev, u_cur], axis=0)
        s2 = ext + pltpu.roll(ext, 1, 0)
        s4 = s2 + pltpu.roll(s2, 2, 0)
        s8 = s4 + pltpu.roll(s4, 4, 0)
        s16 = s8 + pltpu.roll(s8, 8, 0)
        sums = jnp.where(grp2 == 0, s2, jnp.where(grp2 == 1, s4, jnp.where(grp2 == 2, s8, s16)))[16:, :]
        dlt = (sums / cnt - u_cur).astype(bf16)
        oc_ref[s] = (_dot(dlt, wp_ref[...]) * ps_ref[...]).astype(bf16)


def _swa_pool(z, ft, bias_at, sinks, wpool_bd, pool_scale, B, nb):
    L = nb * BLK
    lanes = 2 if B % 2 == 0 else 1
    in_specs, operands = [], []
    for s in range(lanes):
        cur = lambda c, s=s: (lambda g, i: ((lanes * g + s) * nb + i, c))
        prev = lambda c, s=s: (lambda g, i: ((lanes * g + s) * nb + jnp.maximum(i - 1, 0), c))
        cur3 = lambda r, s=s: (lambda g, i: ((lanes * g + s) * nb + i, r, 0))
        prev3 = lambda r, s=s: (lambda g, i: ((lanes * g + s) * nb + jnp.maximum(i - 1, 0), r, 0))
        in_specs += [pl.BlockSpec((None, 512, BLK), cur3(0)),
                     pl.BlockSpec((BLK, LANES), prev(4)),
                     pl.BlockSpec((BLK, LANES), cur(4)),
                     pl.BlockSpec((None, BLK, BLK), prev3(R_VA // BLK)),
                     pl.BlockSpec((None, BLK, BLK), cur3(R_VA // BLK)),
                     pl.BlockSpec((BLK, 256), prev(0)),
                     pl.BlockSpec((BLK, 256), cur(0))]
        operands += [ft, z, z, ft, ft, z, z]
    in_specs += [pl.BlockSpec((None, A_HEADS, 2 * BLK, BLK), lambda g, i: (jnp.minimum(i, 2), 0, 0, 0)),
                 pl.BlockSpec(memory_space=pltpu.SMEM),
                 pl.BlockSpec((256, 256), lambda g, i: (0, 0)),
                 pl.BlockSpec((1, 256), lambda g, i: (0, 0))]
    operands += [bias_at, sinks, wpool_bd, pool_scale]
    oa, oc = pl.pallas_call(
        functools.partial(_swa_pool_body, lanes),
        grid=(B // lanes, nb),
        in_specs=in_specs,
        out_specs=[pl.BlockSpec((None, lanes, BLK, 512), lambda g, i: (g, 0, i, 0)),
                   pl.BlockSpec((None, lanes, BLK, 256), lambda g, i: (g, 0, i, 0))],
        out_shape=[jax.ShapeDtypeStruct((B // lanes, lanes, L, 512), bf16),
                   jax.ShapeDtypeStruct((B // lanes, lanes, L, 256), bf16)],
        compiler_params=_cparams(("parallel", "arbitrary")),
        name="swa_pool",
    )(*operands)
    return oa.reshape(B * L, 512), oc.reshape(B * L, 256)


def _diff_attn_body(lam_init, nb, kb_step, lanes, *refs):
    ins = [refs[3 * s:3 * s + 3] for s in range(lanes)]
    bias_ref, dl_ref, go_ref = refs[3 * lanes:3 * lanes + 3]
    o_ref = refs[3 * lanes + 3]
    scr = [refs[3 * lanes + 4 + 7 * s:3 * lanes + 4 + 7 * s + 7] for s in range(lanes)]
    i = pl.program_id(1)
    row = lax.broadcasted_iota(i32, (BLK, 1), 0)
    zero = jnp.zeros((), bf16)
    rows_k = lax.broadcasted_iota(i32, (kb_step * BLK, 1), 0)
    padmask = jnp.where(rows_k < PAD, NEG, 0.0).astype(f32)

    qcat = []
    for s in range(lanes):
        qt_ref = ins[s][0]
        per_lane = []
        for p in range(2):
            qp = qt_ref[p * BLK:(p + 1) * BLK, :]
            per_lane.append([jnp.concatenate(
                [jnp.where((row >= 64 * e + 32 * c) & (row < 64 * e + 32 * c + 32), qp, zero) for c in range(2)],
                axis=1) for e in range(2)])
        qcat.append(per_lane)
        acc_ref, m_ref, l_ref = scr[s][0:3]
        m_ref[...] = jnp.full(m_ref.shape, 3 * NEG, f32)
        l_ref[...] = jnp.zeros(l_ref.shape, f32)
        acc_ref[...] = jnp.zeros(acc_ref.shape, f32)

    def blocks(st):
        j0 = st * kb_step
        return [jnp.minimum(j0 + kb, nb - 1) for kb in range(kb_step)], j0

    def scores(s, st, first, par):
        k_ref = ins[s][1]
        s_ref, x_ref = scr[s][3 + par], scr[s][5 + par]
        blks, j0 = blocks(st)
        dix = [jnp.where(i - (j0 + kb) >= 0, i - (j0 + kb), nb) for kb in range(kb_step)]
        kparts = [k_ref[pl.ds(pl.multiple_of(b * BLK, BLK), BLK), :] for b in blks]
        for p in range(2):
            kcat = jnp.concatenate([kp[:, p * BLK:(p + 1) * BLK] for kp in kparts], axis=0)
            for e in range(2):
                h = 2 * p + e
                st2 = _dot(kcat, qcat[s][p][e])
                bias = jnp.concatenate([bias_ref[h, dd] for dd in dix], axis=0)
                if first:
                    bias = bias + padmask
                for c in range(2):
                    tile = st2[:, c * BLK:(c + 1) * BLK] + bias
                    s_ref[2 * h + c] = tile
                    x_ref[2 * h + c:2 * h + c + 1, :] = jnp.max(tile, axis=0, keepdims=True)

    def consume(s, st, par):
        vt_ref = ins[s][2]
        acc_ref, m_ref, l_ref = scr[s][0:3]
        s_ref, x_ref = scr[s][3 + par], scr[s][5 + par]
        blks, _ = blocks(st)
        vparts = [vt_ref[b] for b in blks]
        for p in range(2):
            vtc = jnp.concatenate([vp[p * BLK:(p + 1) * BLK, :] for vp in vparts], axis=1)
            for c in range(2):
                for e in range(2):
                    idx = 2 * (2 * p + e) + c
                    m_old = m_ref[idx:idx + 1, :]
                    m_new = jnp.maximum(m_old, x_ref[idx:idx + 1, :])
                    alpha = jnp.exp2(m_old - m_new)
                    pt = jnp.exp2(s_ref[idx] - m_new)
                    l_ref[idx:idx + 1, :] = alpha * l_ref[idx:idx + 1, :] + jnp.sum(pt, axis=0, keepdims=True)
                    m_ref[idx:idx + 1, :] = m_new
                    rows = slice(64 * e, 64 * e + 64)
                    upd = _dot(vtc[rows, :], pt.astype(bf16))
                    acc_ref[2 * p + c, rows, :] = acc_ref[2 * p + c, rows, :] * alpha + upd

    n_steps = (i + kb_step) // kb_step
    for s in range(lanes):
        scores(s, 0, True, 0)

    def body(st, carry):
        for par in range(2):
            @pl.when(st % 2 == par)
            def _(par=par):
                for s in range(lanes):
                    scores(s, st + 1, False, 1 - par)
                for s in range(lanes):
                    consume(s, st, par)

        return carry

    lax.fori_loop(0, n_steps - 1, body, 0)
    last = n_steps - 1
    for par in range(2):
        @pl.when(last % 2 == par)
        def _(par=par):
            for s in range(lanes):
                consume(s, last, par)

    dl = dl_ref[...]
    lam = (jnp.exp(jnp.sum(dl[0:1] * dl[1:2], axis=-1, keepdims=True))
           - jnp.exp(jnp.sum(dl[2:3] * dl[3:4], axis=-1, keepdims=True)) + lam_init)
    for s in range(lanes):
        acc_ref, _, l_ref = scr[s][0:3]
        for p in range(2):
            parts = []
            for e in range(2):
                h = 2 * p + e
                rows = slice(64 * e, 64 * e + 64)
                o = (acc_ref[2 * p, rows, :] / l_ref[2 * h:2 * h + 1, :]
                     - lam * (acc_ref[2 * p + 1, rows, :] / l_ref[2 * h + 1:2 * h + 2, :]))
                ssq = jnp.sum(o * o, axis=0, keepdims=True)
                parts.append(o * lax.rsqrt(ssq * (1.0 / 64) + EPS))
            o = jnp.concatenate(parts, axis=0) * go_ref[...] * (1.0 - lam_init)
            o_ref[s, :, p * BLK:(p + 1) * BLK] = o.T.astype(bf16)


def _diff_attn(ft, z3, bias_bt, dl, go, lam_init, B, nb, kb_step):
    L = nb * BLK
    lanes = 2 if B % 2 == 0 else 1
    in_specs, scratch = [], []
    for s in range(lanes):
        bat = lambda g, s=s: lanes * g + s
        in_specs += [pl.BlockSpec((None, 256, BLK), lambda g, i, bat=bat: (bat(g) * nb + i, R_QB // 256, 0)),
                     pl.BlockSpec((None, L, 256), lambda g, i, bat=bat: (bat(g), 0, 1)),
                     pl.BlockSpec((nb, 256, BLK), lambda g, i, bat=bat: (bat(g), R_VB // 256, 0))]
        scratch += [pltpu.VMEM((4, BLK, BLK), f32), pltpu.VMEM((8, BLK), f32), pltpu.VMEM((8, BLK), f32),
                    pltpu.VMEM((8, kb_step * BLK, BLK), f32), pltpu.VMEM((8, kb_step * BLK, BLK), f32),
                    pltpu.VMEM((8, BLK), f32), pltpu.VMEM((8, BLK), f32)]
    in_specs += [pl.BlockSpec((B_HEADS, nb + 1, BLK, BLK), lambda g, i: (0, 0, 0, 0)),
                 pl.BlockSpec((4, 32), lambda g, i: (0, 0)),
                 pl.BlockSpec((BLK, 1), lambda g, i: (0, 0))]
    operands = [ft, z3, ft] * lanes + [bias_bt, dl, go]
    return pl.pallas_call(
        functools.partial(_diff_attn_body, lam_init, nb, kb_step, lanes),
        grid=(B // lanes, nb),
        in_specs=in_specs,
        out_specs=pl.BlockSpec((None, lanes, BLK, 256), lambda g, i: (g, 0, i, 0)),
        out_shape=jax.ShapeDtypeStruct((B // lanes, lanes, L, 256), bf16),
        scratch_shapes=scratch,
        compiler_params=_cparams(("parallel", "arbitrary")),
        name="diff_attn",
    )(*operands).reshape(B * L, 256)


def _outproj_common(oa_ref, ob_ref, oc_ref, w_ref, h_ref, g_ref, rows=slice(None)):
    y = (_dot(oa_ref[rows, :], w_ref[0:512, :]) + _dot(ob_ref[rows, :], w_ref[512:768, :])
         + _dot(oc_ref[rows, :], w_ref[768:1024, :]))
    hn = h_ref[rows, :] + y
    ms = jnp.mean(hn * hn, axis=-1, keepdims=True)
    return hn, hn * lax.rsqrt(ms + EPS) * g_ref[...]


def _outproj_ffn_body(tf, oa_ref, ob_ref, oc_ref, w_ref, h_ref, g_ref, wg_ref, wu_ref, wd_ref, o_ref):
    hn, xn = _outproj_common(oa_ref, ob_ref, oc_ref, w_ref, h_ref, g_ref)
    x = xn.astype(bf16)
    acc = hn
    for c in range(wg_ref.shape[1] // tf):
        g = _dot(x, wg_ref[:, c * tf:(c + 1) * tf])
        u = _dot(x, wu_ref[:, c * tf:(c + 1) * tf])
        a = (g * jax.nn.sigmoid(g) * u).astype(bf16)
        acc = acc + _dot(a, wd_ref[c * tf:(c + 1) * tf, :])
    o_ref[...] = acc


def _outproj_router_body(nb, tm, oa_ref, ob_ref, oc_ref, w_ref, h_ref, g_ref, wrh_ref, wrl_ref, tri_ref,
                         ho_ref, xn_ref, info_ref, cnt_ref):
    step = pl.program_id(0)
    th = tri_ref.shape[0]
    lane = lax.broadcasted_iota(i32, (1, LANES), 1)
    rowin = lax.broadcasted_iota(i32, (BLK, 1), 0)
    ninf = -jnp.inf

    @pl.when(step == 0)
    def _():
        cnt_ref[...] = jnp.zeros_like(cnt_ref)

    base = cnt_ref[...]
    for r0 in range(0, tm, th):
        rows = slice(r0, r0 + th)
        hn, xn = _outproj_common(oa_ref, ob_ref, oc_ref, w_ref, h_ref, g_ref, rows)
        ho_ref[rows, :] = hn
        xn_ref[rows, :] = xn.astype(bf16)

        hi, lo = _split_bf16(xn)
        logits = _dot(hi, wrh_ref[...]) + _dot(lo, wrh_ref[...]) + _dot(hi, wrl_ref[...])
        lg = jnp.where(lane < N_EXPERTS, logits, ninf)
        m1 = jnp.max(lg, axis=-1, keepdims=True)
        i1 = jnp.min(jnp.where(lg == m1, lane, LANES), axis=-1, keepdims=True)
        lg2 = jnp.where(lane == i1, ninf, lg)
        m2 = jnp.max(lg2, axis=-1, keepdims=True)
        i2 = jnp.min(jnp.where(lg2 == m2, lane, LANES), axis=-1, keepdims=True)
        e21 = jnp.exp(m2 - m1)
        g1 = 1.0 / (1.0 + e21)
        g2 = e21 * g1

        valids = []
        for k in range(th // BLK):
            blk = step * (tm // BLK) + r0 // BLK + k
            valids.append(jnp.logical_or(blk % nb != 0, rowin >= PAD))
        valid = jnp.concatenate(valids, axis=0)
        oh = jnp.where(valid & ((lane == i1) | (lane == i2)), 1.0, 0.0)
        rank = _dot(tri_ref[...], oh.astype(bf16)) + base
        r1 = jnp.sum(jnp.where(lane == i1, rank, 0.0), axis=-1, keepdims=True)
        r2 = jnp.sum(jnp.where(lane == i2, rank, 0.0), axis=-1, keepdims=True)
        base = base + jnp.sum(oh, axis=0, keepdims=True)
        info_ref[rows, :] = jnp.where(lane == 0, i1.astype(f32),
                            jnp.where(lane == 1, i2.astype(f32),
                            jnp.where(lane == 2, g1,
                            jnp.where(lane == 3, g2,
                            jnp.where(lane == 4, r1,
                            jnp.where(lane == 5, r2, 0.0))))))
    cnt_ref[...] = base


def _outproj_ffn(oa, ob, oc, w, h, g, wg, wu, wd, tm, tf):
    n = h.shape[0]
    F = wg.shape[1]
    row = lambda c: pl.BlockSpec((tm, c), lambda i: (i, 0))
    const = lambda r, c: pl.BlockSpec((r, c), lambda i: (0, 0), pipeline_mode=pl.Buffered(1))
    return pl.pallas_call(
        functools.partial(_outproj_ffn_body, tf),
        grid=(n // tm,),
        in_specs=[row(512), row(256), row(256), const(D, D), row(D), const(1, D),
                  const(D, F), const(D, F), const(F, D)],
        out_specs=row(D),
        out_shape=jax.ShapeDtypeStruct((n, D), f32),
        compiler_params=_cparams(("parallel",)),
        name="outproj_ffn",
    )(oa, ob, oc, w, h, g, wg, wu, wd)


def _outproj_router(oa, ob, oc, w, h, g, tm, router, nb):
    n = h.shape[0]
    row = lambda c: pl.BlockSpec((tm, c), lambda i: (i, 0))
    const = lambda r, c: pl.BlockSpec((r, c), lambda i: (0, 0))
    in_specs = [row(512), row(256), row(256), const(D, D), row(D), const(1, D)]
    wrh, wrl, tri = router
    return pl.pallas_call(
        functools.partial(_outproj_router_body, nb, tm),
        grid=(n // tm,),
        in_specs=in_specs + [const(D, LANES), const(D, LANES), const(*tri.shape)],
        out_specs=[row(D), row(D), row(LANES), const(1, LANES)],
        out_shape=[jax.ShapeDtypeStruct((n, D), f32), jax.ShapeDtypeStruct((n, D), bf16),
                   jax.ShapeDtypeStruct((n, LANES), f32), jax.ShapeDtypeStruct((1, LANES), f32)],
        compiler_params=_cparams(("arbitrary",)),
        name="outproj_router",
    )(oa, ob, oc, w, h, g, wrh, wrl, tri)


ROW_ALIGN = 16
TAB_BASE, TAB_LO, TAB_CNT, TAB_W = 0, N_EXPERTS, 2 * N_EXPERTS, 4 * N_EXPERTS


def _sorted_rows(tb):
    return -(-(2 * tb + N_EXPERTS * (ROW_ALIGN - 1)) // LANES) * LANES


def _for_each_run(tab_ref, tb, fn):
    bits = []
    b = pl.next_power_of_2(2 * tb)
    while b >= ROW_ALIGN:
        bits.append(b)
        b //= 2
    for e in range(N_EXPERTS):
        base = tab_ref[0, 0, TAB_BASE + e]
        lo = tab_ref[0, 0, TAB_LO + e]
        cnt = tab_ref[0, 0, TAB_CNT + e]
        off = 0
        for bit in bits:
            piece = cnt & bit

            @pl.when(piece != 0)
            def _(base=base, lo=lo, off=off, bit=bit):
                fn(pl.multiple_of(base + off, ROW_ALIGN), pl.multiple_of(lo + off, ROW_ALIGN), bit)

            off = off + piece


def _dispatch_body(nblk, tb, tab_ref, tabp_ref, lp_ref, x_ref, xs_in_ref, xs_ref, buf_ref, sem):
    del xs_in_ref
    s = pl.program_id(0)
    slot = s % 2
    lp = lp_ref[0]
    r = lax.broadcasted_iota(i32, (buf_ref.shape[1], 1), 0)
    perm = jnp.where((lp[0:1, :] == r) | (lp[1:2, :] == r), 1.0, 0.0).astype(bf16)
    buf_ref[slot] = _dot(perm, x_ref[...]).astype(bf16)

    def copy(sl):
        return lambda base, lo, rows: pltpu.make_async_copy(
            buf_ref.at[sl, pl.ds(lo, rows)], xs_ref.at[pl.ds(base, rows)], sem.at[sl])

    _for_each_run(tab_ref, tb, lambda *a: copy(slot)(*a).start())

    @pl.when(s > 0)
    def _():
        _for_each_run(tabp_ref, tb, lambda *a: copy(1 - slot)(*a).wait())

    @pl.when(s == nblk - 1)
    def _():
        _for_each_run(tab_ref, tb, lambda *a: copy(slot)(*a).wait())


def _dispatch(xn, tab, lp_rows, xs0, tb):
    nblk = tab.shape[0]
    return pl.pallas_call(
        functools.partial(_dispatch_body, nblk, tb),
        grid=(nblk,),
        in_specs=[pl.BlockSpec((1, 1, TAB_W), lambda s: (s, 0, 0), memory_space=pltpu.SMEM),
                  pl.BlockSpec((1, 1, TAB_W), lambda s: (jnp.maximum(s - 1, 0), 0, 0), memory_space=pltpu.SMEM),
                  pl.BlockSpec((1, 8, tb), lambda s: (s, 0, 0)),
                  pl.BlockSpec((tb, D), lambda s: (s, 0)),
                  pl.BlockSpec(memory_space=pl.ANY)],
        out_specs=pl.BlockSpec(memory_space=pl.ANY),
        out_shape=jax.ShapeDtypeStruct(xs0.shape, bf16),
        scratch_shapes=[pltpu.VMEM((2, _sorted_rows(tb), D), bf16), pltpu.SemaphoreType.DMA((2,))],
        input_output_aliases={4: 0},
        compiler_params=pltpu.CompilerParams(dimension_semantics=("arbitrary",), has_side_effects=True,
                                             vmem_limit_bytes=VMEM_LIMIT),
        name="moe_dispatch",
    )(tab, tab, lp_rows, xn, xs0)


def _experts_body(nf, sub, te_ref, tv_ref, x_ref, wg_ref, wu_ref, wd_ref, y_ref, acc_ref):
    del te_ref
    t = pl.program_id(0)
    f = pl.program_id(1)
    valid = tv_ref[t] == 1

    @pl.when(valid & (f == 0))
    def _():
        acc_ref[...] = jnp.zeros_like(acc_ref)

    @pl.when(valid)
    def _():
        x = x_ref[...]
        acc = acc_ref[...]
        for k in range(wg_ref.shape[1] // sub):
            g = _dot(x, wg_ref[:, k * sub:(k + 1) * sub])
            u = _dot(x, wu_ref[:, k * sub:(k + 1) * sub])
            a = (g * jax.nn.sigmoid(g) * u).astype(bf16)
            acc = acc + _dot(a, wd_ref[k * sub:(k + 1) * sub, :])
        acc_ref[...] = acc

    @pl.when(valid & (f == nf - 1))
    def _():
        y_ref[...] = acc_ref[...].astype(bf16)

    @pl.when(jnp.logical_not(valid) & (f == nf - 1))
    def _():
        y_ref[...] = jnp.zeros_like(y_ref)


def _experts(xs, te, tv, wg, wu, wd, tme, tf, sub):
    rows = xs.shape[0]
    T = rows // tme
    nf = D_FF_EXPERT // tf
    grid_spec = pltpu.PrefetchScalarGridSpec(
        num_scalar_prefetch=2,
        grid=(T, nf),
        in_specs=[pl.BlockSpec((tme, D), lambda t, f, te, tv: (t, 0)),
                  pl.BlockSpec((None, D, tf), lambda t, f, te, tv: (te[t], 0, f)),
                  pl.BlockSpec((None, D, tf), lambda t, f, te, tv: (te[t], 0, f)),
                  pl.BlockSpec((None, tf, D), lambda t, f, te, tv: (te[t], f, 0))],
        out_specs=pl.BlockSpec((tme, D), lambda t, f, te, tv: (t, 0)),
        scratch_shapes=[pltpu.VMEM((tme, D), f32)],
    )
    return pl.pallas_call(
        functools.partial(_experts_body, nf, sub),
        grid_spec=grid_spec,
        out_shape=jax.ShapeDtypeStruct((rows, D), bf16),
        compiler_params=_cparams(("arbitrary", "arbitrary")),
        name="moe_experts",
    )(te, tv, xs, wg, wu, wd)


def _combine_body(nblk, tb, nb, tab_ref, tabn_ref, aux_ref, h_ref, ys_ref, o_ref, buf_ref, obuf_ref, sem, osem):
    s = pl.program_id(0)
    slot = s % 2
    sub = tb // BLK

    def copy(sl):
        return lambda base, lo, rows: pltpu.make_async_copy(
            ys_ref.at[pl.ds(base, rows)], buf_ref.at[sl, pl.ds(lo, rows)], sem.at[sl])

    @pl.when(s == 0)
    def _():
        _for_each_run(tab_ref, tb, lambda *a: copy(slot)(*a).start())

    @pl.when(s + 1 < nblk)
    def _():
        _for_each_run(tabn_ref, tb, lambda *a: copy(1 - slot)(*a).start())

    _for_each_run(tab_ref, tb, lambda *a: copy(slot)(*a).wait())

    total = tab_ref[0, 0, TAB_LO + N_EXPERTS - 1] + tab_ref[0, 0, TAB_CNT + N_EXPERTS - 1]
    rows = lax.broadcasted_iota(i32, (buf_ref.shape[1], 1), 0)
    ys = jnp.where(rows < total, buf_ref[slot], jnp.zeros((), bf16))
    aux = aux_ref[...]
    lane = lax.broadcasted_iota(i32, (1, buf_ref.shape[1]), 1)
    w = (jnp.where(lane == aux[:, 0:1].astype(i32), aux[:, 2:3], 0.0)
         + jnp.where(lane == aux[:, 1:2].astype(i32), aux[:, 3:4], 0.0))
    hi, lo = _split_bf16(w)
    obuf_ref[slot] = h_ref[...] + _dot(hi, ys) + _dot(lo, ys)

    def out_copies(step, sl, fn):
        for q in range(sub):
            pb = step * sub + q

            @pl.when(pb % nb != 0)
            def _(pb=pb, q=q):
                orow = pl.multiple_of(((pb // nb) * (nb - 1) + pb % nb - 1) * BLK, BLK)
                fn(pltpu.make_async_copy(obuf_ref.at[sl, pl.ds(q * BLK, BLK)], o_ref.at[pl.ds(orow, BLK)],
                                         osem.at[sl]))

    out_copies(s, slot, lambda c: c.start())

    @pl.when(s > 0)
    def _():
        out_copies(s - 1, 1 - slot, lambda c: c.wait())

    @pl.when(s == nblk - 1)
    def _():
        out_copies(s, slot, lambda c: c.wait())


def _combine(h, aux, tab, ys, B, nb, tb):
    nblk = tab.shape[0]
    return pl.pallas_call(
        functools.partial(_combine_body, nblk, tb, nb),
        grid=(nblk,),
        in_specs=[pl.BlockSpec((1, 1, TAB_W), lambda s: (s, 0, 0), memory_space=pltpu.SMEM),
                  pl.BlockSpec((1, 1, TAB_W), lambda s: (jnp.minimum(s + 1, nblk - 1), 0, 0),
                               memory_space=pltpu.SMEM),
                  pl.BlockSpec((tb, LANES), lambda s: (s, 0)),
                  pl.BlockSpec((tb, D), lambda s: (s, 0)),
                  pl.BlockSpec(memory_space=pl.ANY)],
        out_specs=pl.BlockSpec(memory_space=pl.ANY),
        out_shape=jax.ShapeDtypeStruct((B * (nb - 1) * BLK, D), f32),
        scratch_shapes=[pltpu.VMEM((2, _sorted_rows(tb), D), bf16), pltpu.VMEM((2, tb, D), f32),
                        pltpu.SemaphoreType.DMA((2,)), pltpu.SemaphoreType.DMA((2,))],
        compiler_params=_cparams(("arbitrary",)),
        name="moe_combine",
    )(tab, tab, aux, h, ys)


def _routing_tables(info, cnt_rows, B, nb, tb, tme):
    n = info.shape[0]
    nblk = n // tb
    ar = jnp.arange(N_EXPERTS, dtype=i32)
    e1 = info[:, 0].astype(i32)
    e2 = info[:, 1].astype(i32)
    blk = jnp.arange(n // BLK, dtype=i32)[:, None]
    rowin = jnp.arange(BLK, dtype=i32)[None, :]
    real = ((blk % nb != 0) | (rowin >= PAD)).reshape(n)
    oh1 = (e1[:, None] == ar) & real[:, None]
    oh2 = (e2[:, None] == ar) & real[:, None]
    blk_cnt = (oh1 | oh2).astype(i32).reshape(nblk, tb, N_EXPERTS).sum(axis=1)
    blk_rank0 = jnp.cumsum(blk_cnt, axis=0) - blk_cnt
    run = ((blk_cnt + ROW_ALIGN - 1) // ROW_ALIGN) * ROW_ALIGN
    run0 = jnp.cumsum(run, axis=0) - run
    seg = ((jnp.sum(run, axis=0) + tme - 1) // tme) * tme
    ends = jnp.cumsum(seg)
    offs = ends - seg
    base = offs[None, :] + run0
    lo = jnp.cumsum(run, axis=1) - run
    tab = jnp.concatenate([base, lo, run, jnp.zeros_like(base)], axis=1).reshape(nblk, 1, TAB_W)

    shift = jnp.repeat(lo - blk_rank0, tb, axis=0)
    lp1 = jnp.where(real, jnp.sum(jnp.where(oh1, shift, 0), axis=1) + info[:, 4].astype(i32), -1)
    lp2 = jnp.where(real, jnp.sum(jnp.where(oh2, shift, 0), axis=1) + info[:, 5].astype(i32), -1)
    lp_rows = jnp.concatenate([lp1.reshape(nblk, 1, tb), lp2.reshape(nblk, 1, tb),
                               jnp.full((nblk, 6, tb), -1, i32)], axis=1)
    aux = jnp.concatenate([lp1.astype(f32)[:, None], lp2.astype(f32)[:, None], info[:, 2:4],
                           jnp.zeros((n, LANES - 4), f32)], axis=1)
    del cnt_rows
    return tab, lp_rows, aux, ends


def _strip_body(h_ref, o_ref):
    o_ref[...] = h_ref[...]


def _strip_meta(h, B, nb):
    nt = nb - 1
    return pl.pallas_call(
        _strip_body,
        grid=(B, nt),
        in_specs=[pl.BlockSpec((BLK, D), lambda b, i: (b * nb + i + 1, 0))],
        out_specs=pl.BlockSpec((BLK, D), lambda b, i: (b * nt + i, 0)),
        out_shape=jax.ShapeDtypeStruct((B * nt * BLK, D), f32),
        compiler_params=_cparams(("parallel", "arbitrary")),
        name="strip_meta",
    )(h)


def _bucket_lookup(table, bucket):
    tshape = (table.shape[1],) + (1,) * bucket.ndim
    out = jnp.zeros((table.shape[1],) + bucket.shape, f32)
    for k in range(N_BUCKETS):
        out = out + jnp.where((bucket == k)[None], table[k].astype(f32).reshape(tshape), 0.0)
    return out


def _bias_tables(rel, nb):
    table_a = rel[:, :A_HEADS] * LOG2E
    table_b = rel[:, A_HEADS:] * LOG2E
    kk = jnp.arange(2 * BLK)[:, None]
    qq = jnp.arange(BLK)[None, :]
    n = BLK + qq - kk
    in_win = (n >= 0) & (n < BLK)
    bias = _bucket_lookup(table_a, _t5_bucket(jnp.maximum(n, 0)))
    variants = []
    for blk in range(3):
        key_ok = (blk - 1) * BLK + kk >= PAD
        variants.append(jnp.where((in_win & key_ok)[None], bias, NEG))
    bias_at = jnp.stack(variants, axis=0)

    d = jnp.arange(nb + 1)[:, None, None]
    kb = jnp.arange(BLK)[None, :, None]
    qb = jnp.arange(BLK)[None, None, :]
    nn = jnp.where(d < nb, d * BLK + qb - kb, -1)
    bias_bt = _bucket_lookup(table_b, _t5_bucket(jnp.maximum(nn, 0)))
    bias_bt = jnp.where((nn >= 0)[None], bias_bt, NEG)
    return bias_at, bias_bt


def _block_diag_ones(gs):
    idx = np.arange(LANES) // gs
    return jnp.asarray((idx[:, None] == idx[None, :]).astype(np.float32), dtype=bf16)


def _col_scale(gain_a, gain_b):
    one = jnp.ones((256,), f32)
    cs = jnp.concatenate([one, jnp.tile(gain_b[1], 8), jnp.tile(gain_a[1], 2)]).reshape(1, D_ROW)
    cst = jnp.concatenate([jnp.tile(gain_a[0], 8) * (64 ** -0.5 * LOG2E),
                           jnp.tile(gain_b[0], 8) * (32 ** -0.5 * LOG2E)]).reshape(R_VB, 1)
    return cs.astype(f32), cst.astype(f32)


def _split_in_weight(w):
    main = jnp.concatenate([w[:, 1536:1792], w[:, 1024:1280], w[:, 512:640]], axis=1)
    tr = jnp.concatenate([w[:, 0:512], w[:, 768:1024], w[:, 1280:1536], w[:, 640:768]], axis=1).T
    return main.astype(bf16), tr.astype(bf16)


def _pool_weight(w_pool):
    out = jnp.zeros((256, 256), f32)
    for gi in range(4):
        out = out.at[gi * 64:(gi + 1) * 64, gi * 64:(gi + 1) * 64].set(w_pool[gi])
    return out.astype(bf16)


def _row_tile(n):
    for tm in (512, 256, 128):
        if n % tm == 0:
            return tm
    raise ValueError(n)


def kernel(x, meta_tokens, rel_bias_table, g_mix, w_in, qk_gain_a, sinks, qk_gain_b, diff_lambda, g_diff_out,
           w_pool, pool_scale, w_out, g_ffn, w_ffn_gate, w_ffn_up, w_ffn_down, w_router, w_exp_gate, w_exp_up,
           w_exp_down):
    B, seq, _ = x.shape
    assert seq % BLK == 0
    depth = g_mix.shape[0]
    nb = seq // BLK + 1
    L = nb * BLK
    n = B * L
    tm = _row_tile(n)
    tm_in = 2 * tm if n % (2 * tm) == 0 else tm
    tme = 1024
    tf_e = 1792
    tf_d = 256
    kb_d = 4

    bias_at, bias_bt = _bias_tables(rel_bias_table, nb)
    g64 = _block_diag_ones(64)
    g32 = _block_diag_ones(32)

    h = _embed(x, meta_tokens, B, nb)
    out = None
    for l in range(depth):
        lam_init = 0.8 - 0.6 * math.exp(-0.3 * l)
        cs, cst = _col_scale(qk_gain_a[l], qk_gain_b[l])
        w_main, w_tr = _split_in_weight(w_in[l])
        z, ft = _inproj(h, g_mix[l].reshape(1, D), w_main, w_tr, cs, cst, g64, g32, tm_in)
        oa, oc = _swa_pool(z, ft, bias_at, sinks[l], _pool_weight(w_pool[l]), pool_scale[l].reshape(1, 256), B, nb)
        ob = _diff_attn(ft, z.reshape(B, L, D_ROW), bias_bt, diff_lambda[l],
                        jnp.tile(g_diff_out[l], 2).reshape(BLK, 1), lam_init, B, nb, kb_d)
        wo = w_out[l].astype(bf16)
        gf = g_ffn[l].reshape(1, D)
        j = l // 2
        if l % 2 == 0:
            h = _outproj_ffn(oa, ob, oc, wo, h, gf, w_ffn_gate[j].astype(bf16), w_ffn_up[j].astype(bf16),
                             w_ffn_down[j].astype(bf16), tm, tf_d)
            out = None
        else:
            assert l == depth - 1, "the expert layer must be the last layer"
            wr = jnp.pad(w_router[j], ((0, 0), (0, LANES - N_EXPERTS)))
            wrh = wr.astype(bf16)
            wrl = (wr - wrh.astype(f32)).astype(bf16)
            tri = jnp.asarray(np.tril(np.ones((tm // 2, tm // 2), np.float32), -1), dtype=bf16)
            h, xn, info, cnt = _outproj_router(oa, ob, oc, wo, h, gf, tm, (wrh, wrl, tri), nb)

            tab, lp_rows, aux, ends = _routing_tables(info, cnt, B, nb, tm, tme)
            rows_max = B * (seq + N_META) * 2 + (n // tm) * N_EXPERTS * (ROW_ALIGN - 1) + N_EXPERTS * (tme - 1)
            T = -(-rows_max // tme)
            starts = jnp.arange(T, dtype=i32) * tme
            te = jnp.minimum(jnp.sum((ends[None, :] <= starts[:, None]).astype(i32), axis=1), N_EXPERTS - 1)
            tv = (starts < ends[-1]).astype(i32)

            xs = _dispatch(xn, tab, lp_rows, jnp.zeros((T * tme, D), bf16), tm)
            ys = _experts(xs, te, tv, w_exp_gate[j].astype(bf16), w_exp_up[j].astype(bf16),
                          w_exp_down[j].astype(bf16), tme, tf_e, 256)
            out = _combine(h, aux, tab, ys, B, nb, tm)
    if out is None:
        out = _strip_meta(h, B, nb)
    return out.reshape(B, seq, D)
```

```python
import functools
import math

import jax
import jax.numpy as jnp
import numpy as np
from jax import lax
from jax.experimental import pallas as pl
from jax.experimental.pallas import tpu as pltpu

f32 = jnp.float32
bf16 = jnp.bfloat16
i32 = jnp.int32

D = 1024
BLK = 128
N_META = 16
PAD = BLK - N_META
A_HEADS = 8
B_HEADS = 4
N_BUCKETS = 32
MAX_DISTANCE = 128
N_EXPERTS = 8
D_FF_EXPERT = 3584
EPS = 1e-6
NEG = -1e30
LANES = 128
D_ROW = 640
D_FM = 1152
R_QB, R_VB, R_VA = 512, 768, 1024
LOG2E = 1.4426950408889634

VMEM_LIMIT = 56 * 1024 * 1024


def _cparams(sem):
    return pltpu.CompilerParams(dimension_semantics=sem, vmem_limit_bytes=VMEM_LIMIT)


def _t5_bucket(n):
    max_exact = N_BUCKETS // 2
    large = max_exact
    for j in range(1, N_BUCKETS - max_exact):
        thr = math.ceil(max_exact * (MAX_DISTANCE / max_exact) ** (j / (N_BUCKETS - max_exact)))
        large = large + (n >= thr).astype(i32)
    return jnp.where(n < max_exact, n, large)


def _dot(a, b):
    return jnp.dot(a, b, preferred_element_type=f32)


def _dot_nt(a, b):
    return lax.dot_general(a, b, (((1,), (1,)), ((), ())), preferred_element_type=f32)


def _embed_body(x_ref, m_ref, o_ref):
    o_ref[0:PAD, :] = jnp.zeros((PAD, D), f32)
    o_ref[PAD:BLK, :] = m_ref[...]
    o_ref[BLK:, :] = x_ref[...]


def _embed(x, meta, B, nb):
    seq = x.shape[1]
    return pl.pallas_call(
        _embed_body,
        grid=(B,),
        in_specs=[pl.BlockSpec((None, seq, D), lambda b: (b, 0, 0)),
                  pl.BlockSpec((N_META, D), lambda b: (0, 0))],
        out_specs=pl.BlockSpec((nb * BLK, D), lambda b: (b, 0)),
        out_shape=jax.ShapeDtypeStruct((B * nb * BLK, D), f32),
        compiler_params=_cparams(("parallel",)),
        name="embed",
    )(x, meta)


def _split_bf16(x):
    hi = x.astype(bf16)
    return hi, (x - hi.astype(f32)).astype(bf16)


def _inproj_body(tm, h_ref, g_ref, w_ref, wt_ref, cs_ref, cst_ref, g64_ref, g32_ref, z_ref, ft_ref):
    sub = tm
    for r0 in range(0, tm, sub):
        rows = slice(r0, r0 + sub)
        x = h_ref[rows, :]
        ms = jnp.mean(x * x, axis=-1, keepdims=True)
        xn = (x * lax.rsqrt(ms + EPS) * g_ref[...]).astype(bf16)

        for c in range(D_ROW // LANES):
            zc = _dot(xn, w_ref[:, c * LANES:(c + 1) * LANES])
            if c >= 2:
                gm, inv = (g32_ref, 1.0 / 32) if c < 4 else (g64_ref, 1.0 / 64)
                ss = _dot((zc * zc).astype(bf16), gm[...])
                zc = zc * lax.rsqrt(ss * inv + EPS)
            z_ref[rows, c * LANES:(c + 1) * LANES] = (zc * cs_ref[:, c * LANES:(c + 1) * LANES]).astype(bf16)

        zt_all = _dot_nt(wt_ref[...], xn)
        for c in range(D_FM // LANES):
            zt = zt_all[c * LANES:(c + 1) * LANES]
            if c < R_VB // LANES:
                gm, inv = (g64_ref, 1.0 / 64) if c < R_QB // LANES else (g32_ref, 1.0 / 32)
                ss = _dot(gm[...], (zt * zt).astype(bf16))
                zt = zt * lax.rsqrt(ss * inv + EPS) * cst_ref[c * LANES:(c + 1) * LANES, :]
            for k in range(sub // BLK):
                ft_ref[r0 // BLK + k, c * LANES:(c + 1) * LANES, :] = zt[:, k * BLK:(k + 1) * BLK].astype(bf16)


def _inproj(h, g, w, wt, cs, cst, g64, g32, tm):
    n = h.shape[0]
    const = lambda i: (0, 0)
    kb = tm // BLK
    return pl.pallas_call(
        functools.partial(_inproj_body, tm),
        grid=(n // tm,),
        in_specs=[pl.BlockSpec((tm, D), lambda i: (i, 0)),
                  pl.BlockSpec((1, D), const),
                  pl.BlockSpec((D, D_ROW), const),
                  pl.BlockSpec((D_FM, D), const),
                  pl.BlockSpec((1, D_ROW), const),
                  pl.BlockSpec((R_VB, 1), const),
                  pl.BlockSpec((LANES, LANES), const),
                  pl.BlockSpec((LANES, LANES), const)],
        out_specs=[pl.BlockSpec((tm, D_ROW), lambda i: (i, 0)),
                   pl.BlockSpec((kb, D_FM, BLK), lambda i: (i, 0, 0))],
        out_shape=[jax.ShapeDtypeStruct((n, D_ROW), bf16),
                   jax.ShapeDtypeStruct((n // BLK, D_FM, BLK), bf16)],
        compiler_params=_cparams(("parallel",)),
        name="norm_inproj",
    )(h, g, w, wt, cs, cst, g64, g32)


def _swa_pool_body(lanes, *refs):
    ins = [refs[7 * s:7 * s + 7] for s in range(lanes)]
    bias_ref, sink_ref, wp_ref, ps_ref, oa_ref, oc_ref = refs[7 * lanes:7 * lanes + 6]
    i = pl.program_id(1)
    zeros = jnp.zeros((64, BLK), bf16)
    row = lax.broadcasted_iota(i32, (BLK, 1), 0)
    t = i * BLK + row - PAD
    lane2 = lax.broadcasted_iota(i32, (1, 256), 1)
    grp2 = lane2 // 64
    win = jnp.where(grp2 == 0, 2, jnp.where(grp2 == 1, 4, jnp.where(grp2 == 2, 8, 16)))
    cnt = jnp.maximum(jnp.minimum(t + 1, win), 1).astype(f32)
    for s in range(lanes):
        qt_ref, kp_ref, kc_ref, vtp_ref, vtc_ref, up_ref, uc_ref = ins[s]
        kcat = jnp.concatenate([kp_ref[...], kc_ref[...]], axis=0)
        vt = jnp.concatenate([vtp_ref[...], vtc_ref[...]], axis=1)
        for grp in range(2):
            pts, inv_ls = [], []
            for hh in range(4):
                h = 4 * grp + hh
                qrows = qt_ref[64 * h:64 * h + 64, :]
                qmt = jnp.concatenate([qrows, zeros] if grp == 0 else [zeros, qrows], axis=0)
                st = _dot(kcat, qmt) + bias_ref[h]
                sink = sink_ref[h] * LOG2E
                m = jnp.maximum(jnp.max(st, axis=0, keepdims=True), sink)
                p = jnp.exp2(st - m)
                inv_ls.append(1.0 / (jnp.sum(p, axis=0, keepdims=True) + jnp.exp2(sink - m)))
                pts.append(p.astype(bf16))
            ot = _dot(vt[64 * grp:64 * grp + 64, :], jnp.concatenate(pts, axis=1))
            for pair in range(2):
                o2 = jnp.concatenate([ot[:, (2 * pair + k) * BLK:(2 * pair + k + 1) * BLK] * inv_ls[2 * pair + k]
                                      for k in range(2)], axis=0)
                pb = 2 * grp + pair
                oa_ref[s, :, pb * LANES:(pb + 1) * LANES] = o2.T.astype(bf16)

        u_cur = jnp.where(t >= 0, uc_ref[...].astype(f32), 0.0)
        u_prev = jnp.where(i > 0, up_ref[BLK - 16:, :].astype(f32), 0.0)
        ext = jnp.concatenate([u_prev, u_cur], axis=0)
        s2 = ext + pltpu.roll(ext, 1, 0)
        s4 = s2 + pltpu.roll(s2, 2, 0)
        s8 = s4 + pltpu.roll(s4, 4, 0)
        s16 = s8 + pltpu.roll(s8, 8, 0)
        sums = jnp.where(grp2 == 0, s2, jnp.where(grp2 == 1, s4, jnp.where(grp2 == 2, s8, s16)))[16:, :]
        dlt = (sums / cnt - u_cur).astype(bf16)
        oc_ref[s] = (_dot(dlt, wp_ref[...]) * ps_ref[...]).astype(bf16)


def _swa_pool(z, ft, bias_at, sinks, wpool_bd, pool_scale, B, nb):
    L = nb * BLK
    lanes = next(k for k in (4, 2, 1) if B % k == 0)
    in_specs, operands = [], []
    for s in range(lanes):
        cur = lambda c, s=s: (lambda g, i: ((lanes * g + s) * nb + i, c))
        prev = lambda c, s=s: (lambda g, i: ((lanes * g + s) * nb + jnp.maximum(i - 1, 0), c))
        cur3 = lambda r, s=s: (lambda g, i: ((lanes * g + s) * nb + i, r, 0))
        prev3 = lambda r, s=s: (lambda g, i: ((lanes * g + s) * nb + jnp.maximum(i - 1, 0), r, 0))
        in_specs += [pl.BlockSpec((None, 512, BLK), cur3(0)),
                     pl.BlockSpec((BLK, LANES), prev(4)),
                     pl.BlockSpec((BLK, LANES), cur(4)),
                     pl.BlockSpec((None, BLK, BLK), prev3(R_VA // BLK)),
                     pl.BlockSpec((None, BLK, BLK), cur3(R_VA // BLK)),
                     pl.BlockSpec((BLK, 256), prev(0)),
                     pl.BlockSpec((BLK, 256), cur(0))]
        operands += [ft, z, z, ft, ft, z, z]
    in_specs += [pl.BlockSpec((None, A_HEADS, 2 * BLK, BLK), lambda g, i: (jnp.minimum(i, 2), 0, 0, 0)),
                 pl.BlockSpec(memory_space=pltpu.SMEM),
                 pl.BlockSpec((256, 256), lambda g, i: (0, 0)),
                 pl.BlockSpec((1, 256), lambda g, i: (0, 0))]
    operands += [bias_at, sinks, wpool_bd, pool_scale]
    oa, oc = pl.pallas_call(
        functools.partial(_swa_pool_body, lanes),
        grid=(B // lanes, nb),
        in_specs=in_specs,
        out_specs=[pl.BlockSpec((None, lanes, BLK, 512), lambda g, i: (g, 0, i, 0)),
                   pl.BlockSpec((None, lanes, BLK, 256), lambda g, i: (g, 0, i, 0))],
        out_shape=[jax.ShapeDtypeStruct((B // lanes, lanes, L, 512), bf16),
                   jax.ShapeDtypeStruct((B // lanes, lanes, L, 256), bf16)],
        compiler_params=_cparams(("parallel", "arbitrary")),
        name="swa_pool",
    )(*operands)
    return oa.reshape(B * L, 512), oc.reshape(B * L, 256)


def _diff_attn_body(lam_init, nb, kb_step, lanes, *refs):
    ins = [refs[3 * s:3 * s + 3] for s in range(lanes)]
    bias_ref, dl_ref, go_ref = refs[3 * lanes:3 * lanes + 3]
    o_ref = refs[3 * lanes + 3]
    scr = [refs[3 * lanes + 4 + 7 * s:3 * lanes + 4 + 7 * s + 7] for s in range(lanes)]
    i = pl.program_id(1)
    row = lax.broadcasted_iota(i32, (BLK, 1), 0)
    zero = jnp.zeros((), bf16)
    rows_k = lax.broadcasted_iota(i32, (kb_step * BLK, 1), 0)
    padmask = jnp.where(rows_k < PAD, NEG, 0.0).astype(f32)

    qcat = []
    for s in range(lanes):
        qt_ref = ins[s][0]
        per_lane = []
        for p in range(2):
            qp = qt_ref[p * BLK:(p + 1) * BLK, :]
            per_lane.append([jnp.concatenate(
                [jnp.where((row >= 64 * e + 32 * c) & (row < 64 * e + 32 * c + 32), qp, zero) for c in range(2)],
                axis=1) for e in range(2)])
        qcat.append(per_lane)
        acc_ref, m_ref, l_ref = scr[s][0:3]
        m_ref[...] = jnp.full(m_ref.shape, 3 * NEG, f32)
        l_ref[...] = jnp.zeros(l_ref.shape, f32)
        acc_ref[...] = jnp.zeros(acc_ref.shape, f32)

    def blocks(st):
        j0 = st * kb_step
        return [jnp.minimum(j0 + kb, nb - 1) for kb in range(kb_step)], j0

    def scores(s, st, first, par):
        k_ref = ins[s][1]
        s_ref, x_ref = scr[s][3 + par], scr[s][5 + par]
        blks, j0 = blocks(st)
        dix = [jnp.where(i - (j0 + kb) >= 0, i - (j0 + kb), nb) for kb in range(kb_step)]
        kparts = [k_ref[pl.ds(pl.multiple_of(b * BLK, BLK), BLK), :] for b in blks]
        for p in range(2):
            kcat = jnp.concatenate([kp[:, p * BLK:(p + 1) * BLK] for kp in kparts], axis=0)
            for e in range(2):
                h = 2 * p + e
                st2 = _dot(kcat, qcat[s][p][e])
                bias = jnp.concatenate([bias_ref[h, dd] for dd in dix], axis=0)
                if first:
                    bias = bias + padmask
                for c in range(2):
                    tile = st2[:, c * BLK:(c + 1) * BLK] + bias
                    s_ref[2 * h + c] = tile
                    x_ref[2 * h + c:2 * h + c + 1, :] = jnp.max(tile, axis=0, keepdims=True)

    def consume(s, st, par):
        vt_ref = ins[s][2]
        acc_ref, m_ref, l_ref = scr[s][0:3]
        s_ref, x_ref = scr[s][3 + par], scr[s][5 + par]
        blks, _ = blocks(st)
        vparts = [vt_ref[b] for b in blks]
        for p in range(2):
            vtc = jnp.concatenate([vp[p * BLK:(p + 1) * BLK, :] for vp in vparts], axis=1)
            for c in range(2):
                for e in range(2):
                    idx = 2 * (2 * p + e) + c
                    m_old = m_ref[idx:idx + 1, :]
                    m_new = jnp.maximum(m_old, x_ref[idx:idx + 1, :])
                    alpha = jnp.exp2(m_old - m_new)
                    pt = jnp.exp2(s_ref[idx] - m_new)
                    l_ref[idx:idx + 1, :] = alpha * l_ref[idx:idx + 1, :] + jnp.sum(pt, axis=0, keepdims=True)
                    m_ref[idx:idx + 1, :] = m_new
                    rows = slice(64 * e, 64 * e + 64)
                    upd = _dot(vtc[rows, :], pt.astype(bf16))
                    acc_ref[2 * p + c, rows, :] = acc_ref[2 * p + c, rows, :] * alpha + upd

    n_steps = (i + kb_step) // kb_step
    for s in range(lanes):
        scores(s, 0, True, 0)

    def body(st, carry):
        for par in range(2):
            @pl.when(st % 2 == par)
            def _(par=par):
                for s in range(lanes):
                    scores(s, st + 1, False, 1 - par)
                for s in range(lanes):
                    consume(s, st, par)

        return carry

    lax.fori_loop(0, n_steps - 1, body, 0)
    last = n_steps - 1
    for par in range(2):
        @pl.when(last % 2 == par)
        def _(par=par):
            for s in range(lanes):
                consume(s, last, par)

    dl = dl_ref[...]
    lam = (jnp.exp(jnp.sum(dl[0:1] * dl[1:2], axis=-1, keepdims=True))
           - jnp.exp(jnp.sum(dl[2:3] * dl[3:4], axis=-1, keepdims=True)) + lam_init)
    for s in range(lanes):
        acc_ref, _, l_ref = scr[s][0:3]
        for p in range(2):
            parts = []
            for e in range(2):
                h = 2 * p + e
                rows = slice(64 * e, 64 * e + 64)
                o = (acc_ref[2 * p, rows, :] / l_ref[2 * h:2 * h + 1, :]
                     - lam * (acc_ref[2 * p + 1, rows, :] / l_ref[2 * h + 1:2 * h + 2, :]))
                ssq = jnp.sum(o * o, axis=0, keepdims=True)
                parts.append(o * lax.rsqrt(ssq * (1.0 / 64) + EPS))
            o = jnp.concatenate(parts, axis=0) * go_ref[...] * (1.0 - lam_init)
            o_ref[s, :, p * BLK:(p + 1) * BLK] = o.T.astype(bf16)


def _diff_attn(ft, z3, bias_bt, dl, go, lam_init, B, nb, kb_step):
    L = nb * BLK
    lanes = 2 if B % 2 == 0 else 1
    in_specs, scratch = [], []
    for s in range(lanes):
        bat = lambda g, s=s: lanes * g + s
        in_specs += [pl.BlockSpec((None, 256, BLK), lambda g, i, bat=bat: (bat(g) * nb + i, R_QB // 256, 0)),
                     pl.BlockSpec((None, L, 256), lambda g, i, bat=bat: (bat(g), 0, 1)),
                     pl.BlockSpec((nb, 256, BLK), lambda g, i, bat=bat: (bat(g), R_VB // 256, 0))]
        scratch += [pltpu.VMEM((4, BLK, BLK), f32), pltpu.VMEM((8, BLK), f32), pltpu.VMEM((8, BLK), f32),
                    pltpu.VMEM((8, kb_step * BLK, BLK), f32), pltpu.VMEM((8, kb_step * BLK, BLK), f32),
                    pltpu.VMEM((8, BLK), f32), pltpu.VMEM((8, BLK), f32)]
    in_specs += [pl.BlockSpec((B_HEADS, nb + 1, BLK, BLK), lambda g, i: (0, 0, 0, 0)),
                 pl.BlockSpec((4, 32), lambda g, i: (0, 0)),
                 pl.BlockSpec((BLK, 1), lambda g, i: (0, 0))]
    operands = [ft, z3, ft] * lanes + [bias_bt, dl, go]
    return pl.pallas_call(
        functools.partial(_diff_attn_body, lam_init, nb, kb_step, lanes),
        grid=(B // lanes, nb),
        in_specs=in_specs,
        out_specs=pl.BlockSpec((None, lanes, BLK, 256), lambda g, i: (g, 0, i, 0)),
        out_shape=jax.ShapeDtypeStruct((B // lanes, lanes, L, 256), bf16),
        scratch_shapes=scratch,
        compiler_params=_cparams(("parallel", "arbitrary")),
        name="diff_attn",
    )(*operands).reshape(B * L, 256)


def _outproj_common(oa_ref, ob_ref, oc_ref, w_ref, h_ref, g_ref, rows=slice(None)):
    y = (_dot(oa_ref[rows, :], w_ref[0:512, :]) + _dot(ob_ref[rows, :], w_ref[512:768, :])
         + _dot(oc_ref[rows, :], w_ref[768:1024, :]))
    hn = h_ref[rows, :] + y
    ms = jnp.mean(hn * hn, axis=-1, keepdims=True)
    return hn, hn * lax.rsqrt(ms + EPS) * g_ref[...]


def _outproj_ffn_body(tf, oa_ref, ob_ref, oc_ref, w_ref, h_ref, g_ref, wg_ref, wu_ref, wd_ref, o_ref):
    hn, xn = _outproj_common(oa_ref, ob_ref, oc_ref, w_ref, h_ref, g_ref)
    x = xn.astype(bf16)
    acc = hn
    for c in range(wg_ref.shape[1] // tf):
        g = _dot(x, wg_ref[:, c * tf:(c + 1) * tf])
        u = _dot(x, wu_ref[:, c * tf:(c + 1) * tf])
        a = (g * jax.nn.sigmoid(g) * u).astype(bf16)
        acc = acc + _dot(a, wd_ref[c * tf:(c + 1) * tf, :])
    o_ref[...] = acc


def _outproj_router_body(nb, tm, oa_ref, ob_ref, oc_ref, w_ref, h_ref, g_ref, wrh_ref, wrl_ref, tri_ref,
                         ho_ref, xn_ref, info_ref, cnt_ref):
    step = pl.program_id(0)
    th = tri_ref.shape[0]
    lane = lax.broadcasted_iota(i32, (1, LANES), 1)
    rowin = lax.broadcasted_iota(i32, (BLK, 1), 0)
    ninf = -jnp.inf

    @pl.when(step == 0)
    def _():
        cnt_ref[...] = jnp.zeros_like(cnt_ref)

    base = cnt_ref[...]
    for r0 in range(0, tm, th):
        rows = slice(r0, r0 + th)
        hn, xn = _outproj_common(oa_ref, ob_ref, oc_ref, w_ref, h_ref, g_ref, rows)
        ho_ref[rows, :] = hn
        xn_ref[rows, :] = xn.astype(bf16)

        hi, lo = _split_bf16(xn)
        logits = _dot(hi, wrh_ref[...]) + _dot(lo, wrh_ref[...]) + _dot(hi, wrl_ref[...])
        lg = jnp.where(lane < N_EXPERTS, logits, ninf)
        m1 = jnp.max(lg, axis=-1, keepdims=True)
        i1 = jnp.min(jnp.where(lg == m1, lane, LANES), axis=-1, keepdims=True)
        lg2 = jnp.where(lane == i1, ninf, lg)
        m2 = jnp.max(lg2, axis=-1, keepdims=True)
        i2 = jnp.min(jnp.where(lg2 == m2, lane, LANES), axis=-1, keepdims=True)
        e21 = jnp.exp(m2 - m1)
        g1 = 1.0 / (1.0 + e21)
        g2 = e21 * g1

        valids = []
        for k in range(th // BLK):
            blk = step * (tm // BLK) + r0 // BLK + k
            valids.append(jnp.logical_or(blk % nb != 0, rowin >= PAD))
        valid = jnp.concatenate(valids, axis=0)
        oh = jnp.where(valid & ((lane == i1) | (lane == i2)), 1.0, 0.0)
        rank = _dot(tri_ref[...], oh.astype(bf16)) + base
        r1 = jnp.sum(jnp.where(lane == i1, rank, 0.0), axis=-1, keepdims=True)
        r2 = jnp.sum(jnp.where(lane == i2, rank, 0.0), axis=-1, keepdims=True)
        base = base + jnp.sum(oh, axis=0, keepdims=True)
        info_ref[rows, :] = jnp.where(lane == 0, i1.astype(f32),
                            jnp.where(lane == 1, i2.astype(f32),
                            jnp.where(lane == 2, g1,
                            jnp.where(lane == 3, g2,
                            jnp.where(lane == 4, r1,
                            jnp.where(lane == 5, r2, 0.0))))))
    cnt_ref[...] = base


def _outproj_ffn(oa, ob, oc, w, h, g, wg, wu, wd, tm, tf):
    n = h.shape[0]
    F = wg.shape[1]
    row = lambda c: pl.BlockSpec((tm, c), lambda i: (i, 0))
    const = lambda r, c: pl.BlockSpec((r, c), lambda i: (0, 0), pipeline_mode=pl.Buffered(1))
    return pl.pallas_call(
        functools.partial(_outproj_ffn_body, tf),
        grid=(n // tm,),
        in_specs=[row(512), row(256), row(256), const(D, D), row(D), const(1, D),
                  const(D, F), const(D, F), const(F, D)],
        out_specs=row(D),
        out_shape=jax.ShapeDtypeStruct((n, D), f32),
        compiler_params=_cparams(("parallel",)),
        name="outproj_ffn",
    )(oa, ob, oc, w, h, g, wg, wu, wd)


def _outproj_router(oa, ob, oc, w, h, g, tm, router, nb):
    n = h.shape[0]
    row = lambda c: pl.BlockSpec((tm, c), lambda i: (i, 0))
    const = lambda r, c: pl.BlockSpec((r, c), lambda i: (0, 0))
    in_specs = [row(512), row(256), row(256), const(D, D), row(D), const(1, D)]
    wrh, wrl, tri = router
    return pl.pallas_call(
        functools.partial(_outproj_router_body, nb, tm),
        grid=(n // tm,),
        in_specs=in_specs + [const(D, LANES), const(D, LANES), const(*tri.shape)],
        out_specs=[row(D), row(D), row(LANES), const(1, LANES)],
        out_shape=[jax.ShapeDtypeStruct((n, D), f32), jax.ShapeDtypeStruct((n, D), bf16),
                   jax.ShapeDtypeStruct((n, LANES), f32), jax.ShapeDtypeStruct((1, LANES), f32)],
        compiler_params=_cparams(("arbitrary",)),
        name="outproj_router",
    )(oa, ob, oc, w, h, g, wrh, wrl, tri)


ROW_ALIGN = 16
TAB_BASE, TAB_LO, TAB_CNT, TAB_W = 0, N_EXPERTS, 2 * N_EXPERTS, 4 * N_EXPERTS


def _sorted_rows(tb):
    return -(-(2 * tb + N_EXPERTS * (ROW_ALIGN - 1)) // LANES) * LANES


def _for_each_run(tab_ref, tb, fn):
    bits = []
    b = pl.next_power_of_2(2 * tb)
    while b >= ROW_ALIGN:
        bits.append(b)
        b //= 2
    for e in range(N_EXPERTS):
        base = tab_ref[0, 0, TAB_BASE + e]
        lo = tab_ref[0, 0, TAB_LO + e]
        cnt = tab_ref[0, 0, TAB_CNT + e]
        off = 0
        for bit in bits:
            piece = cnt & bit

            @pl.when(piece != 0)
            def _(base=base, lo=lo, off=off, bit=bit):
                fn(pl.multiple_of(base + off, ROW_ALIGN), pl.multiple_of(lo + off, ROW_ALIGN), bit)

            off = off + piece


def _dispatch_body(nblk, tb, tab_ref, tabp_ref, lp_ref, x_ref, xs_in_ref, xs_ref, buf_ref, sem):
    del xs_in_ref
    s = pl.program_id(0)
    slot = s % 2
    lp = lp_ref[0]
    r = lax.broadcasted_iota(i32, (buf_ref.shape[1], 1), 0)
    perm = jnp.where((lp[0:1, :] == r) | (lp[1:2, :] == r), 1.0, 0.0).astype(bf16)
    buf_ref[slot] = _dot(perm, x_ref[...]).astype(bf16)

    def copy(sl):
        return lambda base, lo, rows: pltpu.make_async_copy(
            buf_ref.at[sl, pl.ds(lo, rows)], xs_ref.at[pl.ds(base, rows)], sem.at[sl])

    _for_each_run(tab_ref, tb, lambda *a: copy(slot)(*a).start())

    @pl.when(s > 0)
    def _():
        _for_each_run(tabp_ref, tb, lambda *a: copy(1 - slot)(*a).wait())

    @pl.when(s == nblk - 1)
    def _():
        _for_each_run(tab_ref, tb, lambda *a: copy(slot)(*a).wait())


def _dispatch(xn, tab, lp_rows, xs0, tb):
    nblk = tab.shape[0]
    return pl.pallas_call(
        functools.partial(_dispatch_body, nblk, tb),
        grid=(nblk,),
        in_specs=[pl.BlockSpec((1, 1, TAB_W), lambda s: (s, 0, 0), memory_space=pltpu.SMEM),
                  pl.BlockSpec((1, 1, TAB_W), lambda s: (jnp.maximum(s - 1, 0), 0, 0), memory_space=pltpu.SMEM),
                  pl.BlockSpec((1, 8, tb), lambda s: (s, 0, 0)),
                  pl.BlockSpec((tb, D), lambda s: (s, 0)),
                  pl.BlockSpec(memory_space=pl.ANY)],
        out_specs=pl.BlockSpec(memory_space=pl.ANY),
        out_shape=jax.ShapeDtypeStruct(xs0.shape, bf16),
        scratch_shapes=[pltpu.VMEM((2, _sorted_rows(tb), D), bf16), pltpu.SemaphoreType.DMA((2,))],
        input_output_aliases={4: 0},
        compiler_params=pltpu.CompilerParams(dimension_semantics=("arbitrary",), has_side_effects=True,
                                             vmem_limit_bytes=VMEM_LIMIT),
        name="moe_dispatch",
    )(tab, tab, lp_rows, xn, xs0)


def _experts_body(nf, sub, te_ref, tv_ref, x_ref, wg_ref, wu_ref, wd_ref, y_ref, acc_ref):
    del te_ref
    t = pl.program_id(0)
    f = pl.program_id(1)
    valid = tv_ref[t] == 1

    @pl.when(valid & (f == 0))
    def _():
        acc_ref[...] = jnp.zeros_like(acc_ref)

    @pl.when(valid)
    def _():
        x = x_ref[...]
        acc = acc_ref[...]
        for k in range(wg_ref.shape[1] // sub):
            g = _dot(x, wg_ref[:, k * sub:(k + 1) * sub])
            u = _dot(x, wu_ref[:, k * sub:(k + 1) * sub])
            a = (g * jax.nn.sigmoid(g) * u).astype(bf16)
            acc = acc + _dot(a, wd_ref[k * sub:(k + 1) * sub, :])
        acc_ref[...] = acc

    @pl.when(valid & (f == nf - 1))
    def _():
        y_ref[...] = acc_ref[...].astype(bf16)

    @pl.when(jnp.logical_not(valid) & (f == nf - 1))
    def _():
        y_ref[...] = jnp.zeros_like(y_ref)


def _experts(xs, te, tv, wg, wu, wd, tme, tf, sub):
    rows = xs.shape[0]
    T = rows // tme
    nf = D_FF_EXPERT // tf
    grid_spec = pltpu.PrefetchScalarGridSpec(
        num_scalar_prefetch=2,
        grid=(T, nf),
        in_specs=[pl.BlockSpec((tme, D), lambda t, f, te, tv: (t, 0)),
                  pl.BlockSpec((None, D, tf), lambda t, f, te, tv: (te[t], 0, f)),
                  pl.BlockSpec((None, D, tf), lambda t, f, te, tv: (te[t], 0, f)),
                  pl.BlockSpec((None, tf, D), lambda t, f, te, tv: (te[t], f, 0))],
        out_specs=pl.BlockSpec((tme, D), lambda t, f, te, tv: (t, 0)),
        scratch_shapes=[pltpu.VMEM((tme, D), f32)],
    )
    return pl.pallas_call(
        functools.partial(_experts_body, nf, sub),
        grid_spec=grid_spec,
        out_shape=jax.ShapeDtypeStruct((rows, D), bf16),
        compiler_params=_cparams(("arbitrary", "arbitrary")),
        name="moe_experts",
    )(te, tv, xs, wg, wu, wd)


def _combine_body(nblk, tb, nb, tab_ref, tabn_ref, aux_ref, h_ref, ys_ref, o_ref, buf_ref, obuf_ref, sem, osem):
    s = pl.program_id(0)
    slot = s % 2
    sub = tb // BLK

    def copy(sl):
        return lambda base, lo, rows: pltpu.make_async_copy(
            ys_ref.at[pl.ds(base, rows)], buf_ref.at[sl, pl.ds(lo, rows)], sem.at[sl])

    @pl.when(s == 0)
    def _():
        _for_each_run(tab_ref, tb, lambda *a: copy(slot)(*a).start())

    @pl.when(s + 1 < nblk)
    def _():
        _for_each_run(tabn_ref, tb, lambda *a: copy(1 - slot)(*a).start())

    _for_each_run(tab_ref, tb, lambda *a: copy(slot)(*a).wait())

    total = tab_ref[0, 0, TAB_LO + N_EXPERTS - 1] + tab_ref[0, 0, TAB_CNT + N_EXPERTS - 1]
    rows = lax.broadcasted_iota(i32, (buf_ref.shape[1], 1), 0)
    ys = jnp.where(rows < total, buf_ref[slot], jnp.zeros((), bf16))
    aux = aux_ref[...]
    lane = lax.broadcasted_iota(i32, (1, buf_ref.shape[1]), 1)
    w = (jnp.where(lane == aux[:, 0:1].astype(i32), aux[:, 2:3], 0.0)
         + jnp.where(lane == aux[:, 1:2].astype(i32), aux[:, 3:4], 0.0))
    obuf_ref[slot] = h_ref[...] + _dot(w.astype(bf16), ys)

    def out_copies(step, sl, fn):
        for q in range(sub):
            pb = step * sub + q

            @pl.when(pb % nb != 0)
            def _(pb=pb, q=q):
                orow = pl.multiple_of(((pb // nb) * (nb - 1) + pb % nb - 1) * BLK, BLK)
                fn(pltpu.make_async_copy(obuf_ref.at[sl, pl.ds(q * BLK, BLK)], o_ref.at[pl.ds(orow, BLK)],
                                         osem.at[sl]))

    out_copies(s, slot, lambda c: c.start())

    @pl.when(s > 0)
    def _():
        out_copies(s - 1, 1 - slot, lambda c: c.wait())

    @pl.when(s == nblk - 1)
    def _():
        out_copies(s, slot, lambda c: c.wait())


def _combine(h, aux, tab, ys, B, nb, tb):
    nblk = tab.shape[0]
    return pl.pallas_call(
        functools.partial(_combine_body, nblk, tb, nb),
        grid=(nblk,),
        in_specs=[pl.BlockSpec((1, 1, TAB_W), lambda s: (s, 0, 0), memory_space=pltpu.SMEM),
                  pl.BlockSpec((1, 1, TAB_W), lambda s: (jnp.minimum(s + 1, nblk - 1), 0, 0),
                               memory_space=pltpu.SMEM),
                  pl.BlockSpec((tb, LANES), lambda s: (s, 0)),
                  pl.BlockSpec((tb, D), lambda s: (s, 0)),
                  pl.BlockSpec(memory_space=pl.ANY)],
        out_specs=pl.BlockSpec(memory_space=pl.ANY),
        out_shape=jax.ShapeDtypeStruct((B * (nb - 1) * BLK, D), f32),
        scratch_shapes=[pltpu.VMEM((2, _sorted_rows(tb), D), bf16), pltpu.VMEM((2, tb, D), f32),
                        pltpu.SemaphoreType.DMA((2,)), pltpu.SemaphoreType.DMA((2,))],
        compiler_params=_cparams(("arbitrary",)),
        name="moe_combine",
    )(tab, tab, aux, h, ys)


def _routing_tables(info, cnt_rows, B, nb, tb, tme):
    n = info.shape[0]
    nblk = n // tb
    ar = jnp.arange(N_EXPERTS, dtype=i32)
    e1 = info[:, 0].astype(i32)
    e2 = info[:, 1].astype(i32)
    blk = jnp.arange(n // BLK, dtype=i32)[:, None]
    rowin = jnp.arange(BLK, dtype=i32)[None, :]
    real = ((blk % nb != 0) | (rowin >= PAD)).reshape(n)
    oh1 = (e1[:, None] == ar) & real[:, None]
    oh2 = (e2[:, None] == ar) & real[:, None]
    blk_cnt = (oh1 | oh2).astype(i32).reshape(nblk, tb, N_EXPERTS).sum(axis=1)
    blk_rank0 = jnp.cumsum(blk_cnt, axis=0) - blk_cnt
    run = ((blk_cnt + ROW_ALIGN - 1) // ROW_ALIGN) * ROW_ALIGN
    run0 = jnp.cumsum(run, axis=0) - run
    seg = ((jnp.sum(run, axis=0) + tme - 1) // tme) * tme
    ends = jnp.cumsum(seg)
    offs = ends - seg
    base = offs[None, :] + run0
    lo = jnp.cumsum(run, axis=1) - run
    tab = jnp.concatenate([base, lo, run, jnp.zeros_like(base)], axis=1).reshape(nblk, 1, TAB_W)

    shift = jnp.repeat(lo - blk_rank0, tb, axis=0)
    lp1 = jnp.where(real, jnp.sum(jnp.where(oh1, shift, 0), axis=1) + info[:, 4].astype(i32), -1)
    lp2 = jnp.where(real, jnp.sum(jnp.where(oh2, shift, 0), axis=1) + info[:, 5].astype(i32), -1)
    lp_rows = jnp.concatenate([lp1.reshape(nblk, 1, tb), lp2.reshape(nblk, 1, tb),
                               jnp.full((nblk, 6, tb), -1, i32)], axis=1)
    aux = jnp.concatenate([lp1.astype(f32)[:, None], lp2.astype(f32)[:, None], info[:, 2:4],
                           jnp.zeros((n, LANES - 4), f32)], axis=1)
    del cnt_rows
    return tab, lp_rows, aux, ends


def _strip_body(h_ref, o_ref):
    o_ref[...] = h_ref[...]


def _strip_meta(h, B, nb):
    nt = nb - 1
    return pl.pallas_call(
        _strip_body,
        grid=(B, nt),
        in_specs=[pl.BlockSpec((BLK, D), lambda b, i: (b * nb + i + 1, 0))],
        out_specs=pl.BlockSpec((BLK, D), lambda b, i: (b * nt + i, 0)),
        out_shape=jax.ShapeDtypeStruct((B * nt * BLK, D), f32),
        compiler_params=_cparams(("parallel", "arbitrary")),
        name="strip_meta",
    )(h)


def _bucket_lookup(table, bucket):
    tshape = (table.shape[1],) + (1,) * bucket.ndim
    out = jnp.zeros((table.shape[1],) + bucket.shape, f32)
    for k in range(N_BUCKETS):
        out = out + jnp.where((bucket == k)[None], table[k].astype(f32).reshape(tshape), 0.0)
    return out


def _bias_tables(rel, nb):
    table_a = rel[:, :A_HEADS] * LOG2E
    table_b = rel[:, A_HEADS:] * LOG2E
    kk = jnp.arange(2 * BLK)[:, None]
    qq = jnp.arange(BLK)[None, :]
    n = BLK + qq - kk
    in_win = (n >= 0) & (n < BLK)
    bias = _bucket_lookup(table_a, _t5_bucket(jnp.maximum(n, 0)))
    variants = []
    for blk in range(3):
        key_ok = (blk - 1) * BLK + kk >= PAD
        variants.append(jnp.where((in_win & key_ok)[None], bias, NEG))
    bias_at = jnp.stack(variants, axis=0)

    d = jnp.arange(nb + 1)[:, None, None]
    kb = jnp.arange(BLK)[None, :, None]
    qb = jnp.arange(BLK)[None, None, :]
    nn = jnp.where(d < nb, d * BLK + qb - kb, -1)
    bias_bt = _bucket_lookup(table_b, _t5_bucket(jnp.maximum(nn, 0)))
    bias_bt = jnp.where((nn >= 0)[None], bias_bt, NEG)
    return bias_at, bias_bt


def _block_diag_ones(gs):
    idx = np.arange(LANES) // gs
    return jnp.asarray((idx[:, None] == idx[None, :]).astype(np.float32), dtype=bf16)


def _col_scale(gain_a, gain_b):
    one = jnp.ones((256,), f32)
    cs = jnp.concatenate([one, jnp.tile(gain_b[1], 8), jnp.tile(gain_a[1], 2)]).reshape(1, D_ROW)
    cst = jnp.concatenate([jnp.tile(gain_a[0], 8) * (64 ** -0.5 * LOG2E),
                           jnp.tile(gain_b[0], 8) * (32 ** -0.5 * LOG2E)]).reshape(R_VB, 1)
    return cs.astype(f32), cst.astype(f32)


def _split_in_weight(w):
    main = jnp.concatenate([w[:, 1536:1792], w[:, 1024:1280], w[:, 512:640]], axis=1)
    tr = jnp.concatenate([w[:, 0:512], w[:, 768:1024], w[:, 1280:1536], w[:, 640:768]], axis=1).T
    return main.astype(bf16), tr.astype(bf16)


def _pool_weight(w_pool):
    out = jnp.zeros((256, 256), f32)
    for gi in range(4):
        out = out.at[gi * 64:(gi + 1) * 64, gi * 64:(gi + 1) * 64].set(w_pool[gi])
    return out.astype(bf16)


def _row_tile(n):
    for tm in (512, 256, 128):
        if n % tm == 0:
            return tm
    raise ValueError(n)


def kernel(x, meta_tokens, rel_bias_table, g_mix, w_in, qk_gain_a, sinks, qk_gain_b, diff_lambda, g_diff_out,
           w_pool, pool_scale, w_out, g_ffn, w_ffn_gate, w_ffn_up, w_ffn_down, w_router, w_exp_gate, w_exp_up,
           w_exp_down):
    B, seq, _ = x.shape
    assert seq % BLK == 0
    depth = g_mix.shape[0]
    nb = seq // BLK + 1
    L = nb * BLK
    n = B * L
    tm = _row_tile(n)
    tm_in = 2 * tm if n % (2 * tm) == 0 else tm
    tme = 1024
    tf_e = 1792
    tf_d = 256
    kb_d = 4

    bias_at, bias_bt = _bias_tables(rel_bias_table, nb)
    g64 = _block_diag_ones(64)
    g32 = _block_diag_ones(32)

    h = _embed(x, meta_tokens, B, nb)
    out = None
    for l in range(depth):
        lam_init = 0.8 - 0.6 * math.exp(-0.3 * l)
        cs, cst = _col_scale(qk_gain_a[l], qk_gain_b[l])
        w_main, w_tr = _split_in_weight(w_in[l])
        z, ft = _inproj(h, g_mix[l].reshape(1, D), w_main, w_tr, cs, cst, g64, g32, tm_in)
        oa, oc = _swa_pool(z, ft, bias_at, sinks[l], _pool_weight(w_pool[l]), pool_scale[l].reshape(1, 256), B, nb)
        ob = _diff_attn(ft, z.reshape(B, L, D_ROW), bias_bt, diff_lambda[l],
                        jnp.tile(g_diff_out[l], 2).reshape(BLK, 1), lam_init, B, nb, kb_d)
        wo = w_out[l].astype(bf16)
        gf = g_ffn[l].reshape(1, D)
        j = l // 2
        if l % 2 == 0:
            h = _outproj_ffn(oa, ob, oc, wo, h, gf, w_ffn_gate[j].astype(bf16), w_ffn_up[j].astype(bf16),
                             w_ffn_down[j].astype(bf16), tm, tf_d)
            out = None
        else:
            assert l == depth - 1, "the expert layer must be the last layer"
            wr = jnp.pad(w_router[j], ((0, 0), (0, LANES - N_EXPERTS)))
            wrh = wr.astype(bf16)
            wrl = (wr - wrh.astype(f32)).astype(bf16)
            tri = jnp.asarray(np.tril(np.ones((tm // 2, tm // 2), np.float32), -1), dtype=bf16)
            h, xn, info, cnt = _outproj_router(oa, ob, oc, wo, h, gf, tm, (wrh, wrl, tri), nb)

            tab, lp_rows, aux, ends = _routing_tables(info, cnt, B, nb, tm, tme)
            rows_max = B * (seq + N_META) * 2 + (n // tm) * N_EXPERTS * (ROW_ALIGN - 1) + N_EXPERTS * (tme - 1)
            T = -(-rows_max // tme)
            starts = jnp.arange(T, dtype=i32) * tme
            te = jnp.minimum(jnp.sum((ends[None, :] <= starts[:, None]).astype(i32), axis=1), N_EXPERTS - 1)
            tv = (starts < ends[-1]).astype(i32)

            xs = _dispatch(xn, tab, lp_rows, jnp.zeros((T * tme, D), bf16), tm)
            ys = _experts(xs, te, tv, w_exp_gate[j].astype(bf16), w_exp_up[j].astype(bf16),
                          w_exp_down[j].astype(bf16), tme, tf_e, 256)
            out = _combine(h, aux, tab, ys, B, nb, tm)
    if out is None:
        out = _strip_meta(h, B, nb)
    return out.reshape(B, seq, D)
```

```python
import functools
import math

import jax
import jax.numpy as jnp
import numpy as np
from jax import lax
from jax.experimental import pallas as pl
from jax.experimental.pallas import tpu as pltpu

f32 = jnp.float32
bf16 = jnp.bfloat16
i32 = jnp.int32

D = 1024
BLK = 128
N_META = 16
PAD = BLK - N_META
A_HEADS = 8
B_HEADS = 4
N_BUCKETS = 32
MAX_DISTANCE = 128
N_EXPERTS = 8
D_FF_EXPERT = 3584
EPS = 1e-6
NEG = -1e30
LANES = 128
D_ROW = 640
D_FM = 1152
R_QB, R_VB, R_VA = 512, 768, 1024
LOG2E = 1.4426950408889634

VMEM_LIMIT = 56 * 1024 * 1024


def _cparams(sem):
    return pltpu.CompilerParams(dimension_semantics=sem, vmem_limit_bytes=VMEM_LIMIT)


def _t5_bucket(n):
    max_exact = N_BUCKETS // 2
    large = max_exact
    for j in range(1, N_BUCKETS - max_exact):
        thr = math.ceil(max_exact * (MAX_DISTANCE / max_exact) ** (j / (N_BUCKETS - max_exact)))
        large = large + (n >= thr).astype(i32)
    return jnp.where(n < max_exact, n, large)


def _dot(a, b):
    return jnp.dot(a, b, preferred_element_type=f32)


def _dot_nt(a, b):
    return lax.dot_general(a, b, (((1,), (1,)), ((), ())), preferred_element_type=f32)


def _embed_body(x_ref, m_ref, o_ref):
    o_ref[0:PAD, :] = jnp.zeros((PAD, D), f32)
    o_ref[PAD:BLK, :] = m_ref[...]
    o_ref[BLK:, :] = x_ref[...]


def _embed(x, meta, B, nb):
    seq = x.shape[1]
    return pl.pallas_call(
        _embed_body,
        grid=(B,),
        in_specs=[pl.BlockSpec((None, seq, D), lambda b: (b, 0, 0)),
                  pl.BlockSpec((N_META, D), lambda b: (0, 0))],
        out_specs=pl.BlockSpec((nb * BLK, D), lambda b: (b, 0)),
        out_shape=jax.ShapeDtypeStruct((B * nb * BLK, D), f32),
        compiler_params=_cparams(("parallel",)),
        name="embed",
    )(x, meta)


def _split_bf16(x):
    hi = x.astype(bf16)
    return hi, (x - hi.astype(f32)).astype(bf16)


def _inproj_body(tm, h_ref, g_ref, w_ref, wt_ref, cs_ref, cst_ref, g64_ref, g32_ref, z_ref, ft_ref):
    sub = tm
    for r0 in range(0, tm, sub):
        rows = slice(r0, r0 + sub)
        x = h_ref[rows, :]
        ms = jnp.mean(x * x, axis=-1, keepdims=True)
        xn = (x * lax.rsqrt(ms + EPS) * g_ref[...]).astype(bf16)

        for c in range(D_ROW // LANES):
            zc = _dot(xn, w_ref[:, c * LANES:(c + 1) * LANES])
            if c >= 2:
                gm, inv = (g32_ref, 1.0 / 32) if c < 4 else (g64_ref, 1.0 / 64)
                ss = _dot((zc * zc).astype(bf16), gm[...])
                zc = zc * lax.rsqrt(ss * inv + EPS)
            z_ref[rows, c * LANES:(c + 1) * LANES] = (zc * cs_ref[:, c * LANES:(c + 1) * LANES]).astype(bf16)

        zt_all = _dot_nt(wt_ref[...], xn)
        for c in range(D_FM // LANES):
            zt = zt_all[c * LANES:(c + 1) * LANES]
            if c < R_VB // LANES:
                gm, inv = (g64_ref, 1.0 / 64) if c < R_QB // LANES else (g32_ref, 1.0 / 32)
                ss = _dot(gm[...], (zt * zt).astype(bf16))
                zt = zt * lax.rsqrt(ss * inv + EPS) * cst_ref[c * LANES:(c + 1) * LANES, :]
            for k in range(sub // BLK):
                ft_ref[r0 // BLK + k, c * LANES:(c + 1) * LANES, :] = zt[:, k * BLK:(k + 1) * BLK].astype(bf16)


def _inproj(h, g, w, wt, cs, cst, g64, g32, tm):
    n = h.shape[0]
    const = lambda i: (0, 0)
    kb = tm // BLK
    return pl.pallas_call(
        functools.partial(_inproj_body, tm),
        grid=(n // tm,),
        in_specs=[pl.BlockSpec((tm, D), lambda i: (i, 0)),
                  pl.BlockSpec((1, D), const),
                  pl.BlockSpec((D, D_ROW), const),
                  pl.BlockSpec((D_FM, D), const),
                  pl.BlockSpec((1, D_ROW), const),
                  pl.BlockSpec((R_VB, 1), const),
                  pl.BlockSpec((LANES, LANES), const),
                  pl.BlockSpec((LANES, LANES), const)],
        out_specs=[pl.BlockSpec((tm, D_ROW), lambda i: (i, 0)),
                   pl.BlockSpec((kb, D_FM, BLK), lambda i: (i, 0, 0))],
        out_shape=[jax.ShapeDtypeStruct((n, D_ROW), bf16),
                   jax.ShapeDtypeStruct((n // BLK, D_FM, BLK), bf16)],
        compiler_params=_cparams(("parallel",)),
        name="norm_inproj",
    )(h, g, w, wt, cs, cst, g64, g32)


def _swa_pool_body(lanes, *refs):
    ins = [refs[7 * s:7 * s + 7] for s in range(lanes)]
    bias_ref, sink_ref, wp_ref, ps_ref, oa_ref, oc_ref = refs[7 * lanes:7 * lanes + 6]
    i = pl.program_id(1)
    zeros = jnp.zeros((64, BLK), bf16)
    row = lax.broadcasted_iota(i32, (BLK, 1), 0)
    t = i * BLK + row - PAD
    lane2 = lax.broadcasted_iota(i32, (1, 256), 1)
    grp2 = lane2 // 64
    win = jnp.where(grp2 == 0, 2, jnp.where(grp2 == 1, 4, jnp.where(grp2 == 2, 8, 16)))
    cnt = jnp.maximum(jnp.minimum(t + 1, win), 1).astype(f32)
    for s in range(lanes):
        qt_ref, kp_ref, kc_ref, vtp_ref, vtc_ref, up_ref, uc_ref = ins[s]
        kcat = jnp.concatenate([kp_ref[...], kc_ref[...]], axis=0)
        vt = jnp.concatenate([vtp_ref[...], vtc_ref[...]], axis=1)
        for grp in range(2):
            pts, inv_ls = [], []
            for hh in range(4):
                h = 4 * grp + hh
                qrows = qt_ref[64 * h:64 * h + 64, :]
                qmt = jnp.concatenate([qrows, zeros] if grp == 0 else [zeros, qrows], axis=0)
                st = _dot(kcat, qmt) + bias_ref[h]
                sink = sink_ref[h] * LOG2E
                m = jnp.maximum(jnp.max(st, axis=0, keepdims=True), sink)
                p = jnp.exp2(st - m)
                inv_ls.append(1.0 / (jnp.sum(p, axis=0, keepdims=True) + jnp.exp2(sink - m)))
                pts.append(p.astype(bf16))
            ot = _dot(vt[64 * grp:64 * grp + 64, :], jnp.concatenate(pts, axis=1))
            for pair in range(2):
                o2 = jnp.concatenate([ot[:, (2 * pair + k) * BLK:(2 * pair + k + 1) * BLK] * inv_ls[2 * pair + k]
                                      for k in range(2)], axis=0)
                pb = 2 * grp + pair
                oa_ref[s, :, pb * LANES:(pb + 1) * LANES] = o2.T.astype(bf16)

        u_cur = jnp.where(t >= 0, uc_ref[...].astype(f32), 0.0)
        u_prev = jnp.where(i > 0, up_ref[BLK - 16:, :].astype(f32), 0.0)
        ext = jnp.concatenate([u_prev, u_cur], axis=0)
        s2 = ext + pltpu.roll(ext, 1, 0)
        s4 = s2 + pltpu.roll(s2, 2, 0)
        s8 = s4 + pltpu.roll(s4, 4, 0)
        s16 = s8 + pltpu.roll(s8, 8, 0)
        sums = jnp.where(grp2 == 0, s2, jnp.where(grp2 == 1, s4, jnp.where(grp2 == 2, s8, s16)))[16:, :]
        dlt = (sums / cnt - u_cur).astype(bf16)
        oc_ref[s] = (_dot(dlt, wp_ref[...]) * ps_ref[...]).astype(bf16)


def _swa_pool(z, ft, bias_at, sinks, wpool_bd, pool_scale, B, nb):
    L = nb * BLK
    lanes = next(k for k in (4, 2, 1) if B % k == 0)
    in_specs, operands = [], []
    for s in range(lanes):
        cur = lambda c, s=s: (lambda g, i: ((lanes * g + s) * nb + i, c))
        prev = lambda c, s=s: (lambda g, i: ((lanes * g + s) * nb + jnp.maximum(i - 1, 0), c))
        cur3 = lambda r, s=s: (lambda g, i: ((lanes * g + s) * nb + i, r, 0))
        prev3 = lambda r, s=s: (lambda g, i: ((lanes * g + s) * nb + jnp.maximum(i - 1, 0), r, 0))
        in_specs += [pl.BlockSpec((None, 512, BLK), cur3(0)),
                     pl.BlockSpec((BLK, LANES), prev(4)),
                     pl.BlockSpec((BLK, LANES), cur(4)),
                     pl.BlockSpec((None, BLK, BLK), prev3(R_VA // BLK)),
                     pl.BlockSpec((None, BLK, BLK), cur3(R_VA // BLK)),
                     pl.BlockSpec((BLK, 256), prev(0)),
                     pl.BlockSpec((BLK, 256), cur(0))]
        operands += [ft, z, z, ft, ft, z, z]
    in_specs += [pl.BlockSpec((None, A_HEADS, 2 * BLK, BLK), lambda g, i: (jnp.minimum(i, 2), 0, 0, 0)),
                 pl.BlockSpec(memory_space=pltpu.SMEM),
                 pl.BlockSpec((256, 256), lambda g, i: (0, 0)),
                 pl.BlockSpec((1, 256), lambda g, i: (0, 0))]
    operands += [bias_at, sinks, wpool_bd, pool_scale]
    oa, oc = pl.pallas_call(
        functools.partial(_swa_pool_body, lanes),
        grid=(B // lanes, nb),
        in_specs=in_specs,
        out_specs=[pl.BlockSpec((None, lanes, BLK, 512), lambda g, i: (g, 0, i, 0)),
                   pl.BlockSpec((None, lanes, BLK, 256), lambda g, i: (g, 0, i, 0))],
        out_shape=[jax.ShapeDtypeStruct((B // lanes, lanes, L, 512), bf16),
                   jax.ShapeDtypeStruct((B // lanes, lanes, L, 256), bf16)],
        compiler_params=_cparams(("parallel", "arbitrary")),
        name="swa_pool",
    )(*operands)
    return oa.reshape(B * L, 512), oc.reshape(B * L, 256)


def _diff_attn_body(lam_init, nb, kb_step, lanes, *refs):
    ins = [refs[3 * s:3 * s + 3] for s in range(lanes)]
    bias_ref, dl_ref, go_ref = refs[3 * lanes:3 * lanes + 3]
    o_ref = refs[3 * lanes + 3]
    scr = [refs[3 * lanes + 4 + 7 * s:3 * lanes + 4 + 7 * s + 7] for s in range(lanes)]
    i = pl.program_id(1)
    row = lax.broadcasted_iota(i32, (BLK, 1), 0)
    zero = jnp.zeros((), bf16)
    rows_k = lax.broadcasted_iota(i32, (kb_step * BLK, 1), 0)
    padmask = jnp.where(rows_k < PAD, NEG, 0.0).astype(f32)

    qcat = []
    for s in range(lanes):
        qt_ref = ins[s][0]
        per_lane = []
        for p in range(2):
            qp = qt_ref[p * BLK:(p + 1) * BLK, :]
            per_lane.append([jnp.concatenate(
                [jnp.where((row >= 64 * e + 32 * c) & (row < 64 * e + 32 * c + 32), qp, zero) for c in range(2)],
                axis=1) for e in range(2)])
        qcat.append(per_lane)
        acc_ref, m_ref, l_ref = scr[s][0:3]
        m_ref[...] = jnp.full(m_ref.shape, 3 * NEG, f32)
        l_ref[...] = jnp.zeros(l_ref.shape, f32)
        acc_ref[...] = jnp.zeros(acc_ref.shape, f32)

    def blocks(st):
        j0 = st * kb_step
        return [jnp.minimum(j0 + kb, nb - 1) for kb in range(kb_step)], j0

    def scores(s, st, first, par):
        k_ref = ins[s][1]
        s_ref, x_ref = scr[s][3 + par], scr[s][5 + par]
        blks, j0 = blocks(st)
        dix = [jnp.where(i - (j0 + kb) >= 0, i - (j0 + kb), nb) for kb in range(kb_step)]
        kparts = [k_ref[pl.ds(pl.multiple_of(b * BLK, BLK), BLK), :] for b in blks]
        for p in range(2):
            kcat = jnp.concatenate([kp[:, p * BLK:(p + 1) * BLK] for kp in kparts], axis=0)
            for e in range(2):
                h = 2 * p + e
                st2 = _dot(kcat, qcat[s][p][e])
                bias = jnp.concatenate([bias_ref[h, dd] for dd in dix], axis=0)
                if first:
                    bias = bias + padmask
                for c in range(2):
                    tile = st2[:, c * BLK:(c + 1) * BLK] + bias
                    s_ref[2 * h + c] = tile
                    x_ref[2 * h + c:2 * h + c + 1, :] = jnp.max(tile, axis=0, keepdims=True)

    def consume(s, st, par):
        vt_ref = ins[s][2]
        acc_ref, m_ref, l_ref = scr[s][0:3]
        s_ref, x_ref = scr[s][3 + par], scr[s][5 + par]
        blks, _ = blocks(st)
        vparts = [vt_ref[b] for b in blks]
        for p in range(2):
            vtc = jnp.concatenate([vp[p * BLK:(p + 1) * BLK, :] for vp in vparts], axis=1)
            for c in range(2):
                for e in range(2):
                    idx = 2 * (2 * p + e) + c
                    m_old = m_ref[idx:idx + 1, :]
                    m_new = jnp.maximum(m_old, x_ref[idx:idx + 1, :])
                    alpha = jnp.exp2(m_old - m_new)
                    pt = jnp.exp2(s_ref[idx] - m_new)
                    l_ref[idx:idx + 1, :] = alpha * l_ref[idx:idx + 1, :] + jnp.sum(pt, axis=0, keepdims=True)
                    m_ref[idx:idx + 1, :] = m_new
                    rows = slice(64 * e, 64 * e + 64)
                    upd = _dot(vtc[rows, :], pt.astype(bf16))
                    acc_ref[2 * p + c, rows, :] = acc_ref[2 * p + c, rows, :] * alpha + upd

    n_steps = (i + kb_step) // kb_step
    for s in range(lanes):
        scores(s, 0, True, 0)

    def body(st, carry):
        for par in range(2):
            @pl.when(st % 2 == par)
            def _(par=par):
                for s in range(lanes):
                    scores(s, st + 1, False, 1 - par)
                for s in range(lanes):
                    consume(s, st, par)

        return carry

    lax.fori_loop(0, n_steps - 1, body, 0)
    last = n_steps - 1
    for par in range(2):
        @pl.when(last % 2 == par)
        def _(par=par):
            for s in range(lanes):
                consume(s, last, par)

    dl = dl_ref[...]
    lam = (jnp.exp(jnp.sum(dl[0:1] * dl[1:2], axis=-1, keepdims=True))
           - jnp.exp(jnp.sum(dl[2:3] * dl[3:4], axis=-1, keepdims=True)) + lam_init)
    for s in range(lanes):
        acc_ref, _, l_ref = scr[s][0:3]
        for p in range(2):
            parts = []
            for e in range(2):
                h = 2 * p + e
                rows = slice(64 * e, 64 * e + 64)
                o = (acc_ref[2 * p, rows, :] / l_ref[2 * h:2 * h + 1, :]
                     - lam * (acc_ref[2 * p + 1, rows, :] / l_ref[2 * h + 1:2 * h + 2, :]))
                ssq = jnp.sum(o * o, axis=0, keepdims=True)
                parts.append(o * lax.rsqrt(ssq * (1.0 / 64) + EPS))
            o = jnp.concatenate(parts, axis=0) * go_ref[...] * (1.0 - lam_init)
            o_ref[s, :, p * BLK:(p + 1) * BLK] = o.T.astype(bf16)


def _diff_attn(ft, z3, bias_bt, dl, go, lam_init, B, nb, kb_step):
    L = nb * BLK
    lanes = next(k for k in (4, 2, 1) if B % k == 0)
    in_specs, scratch = [], []
    for s in range(lanes):
        bat = lambda g, s=s: lanes * g + s
        in_specs += [pl.BlockSpec((None, 256, BLK), lambda g, i, bat=bat: (bat(g) * nb + i, R_QB // 256, 0)),
                     pl.BlockSpec((None, L, 256), lambda g, i, bat=bat: (bat(g), 0, 1)),
                     pl.BlockSpec((nb, 256, BLK), lambda g, i, bat=bat: (bat(g), R_VB // 256, 0))]
        scratch += [pltpu.VMEM((4, BLK, BLK), f32), pltpu.VMEM((8, BLK), f32), pltpu.VMEM((8, BLK), f32),
                    pltpu.VMEM((8, kb_step * BLK, BLK), f32), pltpu.VMEM((8, kb_step * BLK, BLK), f32),
                    pltpu.VMEM((8, BLK), f32), pltpu.VMEM((8, BLK), f32)]
    in_specs += [pl.BlockSpec((B_HEADS, nb + 1, BLK, BLK), lambda g, i: (0, 0, 0, 0),
                              pipeline_mode=pl.Buffered(1)),
                 pl.BlockSpec((4, 32), lambda g, i: (0, 0)),
                 pl.BlockSpec((BLK, 1), lambda g, i: (0, 0))]
    operands = [ft, z3, ft] * lanes + [bias_bt, dl, go]
    return pl.pallas_call(
        functools.partial(_diff_attn_body, lam_init, nb, kb_step, lanes),
        grid=(B // lanes, nb),
        in_specs=in_specs,
        out_specs=pl.BlockSpec((None, lanes, BLK, 256), lambda g, i: (g, 0, i, 0)),
        out_shape=jax.ShapeDtypeStruct((B // lanes, lanes, L, 256), bf16),
        scratch_shapes=scratch,
        compiler_params=_cparams(("parallel", "arbitrary")),
        name="diff_attn",
    )(*operands).reshape(B * L, 256)


def _outproj_common(oa_ref, ob_ref, oc_ref, w_ref, h_ref, g_ref, rows=slice(None)):
    y = (_dot(oa_ref[rows, :], w_ref[0:512, :]) + _dot(ob_ref[rows, :], w_ref[512:768, :])
         + _dot(oc_ref[rows, :], w_ref[768:1024, :]))
    hn = h_ref[rows, :] + y
    ms = jnp.mean(hn * hn, axis=-1, keepdims=True)
    return hn, hn * lax.rsqrt(ms + EPS) * g_ref[...]


def _outproj_ffn_body(tf, oa_ref, ob_ref, oc_ref, w_ref, h_ref, g_ref, wg_ref, wu_ref, wd_ref, o_ref):
    hn, xn = _outproj_common(oa_ref, ob_ref, oc_ref, w_ref, h_ref, g_ref)
    x = xn.astype(bf16)
    acc = hn
    for c in range(wg_ref.shape[1] // tf):
        g = _dot(x, wg_ref[:, c * tf:(c + 1) * tf])
        u = _dot(x, wu_ref[:, c * tf:(c + 1) * tf])
        a = (g * jax.nn.sigmoid(g) * u).astype(bf16)
        acc = acc + _dot(a, wd_ref[c * tf:(c + 1) * tf, :])
    o_ref[...] = acc


def _outproj_router_body(nb, tm, oa_ref, ob_ref, oc_ref, w_ref, h_ref, g_ref, wrh_ref, wrl_ref, tri_ref,
                         ho_ref, xn_ref, info_ref, cnt_ref):
    step = pl.program_id(0)
    th = tri_ref.shape[0]
    lane = lax.broadcasted_iota(i32, (1, LANES), 1)
    rowin = lax.broadcasted_iota(i32, (BLK, 1), 0)
    ninf = -jnp.inf

    @pl.when(step == 0)
    def _():
        cnt_ref[...] = jnp.zeros_like(cnt_ref)

    base = cnt_ref[...]
    for r0 in range(0, tm, th):
        rows = slice(r0, r0 + th)
        hn, xn = _outproj_common(oa_ref, ob_ref, oc_ref, w_ref, h_ref, g_ref, rows)
        ho_ref[rows, :] = hn
        xn_ref[rows, :] = xn.astype(bf16)

        hi, lo = _split_bf16(xn)
        logits = _dot(hi, wrh_ref[...]) + _dot(lo, wrh_ref[...]) + _dot(hi, wrl_ref[...])
        lg = jnp.where(lane < N_EXPERTS, logits, ninf)
        m1 = jnp.max(lg, axis=-1, keepdims=True)
        i1 = jnp.min(jnp.where(lg == m1, lane, LANES), axis=-1, keepdims=True)
        lg2 = jnp.where(lane == i1, ninf, lg)
        m2 = jnp.max(lg2, axis=-1, keepdims=True)
        i2 = jnp.min(jnp.where(lg2 == m2, lane, LANES), axis=-1, keepdims=True)
        e21 = jnp.exp(m2 - m1)
        g1 = 1.0 / (1.0 + e21)
        g2 = e21 * g1

        valids = []
        for k in range(th // BLK):
            blk = step * (tm // BLK) + r0 // BLK + k
            valids.append(jnp.logical_or(blk % nb != 0, rowin >= PAD))
        valid = jnp.concatenate(valids, axis=0)
        oh = jnp.where(valid & ((lane == i1) | (lane == i2)), 1.0, 0.0)
        rank = _dot(tri_ref[...], oh.astype(bf16)) + base
        r1 = jnp.sum(jnp.where(lane == i1, rank, 0.0), axis=-1, keepdims=True)
        r2 = jnp.sum(jnp.where(lane == i2, rank, 0.0), axis=-1, keepdims=True)
        base = base + jnp.sum(oh, axis=0, keepdims=True)
        info_ref[rows, :] = jnp.where(lane == 0, i1.astype(f32),
                            jnp.where(lane == 1, i2.astype(f32),
                            jnp.where(lane == 2, g1,
                            jnp.where(lane == 3, g2,
                            jnp.where(lane == 4, r1,
                            jnp.where(lane == 5, r2, 0.0))))))
    cnt_ref[...] = base


def _outproj_ffn(oa, ob, oc, w, h, g, wg, wu, wd, tm, tf):
    n = h.shape[0]
    F = wg.shape[1]
    row = lambda c: pl.BlockSpec((tm, c), lambda i: (i, 0))
    const = lambda r, c: pl.BlockSpec((r, c), lambda i: (0, 0), pipeline_mode=pl.Buffered(1))
    return pl.pallas_call(
        functools.partial(_outproj_ffn_body, tf),
        grid=(n // tm,),
        in_specs=[row(512), row(256), row(256), const(D, D), row(D), const(1, D),
                  const(D, F), const(D, F), const(F, D)],
        out_specs=row(D),
        out_shape=jax.ShapeDtypeStruct((n, D), f32),
        compiler_params=_cparams(("parallel",)),
        name="outproj_ffn",
    )(oa, ob, oc, w, h, g, wg, wu, wd)


def _outproj_router(oa, ob, oc, w, h, g, tm, router, nb):
    n = h.shape[0]
    row = lambda c: pl.BlockSpec((tm, c), lambda i: (i, 0))
    const = lambda r, c: pl.BlockSpec((r, c), lambda i: (0, 0))
    in_specs = [row(512), row(256), row(256), const(D, D), row(D), const(1, D)]
    wrh, wrl, tri = router
    return pl.pallas_call(
        functools.partial(_outproj_router_body, nb, tm),
        grid=(n // tm,),
        in_specs=in_specs + [const(D, LANES), const(D, LANES), const(*tri.shape)],
        out_specs=[row(D), row(D), row(LANES), const(1, LANES)],
        out_shape=[jax.ShapeDtypeStruct((n, D), f32), jax.ShapeDtypeStruct((n, D), bf16),
                   jax.ShapeDtypeStruct((n, LANES), f32), jax.ShapeDtypeStruct((1, LANES), f32)],
        compiler_params=_cparams(("arbitrary",)),
        name="outproj_router",
    )(oa, ob, oc, w, h, g, wrh, wrl, tri)


ROW_ALIGN = 16
TAB_BASE, TAB_LO, TAB_CNT, TAB_W = 0, N_EXPERTS, 2 * N_EXPERTS, 4 * N_EXPERTS


def _sorted_rows(tb):
    return -(-(2 * tb + N_EXPERTS * (ROW_ALIGN - 1)) // LANES) * LANES


def _for_each_run(tab_ref, tb, fn):
    bits = []
    b = pl.next_power_of_2(2 * tb)
    while b >= ROW_ALIGN:
        bits.append(b)
        b //= 2
    for e in range(N_EXPERTS):
        base = tab_ref[0, 0, TAB_BASE + e]
        lo = tab_ref[0, 0, TAB_LO + e]
        cnt = tab_ref[0, 0, TAB_CNT + e]
        off = 0
        for bit in bits:
            piece = cnt & bit

            @pl.when(piece != 0)
            def _(base=base, lo=lo, off=off, bit=bit):
                fn(pl.multiple_of(base + off, ROW_ALIGN), pl.multiple_of(lo + off, ROW_ALIGN), bit)

            off = off + piece


def _dispatch_body(nblk, tb, tab_ref, tabp_ref, lp_ref, x_ref, xs_in_ref, xs_ref, buf_ref, sem):
    del xs_in_ref
    s = pl.program_id(0)
    slot = s % 2
    lp = lp_ref[0]
    r = lax.broadcasted_iota(i32, (buf_ref.shape[1], 1), 0)
    perm = jnp.where((lp[0:1, :] == r) | (lp[1:2, :] == r), 1.0, 0.0).astype(bf16)
    buf_ref[slot] = _dot(perm, x_ref[...]).astype(bf16)

    def copy(sl):
        return lambda base, lo, rows: pltpu.make_async_copy(
            buf_ref.at[sl, pl.ds(lo, rows)], xs_ref.at[pl.ds(base, rows)], sem.at[sl])

    _for_each_run(tab_ref, tb, lambda *a: copy(slot)(*a).start())

    @pl.when(s > 0)
    def _():
        _for_each_run(tabp_ref, tb, lambda *a: copy(1 - slot)(*a).wait())

    @pl.when(s == nblk - 1)
    def _():
        _for_each_run(tab_ref, tb, lambda *a: copy(slot)(*a).wait())


def _dispatch(xn, tab, lp_rows, xs0, tb):
    nblk = tab.shape[0]
    return pl.pallas_call(
        functools.partial(_dispatch_body, nblk, tb),
        grid=(nblk,),
        in_specs=[pl.BlockSpec((1, 1, TAB_W), lambda s: (s, 0, 0), memory_space=pltpu.SMEM),
                  pl.BlockSpec((1, 1, TAB_W), lambda s: (jnp.maximum(s - 1, 0), 0, 0), memory_space=pltpu.SMEM),
                  pl.BlockSpec((1, 8, tb), lambda s: (s, 0, 0)),
                  pl.BlockSpec((tb, D), lambda s: (s, 0)),
                  pl.BlockSpec(memory_space=pl.ANY)],
        out_specs=pl.BlockSpec(memory_space=pl.ANY),
        out_shape=jax.ShapeDtypeStruct(xs0.shape, bf16),
        scratch_shapes=[pltpu.VMEM((2, _sorted_rows(tb), D), bf16), pltpu.SemaphoreType.DMA((2,))],
        input_output_aliases={4: 0},
        compiler_params=pltpu.CompilerParams(dimension_semantics=("arbitrary",), has_side_effects=True,
                                             vmem_limit_bytes=VMEM_LIMIT),
        name="moe_dispatch",
    )(tab, tab, lp_rows, xn, xs0)


def _experts_body(nf, sub, te_ref, tv_ref, x_ref, wg_ref, wu_ref, wd_ref, y_ref, acc_ref):
    del te_ref
    t = pl.program_id(0)
    f = pl.program_id(1)
    valid = tv_ref[t] == 1

    @pl.when(valid & (f == 0))
    def _():
        acc_ref[...] = jnp.zeros_like(acc_ref)

    @pl.when(valid)
    def _():
        x = x_ref[...]
        acc = acc_ref[...]
        for k in range(wg_ref.shape[1] // sub):
            g = _dot(x, wg_ref[:, k * sub:(k + 1) * sub])
            u = _dot(x, wu_ref[:, k * sub:(k + 1) * sub])
            a = (g * jax.nn.sigmoid(g) * u).astype(bf16)
            acc = acc + _dot(a, wd_ref[k * sub:(k + 1) * sub, :])
        acc_ref[...] = acc

    @pl.when(valid & (f == nf - 1))
    def _():
        y_ref[...] = acc_ref[...].astype(bf16)

    @pl.when(jnp.logical_not(valid) & (f == nf - 1))
    def _():
        y_ref[...] = jnp.zeros_like(y_ref)


def _experts(xs, te, tv, wg, wu, wd, tme, tf, sub):
    rows = xs.shape[0]
    T = rows // tme
    nf = D_FF_EXPERT // tf
    grid_spec = pltpu.PrefetchScalarGridSpec(
        num_scalar_prefetch=2,
        grid=(T, nf),
        in_specs=[pl.BlockSpec((tme, D), lambda t, f, te, tv: (t, 0)),
                  pl.BlockSpec((None, D, tf), lambda t, f, te, tv: (te[t], 0, f)),
                  pl.BlockSpec((None, D, tf), lambda t, f, te, tv: (te[t], 0, f)),
                  pl.BlockSpec((None, tf, D), lambda t, f, te, tv: (te[t], f, 0))],
        out_specs=pl.BlockSpec((tme, D), lambda t, f, te, tv: (t, 0)),
        scratch_shapes=[pltpu.VMEM((tme, D), f32)],
    )
    return pl.pallas_call(
        functools.partial(_experts_body, nf, sub),
        grid_spec=grid_spec,
        out_shape=jax.ShapeDtypeStruct((rows, D), bf16),
        compiler_params=_cparams(("arbitrary", "arbitrary")),
        name="moe_experts",
    )(te, tv, xs, wg, wu, wd)


def _combine_body(nblk, tb, nb, tab_ref, tabn_ref, aux_ref, h_ref, ys_ref, o_ref, buf_ref, obuf_ref, sem, osem):
    s = pl.program_id(0)
    slot = s % 2
    sub = tb // BLK

    def copy(sl):
        return lambda base, lo, rows: pltpu.make_async_copy(
            ys_ref.at[pl.ds(base, rows)], buf_ref.at[sl, pl.ds(lo, rows)], sem.at[sl])

    @pl.when(s == 0)
    def _():
        _for_each_run(tab_ref, tb, lambda *a: copy(slot)(*a).start())

    @pl.when(s + 1 < nblk)
    def _():
        _for_each_run(tabn_ref, tb, lambda *a: copy(1 - slot)(*a).start())

    _for_each_run(tab_ref, tb, lambda *a: copy(slot)(*a).wait())

    total = tab_ref[0, 0, TAB_LO + N_EXPERTS - 1] + tab_ref[0, 0, TAB_CNT + N_EXPERTS - 1]
    rows = lax.broadcasted_iota(i32, (buf_ref.shape[1], 1), 0)
    ys = jnp.where(rows < total, buf_ref[slot], jnp.zeros((), bf16))
    aux = aux_ref[...]
    lane = lax.broadcasted_iota(i32, (1, buf_ref.shape[1]), 1)
    w = (jnp.where(lane == aux[:, 0:1].astype(i32), aux[:, 2:3], 0.0)
         + jnp.where(lane == aux[:, 1:2].astype(i32), aux[:, 3:4], 0.0))
    obuf_ref[slot] = h_ref[...] + _dot(w.astype(bf16), ys)

    def out_copies(step, sl, fn):
        for q in range(sub):
            pb = step * sub + q

            @pl.when(pb % nb != 0)
            def _(pb=pb, q=q):
                orow = pl.multiple_of(((pb // nb) * (nb - 1) + pb % nb - 1) * BLK, BLK)
                fn(pltpu.make_async_copy(obuf_ref.at[sl, pl.ds(q * BLK, BLK)], o_ref.at[pl.ds(orow, BLK)],
                                         osem.at[sl]))

    out_copies(s, slot, lambda c: c.start())

    @pl.when(s > 0)
    def _():
        out_copies(s - 1, 1 - slot, lambda c: c.wait())

    @pl.when(s == nblk - 1)
    def _():
        out_copies(s, slot, lambda c: c.wait())


def _combine(h, aux, tab, ys, B, nb, tb):
    nblk = tab.shape[0]
    return pl.pallas_call(
        functools.partial(_combine_body, nblk, tb, nb),
        grid=(nblk,),
        in_specs=[pl.BlockSpec((1, 1, TAB_W), lambda s: (s, 0, 0), memory_space=pltpu.SMEM),
                  pl.BlockSpec((1, 1, TAB_W), lambda s: (jnp.minimum(s + 1, nblk - 1), 0, 0),
                               memory_space=pltpu.SMEM),
                  pl.BlockSpec((tb, LANES), lambda s: (s, 0)),
                  pl.BlockSpec((tb, D), lambda s: (s, 0)),
                  pl.BlockSpec(memory_space=pl.ANY)],
        out_specs=pl.BlockSpec(memory_space=pl.ANY),
        out_shape=jax.ShapeDtypeStruct((B * (nb - 1) * BLK, D), f32),
        scratch_shapes=[pltpu.VMEM((2, _sorted_rows(tb), D), bf16), pltpu.VMEM((2, tb, D), f32),
                        pltpu.SemaphoreType.DMA((2,)), pltpu.SemaphoreType.DMA((2,))],
        compiler_params=_cparams(("arbitrary",)),
        name="moe_combine",
    )(tab, tab, aux, h, ys)


def _routing_tables(info, cnt_rows, B, nb, tb, tme):
    n = info.shape[0]
    nblk = n // tb
    ar = jnp.arange(N_EXPERTS, dtype=i32)
    e1 = info[:, 0].astype(i32)
    e2 = info[:, 1].astype(i32)
    blk = jnp.arange(n // BLK, dtype=i32)[:, None]
    rowin = jnp.arange(BLK, dtype=i32)[None, :]
    real = ((blk % nb != 0) | (rowin >= PAD)).reshape(n)
    oh1 = (e1[:, None] == ar) & real[:, None]
    oh2 = (e2[:, None] == ar) & real[:, None]
    blk_cnt = (oh1 | oh2).astype(i32).reshape(nblk, tb, N_EXPERTS).sum(axis=1)
    blk_rank0 = jnp.cumsum(blk_cnt, axis=0) - blk_cnt
    run = ((blk_cnt + ROW_ALIGN - 1) // ROW_ALIGN) * ROW_ALIGN
    run0 = jnp.cumsum(run, axis=0) - run
    seg = ((jnp.sum(run, axis=0) + tme - 1) // tme) * tme
    ends = jnp.cumsum(seg)
    offs = ends - seg
    base = offs[None, :] + run0
    lo = jnp.cumsum(run, axis=1) - run
    tab = jnp.concatenate([base, lo, run, jnp.zeros_like(base)], axis=1).reshape(nblk, 1, TAB_W)

    shift = jnp.repeat(lo - blk_rank0, tb, axis=0)
    lp1 = jnp.where(real, jnp.sum(jnp.where(oh1, shift, 0), axis=1) + info[:, 4].astype(i32), -1)
    lp2 = jnp.where(real, jnp.sum(jnp.where(oh2, shift, 0), axis=1) + info[:, 5].astype(i32), -1)
    lp_rows = jnp.concatenate([lp1.reshape(nblk, 1, tb), lp2.reshape(nblk, 1, tb),
                               jnp.full((nblk, 6, tb), -1, i32)], axis=1)
    aux = jnp.concatenate([lp1.astype(f32)[:, None], lp2.astype(f32)[:, None], info[:, 2:4],
                           jnp.zeros((n, LANES - 4), f32)], axis=1)
    del cnt_rows
    return tab, lp_rows, aux, ends


def _strip_body(h_ref, o_ref):
    o_ref[...] = h_ref[...]


def _strip_meta(h, B, nb):
    nt = nb - 1
    return pl.pallas_call(
        _strip_body,
        grid=(B, nt),
        in_specs=[pl.BlockSpec((BLK, D), lambda b, i: (b * nb + i + 1, 0))],
        out_specs=pl.BlockSpec((BLK, D), lambda b, i: (b * nt + i, 0)),
        out_shape=jax.ShapeDtypeStruct((B * nt * BLK, D), f32),
        compiler_params=_cparams(("parallel", "arbitrary")),
        name="strip_meta",
    )(h)


def _bucket_lookup(table, bucket):
    tshape = (table.shape[1],) + (1,) * bucket.ndim
    out = jnp.zeros((table.shape[1],) + bucket.shape, f32)
    for k in range(N_BUCKETS):
        out = out + jnp.where((bucket == k)[None], table[k].astype(f32).reshape(tshape), 0.0)
    return out


def _bias_tables(rel, nb):
    table_a = rel[:, :A_HEADS] * LOG2E
    table_b = rel[:, A_HEADS:] * LOG2E
    kk = jnp.arange(2 * BLK)[:, None]
    qq = jnp.arange(BLK)[None, :]
    n = BLK + qq - kk
    in_win = (n >= 0) & (n < BLK)
    bias = _bucket_lookup(table_a, _t5_bucket(jnp.maximum(n, 0)))
    variants = []
    for blk in range(3):
        key_ok = (blk - 1) * BLK + kk >= PAD
        variants.append(jnp.where((in_win & key_ok)[None], bias, NEG))
    bias_at = jnp.stack(variants, axis=0)

    d = jnp.arange(nb + 1)[:, None, None]
    kb = jnp.arange(BLK)[None, :, None]
    qb = jnp.arange(BLK)[None, None, :]
    nn = jnp.where(d < nb, d * BLK + qb - kb, -1)
    bias_bt = _bucket_lookup(table_b, _t5_bucket(jnp.maximum(nn, 0)))
    bias_bt = jnp.where((nn >= 0)[None], bias_bt, NEG)
    return bias_at, bias_bt


def _block_diag_ones(gs):
    idx = np.arange(LANES) // gs
    return jnp.asarray((idx[:, None] == idx[None, :]).astype(np.float32), dtype=bf16)


def _col_scale(gain_a, gain_b):
    one = jnp.ones((256,), f32)
    cs = jnp.concatenate([one, jnp.tile(gain_b[1], 8), jnp.tile(gain_a[1], 2)]).reshape(1, D_ROW)
    cst = jnp.concatenate([jnp.tile(gain_a[0], 8) * (64 ** -0.5 * LOG2E),
                           jnp.tile(gain_b[0], 8) * (32 ** -0.5 * LOG2E)]).reshape(R_VB, 1)
    return cs.astype(f32), cst.astype(f32)


def _split_in_weight(w):
    main = jnp.concatenate([w[:, 1536:1792], w[:, 1024:1280], w[:, 512:640]], axis=1)
    tr = jnp.concatenate([w[:, 0:512], w[:, 768:1024], w[:, 1280:1536], w[:, 640:768]], axis=1).T
    return main.astype(bf16), tr.astype(bf16)


def _pool_weight(w_pool):
    out = jnp.zeros((256, 256), f32)
    for gi in range(4):
        out = out.at[gi * 64:(gi + 1) * 64, gi * 64:(gi + 1) * 64].set(w_pool[gi])
    return out.astype(bf16)


def _row_tile(n):
    for tm in (512, 256, 128):
        if n % tm == 0:
            return tm
    raise ValueError(n)


def kernel(x, meta_tokens, rel_bias_table, g_mix, w_in, qk_gain_a, sinks, qk_gain_b, diff_lambda, g_diff_out,
           w_pool, pool_scale, w_out, g_ffn, w_ffn_gate, w_ffn_up, w_ffn_down, w_router, w_exp_gate, w_exp_up,
           w_exp_down):
    B, seq, _ = x.shape
    assert seq % BLK == 0
    depth = g_mix.shape[0]
    nb = seq // BLK + 1
    L = nb * BLK
    n = B * L
    tm = _row_tile(n)
    tm_in = 2 * tm if n % (2 * tm) == 0 else tm
    tme = 1024
    tf_e = 1792
    tf_d = 256
    kb_d = 4

    bias_at, bias_bt = _bias_tables(rel_bias_table, nb)
    g64 = _block_diag_ones(64)
    g32 = _block_diag_ones(32)

    h = _embed(x, meta_tokens, B, nb)
    out = None
    for l in range(depth):
        lam_init = 0.8 - 0.6 * math.exp(-0.3 * l)
        cs, cst = _col_scale(qk_gain_a[l], qk_gain_b[l])
        w_main, w_tr = _split_in_weight(w_in[l])
        z, ft = _inproj(h, g_mix[l].reshape(1, D), w_main, w_tr, cs, cst, g64, g32, tm_in)
        oa, oc = _swa_pool(z, ft, bias_at, sinks[l], _pool_weight(w_pool[l]), pool_scale[l].reshape(1, 256), B, nb)
        ob = _diff_attn(ft, z.reshape(B, L, D_ROW), bias_bt, diff_lambda[l],
                        jnp.tile(g_diff_out[l], 2).reshape(BLK, 1), lam_init, B, nb, kb_d)
        wo = w_out[l].astype(bf16)
        gf = g_ffn[l].reshape(1, D)
        j = l // 2
        if l % 2 == 0:
            h = _outproj_ffn(oa, ob, oc, wo, h, gf, w_ffn_gate[j].astype(bf16), w_ffn_up[j].astype(bf16),
                             w_ffn_down[j].astype(bf16), tm, tf_d)
            out = None
        else:
            assert l == depth - 1, "the expert layer must be the last layer"
            wr = jnp.pad(w_router[j], ((0, 0), (0, LANES - N_EXPERTS)))
            wrh = wr.astype(bf16)
            wrl = (wr - wrh.astype(f32)).astype(bf16)
            tri = jnp.asarray(np.tril(np.ones((tm // 2, tm // 2), np.float32), -1), dtype=bf16)
            h, xn, info, cnt = _outproj_router(oa, ob, oc, wo, h, gf, tm, (wrh, wrl, tri), nb)

            tab, lp_rows, aux, ends = _routing_tables(info, cnt, B, nb, tm, tme)
            rows_max = B * (seq + N_META) * 2 + (n // tm) * N_EXPERTS * (ROW_ALIGN - 1) + N_EXPERTS * (tme - 1)
            T = -(-rows_max // tme)
            starts = jnp.arange(T, dtype=i32) * tme
            te = jnp.minimum(jnp.sum((ends[None, :] <= starts[:, None]).astype(i32), axis=1), N_EXPERTS - 1)
            tv = (starts < ends[-1]).astype(i32)

            xs = _dispatch(xn, tab, lp_rows, jnp.zeros((T * tme, D), bf16), tm)
            ys = _experts(xs, te, tv, w_exp_gate[j].astype(bf16), w_exp_up[j].astype(bf16),
                          w_exp_down[j].astype(bf16), tme, tf_e, 256)
            out = _combine(h, aux, tab, ys, B, nb, tm)
    if out is None:
        out = _strip_meta(h, B, nb)
    return out.reshape(B, seq, D)
```

```python
import functools
import math

import jax
import jax.numpy as jnp
import numpy as np
from jax import lax
from jax.experimental import pallas as pl
from jax.experimental.pallas import tpu as pltpu

f32 = jnp.float32
bf16 = jnp.bfloat16
i32 = jnp.int32

D = 1024
BLK = 128
N_META = 16
PAD = BLK - N_META
A_HEADS = 8
B_HEADS = 4
N_BUCKETS = 32
MAX_DISTANCE = 128
N_EXPERTS = 8
D_FF_EXPERT = 3584
EPS = 1e-6
NEG = -1e30
LANES = 128
D_ROW = 640
D_FM = 1152
R_QB, R_VB, R_VA = 512, 768, 1024
LOG2E = 1.4426950408889634

VMEM_LIMIT = 56 * 1024 * 1024


def _cparams(sem):
    return pltpu.CompilerParams(dimension_semantics=sem, vmem_limit_bytes=VMEM_LIMIT)


def _t5_bucket(n):
    max_exact = N_BUCKETS // 2
    large = max_exact
    for j in range(1, N_BUCKETS - max_exact):
        thr = math.ceil(max_exact * (MAX_DISTANCE / max_exact) ** (j / (N_BUCKETS - max_exact)))
        large = large + (n >= thr).astype(i32)
    return jnp.where(n < max_exact, n, large)


def _dot(a, b):
    return jnp.dot(a, b, preferred_element_type=f32)


def _dot_nt(a, b):
    return lax.dot_general(a, b, (((1,), (1,)), ((), ())), preferred_element_type=f32)


def _embed_body(x_ref, m_ref, o_ref):
    o_ref[0:PAD, :] = jnp.zeros((PAD, D), f32)
    o_ref[PAD:BLK, :] = m_ref[...]
    o_ref[BLK:, :] = x_ref[...]


def _embed(x, meta, B, nb):
    seq = x.shape[1]
    return pl.pallas_call(
        _embed_body,
        grid=(B,),
        in_specs=[pl.BlockSpec((None, seq, D), lambda b: (b, 0, 0)),
                  pl.BlockSpec((N_META, D), lambda b: (0, 0))],
        out_specs=pl.BlockSpec((nb * BLK, D), lambda b: (b, 0)),
        out_shape=jax.ShapeDtypeStruct((B * nb * BLK, D), f32),
        compiler_params=_cparams(("parallel",)),
        name="embed",
    )(x, meta)


def _split_bf16(x):
    hi = x.astype(bf16)
    return hi, (x - hi.astype(f32)).astype(bf16)


def _inproj_body(tm, h_ref, g_ref, w_ref, wt_ref, cs_ref, cst_ref, g64_ref, g32_ref, z_ref, ft_ref):
    sub = tm
    for r0 in range(0, tm, sub):
        rows = slice(r0, r0 + sub)
        x = h_ref[rows, :]
        ms = jnp.mean(x * x, axis=-1, keepdims=True)
        xn = (x * lax.rsqrt(ms + EPS) * g_ref[...]).astype(bf16)

        for c in range(D_ROW // LANES):
            zc = _dot(xn, w_ref[:, c * LANES:(c + 1) * LANES])
            if c >= 2:
                gm, inv = (g32_ref, 1.0 / 32) if c < 4 else (g64_ref, 1.0 / 64)
                ss = _dot((zc * zc).astype(bf16), gm[...])
                zc = zc * lax.rsqrt(ss * inv + EPS)
            z_ref[rows, c * LANES:(c + 1) * LANES] = (zc * cs_ref[:, c * LANES:(c + 1) * LANES]).astype(bf16)

        zt_all = _dot_nt(wt_ref[...], xn)
        for c in range(D_FM // LANES):
            zt = zt_all[c * LANES:(c + 1) * LANES]
            if c < R_VB // LANES:
                gm, inv = (g64_ref, 1.0 / 64) if c < R_QB // LANES else (g32_ref, 1.0 / 32)
                ss = _dot(gm[...], (zt * zt).astype(bf16))
                zt = zt * lax.rsqrt(ss * inv + EPS) * cst_ref[c * LANES:(c + 1) * LANES, :]
            for k in range(sub // BLK):
                ft_ref[r0 // BLK + k, c * LANES:(c + 1) * LANES, :] = zt[:, k * BLK:(k + 1) * BLK].astype(bf16)


def _inproj(h, g, w, wt, cs, cst, g64, g32, tm):
    n = h.shape[0]
    const = lambda i: (0, 0)
    kb = tm // BLK
    return pl.pallas_call(
        functools.partial(_inproj_body, tm),
        grid=(n // tm,),
        in_specs=[pl.BlockSpec((tm, D), lambda i: (i, 0)),
                  pl.BlockSpec((1, D), const),
                  pl.BlockSpec((D, D_ROW), const),
                  pl.BlockSpec((D_FM, D), const),
                  pl.BlockSpec((1, D_ROW), const),
                  pl.BlockSpec((R_VB, 1), const),
                  pl.BlockSpec((LANES, LANES), const),
                  pl.BlockSpec((LANES, LANES), const)],
        out_specs=[pl.BlockSpec((tm, D_ROW), lambda i: (i, 0)),
                   pl.BlockSpec((kb, D_FM, BLK), lambda i: (i, 0, 0))],
        out_shape=[jax.ShapeDtypeStruct((n, D_ROW), bf16),
                   jax.ShapeDtypeStruct((n // BLK, D_FM, BLK), bf16)],
        compiler_params=_cparams(("parallel",)),
        name="norm_inproj",
    )(h, g, w, wt, cs, cst, g64, g32)


def _swa_pool_body(lanes, *refs):
    ins = [refs[7 * s:7 * s + 7] for s in range(lanes)]
    bias_ref, sink_ref, wp_ref, ps_ref, oa_ref, oc_ref = refs[7 * lanes:7 * lanes + 6]
    i = pl.program_id(1)
    zeros = jnp.zeros((64, BLK), bf16)
    row = lax.broadcasted_iota(i32, (BLK, 1), 0)
    t = i * BLK + row - PAD
    lane2 = lax.broadcasted_iota(i32, (1, 256), 1)
    grp2 = lane2 // 64
    win = jnp.where(grp2 == 0, 2, jnp.where(grp2 == 1, 4, jnp.where(grp2 == 2, 8, 16)))
    cnt = jnp.maximum(jnp.minimum(t + 1, win), 1).astype(f32)
    for s in range(lanes):
        qt_ref, kp_ref, kc_ref, vtp_ref, vtc_ref, up_ref, uc_ref = ins[s]
        kcat = jnp.concatenate([kp_ref[...], kc_ref[...]], axis=0)
        vt = jnp.concatenate([vtp_ref[...], vtc_ref[...]], axis=1)
        for grp in range(2):
            pts, inv_ls = [], []
            for hh in range(4):
                h = 4 * grp + hh
                qrows = qt_ref[64 * h:64 * h + 64, :]
                qmt = jnp.concatenate([qrows, zeros] if grp == 0 else [zeros, qrows], axis=0)
                st = _dot(kcat, qmt) + bias_ref[h]
                sink = sink_ref[h] * LOG2E
                m = jnp.maximum(jnp.max(st, axis=0, keepdims=True), sink)
                p = jnp.exp2(st - m)
                inv_ls.append(1.0 / (jnp.sum(p, axis=0, keepdims=True) + jnp.exp2(sink - m)))
                pts.append(p.astype(bf16))
            ot = _dot(vt[64 * grp:64 * grp + 64, :], jnp.concatenate(pts, axis=1))
            for pair in range(2):
                o2 = jnp.concatenate([ot[:, (2 * pair + k) * BLK:(2 * pair + k + 1) * BLK] * inv_ls[2 * pair + k]
                                      for k in range(2)], axis=0)
                pb = 2 * grp + pair
                oa_ref[s, :, pb * LANES:(pb + 1) * LANES] = o2.T.astype(bf16)

        u_cur = jnp.where(t >= 0, uc_ref[...].astype(f32), 0.0)
        u_prev = jnp.where(i > 0, up_ref[BLK - 16:, :].astype(f32), 0.0)
        ext = jnp.concatenate([u_prev, u_cur], axis=0)
        s2 = ext + pltpu.roll(ext, 1, 0)
        s4 = s2 + pltpu.roll(s2, 2, 0)
        s8 = s4 + pltpu.roll(s4, 4, 0)
        s16 = s8 + pltpu.roll(s8, 8, 0)
        sums = jnp.where(grp2 == 0, s2, jnp.where(grp2 == 1, s4, jnp.where(grp2 == 2, s8, s16)))[16:, :]
        dlt = (sums / cnt - u_cur).astype(bf16)
        oc_ref[s] = (_dot(dlt, wp_ref[...]) * ps_ref[...]).astype(bf16)


def _swa_pool(z, ft, bias_at, sinks, wpool_bd, pool_scale, B, nb):
    L = nb * BLK
    lanes = next(k for k in (4, 2, 1) if B % k == 0)
    in_specs, operands = [], []
    for s in range(lanes):
        cur = lambda c, s=s: (lambda g, i: ((lanes * g + s) * nb + i, c))
        prev = lambda c, s=s: (lambda g, i: ((lanes * g + s) * nb + jnp.maximum(i - 1, 0), c))
        cur3 = lambda r, s=s: (lambda g, i: ((lanes * g + s) * nb + i, r, 0))
        prev3 = lambda r, s=s: (lambda g, i: ((lanes * g + s) * nb + jnp.maximum(i - 1, 0), r, 0))
        in_specs += [pl.BlockSpec((None, 512, BLK), cur3(0)),
                     pl.BlockSpec((BLK, LANES), prev(4)),
                     pl.BlockSpec((BLK, LANES), cur(4)),
                     pl.BlockSpec((None, BLK, BLK), prev3(R_VA // BLK)),
                     pl.BlockSpec((None, BLK, BLK), cur3(R_VA // BLK)),
                     pl.BlockSpec((BLK, 256), prev(0)),
                     pl.BlockSpec((BLK, 256), cur(0))]
        operands += [ft, z, z, ft, ft, z, z]
    in_specs += [pl.BlockSpec((None, A_HEADS, 2 * BLK, BLK), lambda g, i: (jnp.minimum(i, 2), 0, 0, 0)),
                 pl.BlockSpec(memory_space=pltpu.SMEM),
                 pl.BlockSpec((256, 256), lambda g, i: (0, 0)),
                 pl.BlockSpec((1, 256), lambda g, i: (0, 0))]
    operands += [bias_at, sinks, wpool_bd, pool_scale]
    oa, oc = pl.pallas_call(
        functools.partial(_swa_pool_body, lanes),
        grid=(B // lanes, nb),
        in_specs=in_specs,
        out_specs=[pl.BlockSpec((None, lanes, BLK, 512), lambda g, i: (g, 0, i, 0)),
                   pl.BlockSpec((None, lanes, BLK, 256), lambda g, i: (g, 0, i, 0))],
        out_shape=[jax.ShapeDtypeStruct((B // lanes, lanes, L, 512), bf16),
                   jax.ShapeDtypeStruct((B // lanes, lanes, L, 256), bf16)],
        compiler_params=_cparams(("parallel", "arbitrary")),
        name="swa_pool",
    )(*operands)
    return oa.reshape(B * L, 512), oc.reshape(B * L, 256)


def _diff_attn_body(lam_init, nb, kb_step, lanes, *refs):
    ins = [refs[3 * s:3 * s + 3] for s in range(lanes)]
    bias_ref, dl_ref, go_ref = refs[3 * lanes:3 * lanes + 3]
    o_ref = refs[3 * lanes + 3]
    scr = [refs[3 * lanes + 4 + 7 * s:3 * lanes + 4 + 7 * s + 7] for s in range(lanes)]
    i = pl.program_id(1)
    row = lax.broadcasted_iota(i32, (BLK, 1), 0)
    zero = jnp.zeros((), bf16)
    rows_k = lax.broadcasted_iota(i32, (kb_step * BLK, 1), 0)
    padmask = jnp.where(rows_k < PAD, NEG, 0.0).astype(f32)

    qcat = []
    for s in range(lanes):
        qt_ref = ins[s][0]
        per_lane = []
        for p in range(2):
            qp = qt_ref[p * BLK:(p + 1) * BLK, :]
            per_lane.append([jnp.concatenate(
                [jnp.where((row >= 64 * e + 32 * c) & (row < 64 * e + 32 * c + 32), qp, zero) for c in range(2)],
                axis=1) for e in range(2)])
        qcat.append(per_lane)
        acc_ref, m_ref, l_ref = scr[s][0:3]
        m_ref[...] = jnp.full(m_ref.shape, 3 * NEG, f32)
        l_ref[...] = jnp.zeros(l_ref.shape, f32)
        acc_ref[...] = jnp.zeros(acc_ref.shape, f32)

    def blocks(st):
        j0 = st * kb_step
        return [jnp.minimum(j0 + kb, nb - 1) for kb in range(kb_step)], j0

    def scores(s, st, first, par):
        k_ref = ins[s][1]
        s_ref, x_ref = scr[s][3 + par], scr[s][5 + par]
        blks, j0 = blocks(st)
        dix = [jnp.where(i - (j0 + kb) >= 0, i - (j0 + kb), nb) for kb in range(kb_step)]
        kparts = [k_ref[pl.ds(pl.multiple_of(b * BLK, BLK), BLK), :] for b in blks]
        for p in range(2):
            kcat = jnp.concatenate([kp[:, p * BLK:(p + 1) * BLK] for kp in kparts], axis=0)
            for e in range(2):
                h = 2 * p + e
                st2 = _dot(kcat, qcat[s][p][e])
                bias = jnp.concatenate([bias_ref[h, dd] for dd in dix], axis=0)
                if first:
                    bias = bias + padmask
                for c in range(2):
                    tile = st2[:, c * BLK:(c + 1) * BLK] + bias
                    s_ref[2 * h + c] = tile
                    x_ref[2 * h + c:2 * h + c + 1, :] = jnp.max(tile, axis=0, keepdims=True)

    def consume(s, st, par):
        vt_ref = ins[s][2]
        acc_ref, m_ref, l_ref = scr[s][0:3]
        s_ref, x_ref = scr[s][3 + par], scr[s][5 + par]
        blks, _ = blocks(st)
        vparts = [vt_ref[b] for b in blks]
        for p in range(2):
            vtc = jnp.concatenate([vp[p * BLK:(p + 1) * BLK, :] for vp in vparts], axis=1)
            for c in range(2):
                for e in range(2):
                    idx = 2 * (2 * p + e) + c
                    m_old = m_ref[idx:idx + 1, :]
                    m_new = jnp.maximum(m_old, x_ref[idx:idx + 1, :])
                    alpha = jnp.exp2(m_old - m_new)
                    pt = jnp.exp2(s_ref[idx] - m_new)
                    l_ref[idx:idx + 1, :] = alpha * l_ref[idx:idx + 1, :] + jnp.sum(pt, axis=0, keepdims=True)
                    m_ref[idx:idx + 1, :] = m_new
                    rows = slice(64 * e, 64 * e + 64)
                    upd = _dot(vtc[rows, :], pt.astype(bf16))
                    acc_ref[2 * p + c, rows, :] = acc_ref[2 * p + c, rows, :] * alpha + upd

    n_steps = (i + kb_step) // kb_step
    for s in range(lanes):
        scores(s, 0, True, 0)

    def body(st, carry):
        for par in range(2):
            @pl.when(st % 2 == par)
            def _(par=par):
                for s in range(lanes):
                    scores(s, st + 1, False, 1 - par)
                for s in range(lanes):
                    consume(s, st, par)

        return carry

    lax.fori_loop(0, n_steps - 1, body, 0)
    last = n_steps - 1
    for par in range(2):
        @pl.when(last % 2 == par)
        def _(par=par):
            for s in range(lanes):
                consume(s, last, par)

    dl = dl_ref[...]
    lam = (jnp.exp(jnp.sum(dl[0:1] * dl[1:2], axis=-1, keepdims=True))
           - jnp.exp(jnp.sum(dl[2:3] * dl[3:4], axis=-1, keepdims=True)) + lam_init)
    for s in range(lanes):
        acc_ref, _, l_ref = scr[s][0:3]
        for p in range(2):
            parts = []
            for e in range(2):
                h = 2 * p + e
                rows = slice(64 * e, 64 * e + 64)
                o = (acc_ref[2 * p, rows, :] / l_ref[2 * h:2 * h + 1, :]
                     - lam * (acc_ref[2 * p + 1, rows, :] / l_ref[2 * h + 1:2 * h + 2, :]))
                ssq = jnp.sum(o * o, axis=0, keepdims=True)
                parts.append(o * lax.rsqrt(ssq * (1.0 / 64) + EPS))
            o = jnp.concatenate(parts, axis=0) * go_ref[...] * (1.0 - lam_init)
            o_ref[s, :, p * BLK:(p + 1) * BLK] = o.T.astype(bf16)


def _diff_attn(ft, z3, bias_bt, dl, go, lam_init, B, nb, kb_step):
    L = nb * BLK
    lanes = next(k for k in (4, 2, 1) if B % k == 0)
    in_specs, scratch = [], []
    for s in range(lanes):
        bat = lambda g, s=s: lanes * g + s
        in_specs += [pl.BlockSpec((None, 256, BLK), lambda g, i, bat=bat: (bat(g) * nb + i, R_QB // 256, 0)),
                     pl.BlockSpec((None, L, 256), lambda g, i, bat=bat: (bat(g), 0, 1)),
                     pl.BlockSpec((nb, 256, BLK), lambda g, i, bat=bat: (bat(g), R_VB // 256, 0))]
        scratch += [pltpu.VMEM((4, BLK, BLK), f32), pltpu.VMEM((8, BLK), f32), pltpu.VMEM((8, BLK), f32),
                    pltpu.VMEM((8, kb_step * BLK, BLK), f32), pltpu.VMEM((8, kb_step * BLK, BLK), f32),
                    pltpu.VMEM((8, BLK), f32), pltpu.VMEM((8, BLK), f32)]
    in_specs += [pl.BlockSpec((B_HEADS, nb + 1, BLK, BLK), lambda g, i: (0, 0, 0, 0),
                              pipeline_mode=pl.Buffered(1)),
                 pl.BlockSpec((4, 32), lambda g, i: (0, 0)),
                 pl.BlockSpec((BLK, 1), lambda g, i: (0, 0))]
    operands = [ft, z3, ft] * lanes + [bias_bt, dl, go]
    return pl.pallas_call(
        functools.partial(_diff_attn_body, lam_init, nb, kb_step, lanes),
        grid=(B // lanes, nb),
        in_specs=in_specs,
        out_specs=pl.BlockSpec((None, lanes, BLK, 256), lambda g, i: (g, 0, i, 0)),
        out_shape=jax.ShapeDtypeStruct((B // lanes, lanes, L, 256), bf16),
        scratch_shapes=scratch,
        compiler_params=_cparams(("parallel", "arbitrary")),
        name="diff_attn",
    )(*operands).reshape(B * L, 256)


def _outproj_common(oa_ref, ob_ref, oc_ref, w_ref, h_ref, g_ref, rows=slice(None)):
    y = (_dot(oa_ref[rows, :], w_ref[0:512, :]) + _dot(ob_ref[rows, :], w_ref[512:768, :])
         + _dot(oc_ref[rows, :], w_ref[768:1024, :]))
    hn = h_ref[rows, :] + y
    ms = jnp.mean(hn * hn, axis=-1, keepdims=True)
    return hn, hn * lax.rsqrt(ms + EPS) * g_ref[...]


def _outproj_ffn_body(tf, oa_ref, ob_ref, oc_ref, w_ref, h_ref, g_ref, wg_ref, wu_ref, wd_ref, o_ref):
    hn, xn = _outproj_common(oa_ref, ob_ref, oc_ref, w_ref, h_ref, g_ref)
    x = xn.astype(bf16)
    acc = hn
    for c in range(wg_ref.shape[1] // tf):
        g = _dot(x, wg_ref[:, c * tf:(c + 1) * tf])
        u = _dot(x, wu_ref[:, c * tf:(c + 1) * tf])
        a = (g * jax.nn.sigmoid(g) * u).astype(bf16)
        acc = acc + _dot(a, wd_ref[c * tf:(c + 1) * tf, :])
    o_ref[...] = acc


def _outproj_router_body(nb, tm, oa_ref, ob_ref, oc_ref, w_ref, h_ref, g_ref, wrh_ref, wrl_ref, tri_ref,
                         ho_ref, xn_ref, info_ref, cnt_ref):
    step = pl.program_id(0)
    th = tri_ref.shape[0]
    lane = lax.broadcasted_iota(i32, (1, LANES), 1)
    rowin = lax.broadcasted_iota(i32, (BLK, 1), 0)
    ninf = -jnp.inf

    @pl.when(step == 0)
    def _():
        cnt_ref[...] = jnp.zeros_like(cnt_ref)

    base = cnt_ref[...]
    for r0 in range(0, tm, th):
        rows = slice(r0, r0 + th)
        hn, xn = _outproj_common(oa_ref, ob_ref, oc_ref, w_ref, h_ref, g_ref, rows)
        ho_ref[rows, :] = hn
        xn_ref[rows, :] = xn.astype(bf16)

        hi, lo = _split_bf16(xn)
        logits = _dot(hi, wrh_ref[...]) + _dot(lo, wrh_ref[...]) + _dot(hi, wrl_ref[...])
        lg = jnp.where(lane < N_EXPERTS, logits, ninf)
        m1 = jnp.max(lg, axis=-1, keepdims=True)
        i1 = jnp.min(jnp.where(lg == m1, lane, LANES), axis=-1, keepdims=True)
        lg2 = jnp.where(lane == i1, ninf, lg)
        m2 = jnp.max(lg2, axis=-1, keepdims=True)
        i2 = jnp.min(jnp.where(lg2 == m2, lane, LANES), axis=-1, keepdims=True)
        e21 = jnp.exp(m2 - m1)
        g1 = 1.0 / (1.0 + e21)
        g2 = e21 * g1

        valids = []
        for k in range(th // BLK):
            blk = step * (tm // BLK) + r0 // BLK + k
            valids.append(jnp.logical_or(blk % nb != 0, rowin >= PAD))
        valid = jnp.concatenate(valids, axis=0)
        oh = jnp.where(valid & ((lane == i1) | (lane == i2)), 1.0, 0.0)
        rank = _dot(tri_ref[...], oh.astype(bf16)) + base
        r1 = jnp.sum(jnp.where(lane == i1, rank, 0.0), axis=-1, keepdims=True)
        r2 = jnp.sum(jnp.where(lane == i2, rank, 0.0), axis=-1, keepdims=True)
        base = base + jnp.sum(oh, axis=0, keepdims=True)
        info_ref[rows, :] = jnp.where(lane == 0, i1.astype(f32),
                            jnp.where(lane == 1, i2.astype(f32),
                            jnp.where(lane == 2, g1,
                            jnp.where(lane == 3, g2,
                            jnp.where(lane == 4, r1,
                            jnp.where(lane == 5, r2, 0.0))))))
    cnt_ref[...] = base


def _outproj_ffn(oa, ob, oc, w, h, g, wg, wu, wd, tm, tf):
    n = h.shape[0]
    F = wg.shape[1]
    row = lambda c: pl.BlockSpec((tm, c), lambda i: (i, 0))
    const = lambda r, c: pl.BlockSpec((r, c), lambda i: (0, 0), pipeline_mode=pl.Buffered(1))
    return pl.pallas_call(
        functools.partial(_outproj_ffn_body, tf),
        grid=(n // tm,),
        in_specs=[row(512), row(256), row(256), const(D, D), row(D), const(1, D),
                  const(D, F), const(D, F), const(F, D)],
        out_specs=row(D),
        out_shape=jax.ShapeDtypeStruct((n, D), f32),
        compiler_params=_cparams(("parallel",)),
        name="outproj_ffn",
    )(oa, ob, oc, w, h, g, wg, wu, wd)


def _outproj_router(oa, ob, oc, w, h, g, tm, router, nb):
    n = h.shape[0]
    row = lambda c: pl.BlockSpec((tm, c), lambda i: (i, 0))
    const = lambda r, c: pl.BlockSpec((r, c), lambda i: (0, 0))
    in_specs = [row(512), row(256), row(256), const(D, D), row(D), const(1, D)]
    wrh, wrl, tri = router
    return pl.pallas_call(
        functools.partial(_outproj_router_body, nb, tm),
        grid=(n // tm,),
        in_specs=in_specs + [const(D, LANES), const(D, LANES), const(*tri.shape)],
        out_specs=[row(D), row(D), row(LANES), const(1, LANES)],
        out_shape=[jax.ShapeDtypeStruct((n, D), f32), jax.ShapeDtypeStruct((n, D), bf16),
                   jax.ShapeDtypeStruct((n, LANES), f32), jax.ShapeDtypeStruct((1, LANES), f32)],
        compiler_params=_cparams(("arbitrary",)),
        name="outproj_router",
    )(oa, ob, oc, w, h, g, wrh, wrl, tri)


ROW_ALIGN = 16
TAB_BASE, TAB_LO, TAB_CNT, TAB_W = 0, N_EXPERTS, 2 * N_EXPERTS, 4 * N_EXPERTS


def _sorted_rows(tb):
    return -(-(2 * tb + N_EXPERTS * (ROW_ALIGN - 1)) // LANES) * LANES


def _for_each_run(tab_ref, tb, fn):
    bits = []
    b = pl.next_power_of_2(2 * tb)
    while b >= ROW_ALIGN:
        bits.append(b)
        b //= 2
    for e in range(N_EXPERTS):
        base = tab_ref[0, 0, TAB_BASE + e]
        lo = tab_ref[0, 0, TAB_LO + e]
        cnt = tab_ref[0, 0, TAB_CNT + e]
        off = 0
        for bit in bits:
            piece = cnt & bit

            @pl.when(piece != 0)
            def _(base=base, lo=lo, off=off, bit=bit):
                fn(pl.multiple_of(base + off, ROW_ALIGN), pl.multiple_of(lo + off, ROW_ALIGN), bit)

            off = off + piece


def _dispatch_body(nblk, tb, tab_ref, tabp_ref, lp_ref, x_ref, xs_in_ref, xs_ref, buf_ref, sem):
    del xs_in_ref
    s = pl.program_id(0)
    slot = s % 2
    lp = lp_ref[0]
    r = lax.broadcasted_iota(i32, (buf_ref.shape[1], 1), 0)
    perm = jnp.where((lp[0:1, :] == r) | (lp[1:2, :] == r), 1.0, 0.0).astype(bf16)
    buf_ref[slot] = _dot(perm, x_ref[...]).astype(bf16)

    def copy(sl):
        return lambda base, lo, rows: pltpu.make_async_copy(
            buf_ref.at[sl, pl.ds(lo, rows)], xs_ref.at[pl.ds(base, rows)], sem.at[sl])

    _for_each_run(tab_ref, tb, lambda *a: copy(slot)(*a).start())

    @pl.when(s > 0)
    def _():
        _for_each_run(tabp_ref, tb, lambda *a: copy(1 - slot)(*a).wait())

    @pl.when(s == nblk - 1)
    def _():
        _for_each_run(tab_ref, tb, lambda *a: copy(slot)(*a).wait())


def _dispatch(xn, tab, lp_rows, xs0, tb):
    nblk = tab.shape[0]
    return pl.pallas_call(
        functools.partial(_dispatch_body, nblk, tb),
        grid=(nblk,),
        in_specs=[pl.BlockSpec((1, 1, TAB_W), lambda s: (s, 0, 0), memory_space=pltpu.SMEM),
                  pl.BlockSpec((1, 1, TAB_W), lambda s: (jnp.maximum(s - 1, 0), 0, 0), memory_space=pltpu.SMEM),
                  pl.BlockSpec((1, 8, tb), lambda s: (s, 0, 0)),
                  pl.BlockSpec((tb, D), lambda s: (s, 0)),
                  pl.BlockSpec(memory_space=pl.ANY)],
        out_specs=pl.BlockSpec(memory_space=pl.ANY),
        out_shape=jax.ShapeDtypeStruct(xs0.shape, bf16),
        scratch_shapes=[pltpu.VMEM((2, _sorted_rows(tb), D), bf16), pltpu.SemaphoreType.DMA((2,))],
        input_output_aliases={4: 0},
        compiler_params=pltpu.CompilerParams(dimension_semantics=("arbitrary",), has_side_effects=True,
                                             vmem_limit_bytes=VMEM_LIMIT),
        name="moe_dispatch",
    )(tab, tab, lp_rows, xn, xs0)


def _experts_body(nf, sub, te_ref, tv_ref, x_ref, wg_ref, wu_ref, wd_ref, y_ref, acc_ref):
    del te_ref
    t = pl.program_id(0)
    f = pl.program_id(1)
    valid = tv_ref[t] == 1

    @pl.when(valid & (f == 0))
    def _():
        acc_ref[...] = jnp.zeros_like(acc_ref)

    @pl.when(valid)
    def _():
        x = x_ref[...]
        acc = acc_ref[...]
        for k in range(wg_ref.shape[1] // sub):
            g = _dot(x, wg_ref[:, k * sub:(k + 1) * sub])
            u = _dot(x, wu_ref[:, k * sub:(k + 1) * sub])
            a = (g * jax.nn.sigmoid(g) * u).astype(bf16)
            acc = acc + _dot(a, wd_ref[k * sub:(k + 1) * sub, :])
        acc_ref[...] = acc

    @pl.when(valid & (f == nf - 1))
    def _():
        y_ref[...] = acc_ref[...].astype(bf16)

    @pl.when(jnp.logical_not(valid) & (f == nf - 1))
    def _():
        y_ref[...] = jnp.zeros_like(y_ref)


def _experts(xs, te, tv, wg, wu, wd, tme, tf, sub):
    rows = xs.shape[0]
    T = rows // tme
    nf = D_FF_EXPERT // tf
    grid_spec = pltpu.PrefetchScalarGridSpec(
        num_scalar_prefetch=2,
        grid=(T, nf),
        in_specs=[pl.BlockSpec((tme, D), lambda t, f, te, tv: (t, 0)),
                  pl.BlockSpec((None, D, tf), lambda t, f, te, tv: (te[t], 0, f)),
                  pl.BlockSpec((None, D, tf), lambda t, f, te, tv: (te[t], 0, f)),
                  pl.BlockSpec((None, tf, D), lambda t, f, te, tv: (te[t], f, 0))],
        out_specs=pl.BlockSpec((tme, D), lambda t, f, te, tv: (t, 0)),
        scratch_shapes=[pltpu.VMEM((tme, D), f32)],
    )
    return pl.pallas_call(
        functools.partial(_experts_body, nf, sub),
        grid_spec=grid_spec,
        out_shape=jax.ShapeDtypeStruct((rows, D), bf16),
        compiler_params=_cparams(("arbitrary", "arbitrary")),
        name="moe_experts",
    )(te, tv, xs, wg, wu, wd)


def _combine_body(nblk, tb, nb, tab_ref, tabn_ref, aux_ref, h_ref, ys_ref, o_ref, buf_ref, obuf_ref, sem, osem):
    s = pl.program_id(0)
    slot = s % 2
    sub = tb // BLK

    def copy(sl):
        return lambda base, lo, rows: pltpu.make_async_copy(
            ys_ref.at[pl.ds(base, rows)], buf_ref.at[sl, pl.ds(lo, rows)], sem.at[sl])

    @pl.when(s == 0)
    def _():
        _for_each_run(tab_ref, tb, lambda *a: copy(slot)(*a).start())

    @pl.when(s + 1 < nblk)
    def _():
        _for_each_run(tabn_ref, tb, lambda *a: copy(1 - slot)(*a).start())

    _for_each_run(tab_ref, tb, lambda *a: copy(slot)(*a).wait())

    total = tab_ref[0, 0, TAB_LO + N_EXPERTS - 1] + tab_ref[0, 0, TAB_CNT + N_EXPERTS - 1]
    rows = lax.broadcasted_iota(i32, (buf_ref.shape[1], 1), 0)
    ys = jnp.where(rows < total, buf_ref[slot], jnp.zeros((), bf16))
    aux = aux_ref[...]
    lane = lax.broadcasted_iota(i32, (1, buf_ref.shape[1]), 1)
    w = (jnp.where(lane == aux[:, 0:1].astype(i32), aux[:, 2:3], 0.0)
         + jnp.where(lane == aux[:, 1:2].astype(i32), aux[:, 3:4], 0.0))
    obuf_ref[slot] = h_ref[...] + _dot(w.astype(bf16), ys)

    def out_copies(step, sl, fn):
        for q in range(sub):
            pb = step * sub + q

            @pl.when(pb % nb != 0)
            def _(pb=pb, q=q):
                orow = pl.multiple_of(((pb // nb) * (nb - 1) + pb % nb - 1) * BLK, BLK)
                fn(pltpu.make_async_copy(obuf_ref.at[sl, pl.ds(q * BLK, BLK)], o_ref.at[pl.ds(orow, BLK)],
                                         osem.at[sl]))

    out_copies(s, slot, lambda c: c.start())

    @pl.when(s > 0)
    def _():
        out_copies(s - 1, 1 - slot, lambda c: c.wait())

    @pl.when(s == nblk - 1)
    def _():
        out_copies(s, slot, lambda c: c.wait())


def _combine(h, aux, tab, ys, B, nb, tb):
    nblk = tab.shape[0]
    return pl.pallas_call(
        functools.partial(_combine_body, nblk, tb, nb),
        grid=(nblk,),
        in_specs=[pl.BlockSpec((1, 1, TAB_W), lambda s: (s, 0, 0), memory_space=pltpu.SMEM),
                  pl.BlockSpec((1, 1, TAB_W), lambda s: (jnp.minimum(s + 1, nblk - 1), 0, 0),
                               memory_space=pltpu.SMEM),
                  pl.BlockSpec((tb, LANES), lambda s: (s, 0)),
                  pl.BlockSpec((tb, D), lambda s: (s, 0)),
                  pl.BlockSpec(memory_space=pl.ANY)],
        out_specs=pl.BlockSpec(memory_space=pl.ANY),
        out_shape=jax.ShapeDtypeStruct((B * (nb - 1) * BLK, D), f32),
        scratch_shapes=[pltpu.VMEM((2, _sorted_rows(tb), D), bf16), pltpu.VMEM((2, tb, D), f32),
                        pltpu.SemaphoreType.DMA((2,)), pltpu.SemaphoreType.DMA((2,))],
        compiler_params=_cparams(("arbitrary",)),
        name="moe_combine",
    )(tab, tab, aux, h, ys)


def _routing_tables(info, cnt_rows, B, nb, tb, tme):
    n = info.shape[0]
    nblk = n // tb
    ar = jnp.arange(N_EXPERTS, dtype=i32)
    e1 = info[:, 0].astype(i32)
    e2 = info[:, 1].astype(i32)
    blk = jnp.arange(n // BLK, dtype=i32)[:, None]
    rowin = jnp.arange(BLK, dtype=i32)[None, :]
    real = ((blk % nb != 0) | (rowin >= PAD)).reshape(n)
    oh1 = (e1[:, None] == ar) & real[:, None]
    oh2 = (e2[:, None] == ar) & real[:, None]
    blk_cnt = (oh1 | oh2).astype(i32).reshape(nblk, tb, N_EXPERTS).sum(axis=1)
    blk_rank0 = jnp.cumsum(blk_cnt, axis=0) - blk_cnt
    run = ((blk_cnt + ROW_ALIGN - 1) // ROW_ALIGN) * ROW_ALIGN
    run0 = jnp.cumsum(run, axis=0) - run
    seg = ((jnp.sum(run, axis=0) + tme - 1) // tme) * tme
    ends = jnp.cumsum(seg)
    offs = ends - seg
    base = offs[None, :] + run0
    lo = jnp.cumsum(run, axis=1) - run
    tab = jnp.concatenate([base, lo, run, jnp.zeros_like(base)], axis=1).reshape(nblk, 1, TAB_W)

    shift = jnp.repeat(lo - blk_rank0, tb, axis=0)
    lp1 = jnp.where(real, jnp.sum(jnp.where(oh1, shift, 0), axis=1) + info[:, 4].astype(i32), -1)
    lp2 = jnp.where(real, jnp.sum(jnp.where(oh2, shift, 0), axis=1) + info[:, 5].astype(i32), -1)
    lp_rows = jnp.concatenate([lp1.reshape(nblk, 1, tb), lp2.reshape(nblk, 1, tb),
                               jnp.full((nblk, 6, tb), -1, i32)], axis=1)
    aux = jnp.concatenate([lp1.astype(f32)[:, None], lp2.astype(f32)[:, None], info[:, 2:4],
                           jnp.zeros((n, LANES - 4), f32)], axis=1)
    del cnt_rows
    return tab, lp_rows, aux, ends


def _strip_body(h_ref, o_ref):
    o_ref[...] = h_ref[...]


def _strip_meta(h, B, nb):
    nt = nb - 1
    return pl.pallas_call(
        _strip_body,
        grid=(B, nt),
        in_specs=[pl.BlockSpec((BLK, D), lambda b, i: (b * nb + i + 1, 0))],
        out_specs=pl.BlockSpec((BLK, D), lambda b, i: (b * nt + i, 0)),
        out_shape=jax.ShapeDtypeStruct((B * nt * BLK, D), f32),
        compiler_params=_cparams(("parallel", "arbitrary")),
        name="strip_meta",
    )(h)


def _bucket_lookup(table, bucket):
    tshape = (table.shape[1],) + (1,) * bucket.ndim
    out = jnp.zeros((table.shape[1],) + bucket.shape, f32)
    for k in range(N_BUCKETS):
        out = out + jnp.where((bucket == k)[None], table[k].astype(f32).reshape(tshape), 0.0)
    return out


def _bias_tables(rel, nb):
    table_a = rel[:, :A_HEADS] * LOG2E
    table_b = rel[:, A_HEADS:] * LOG2E
    kk = jnp.arange(2 * BLK)[:, None]
    qq = jnp.arange(BLK)[None, :]
    n = BLK + qq - kk
    in_win = (n >= 0) & (n < BLK)
    bias = _bucket_lookup(table_a, _t5_bucket(jnp.maximum(n, 0)))
    variants = []
    for blk in range(3):
        key_ok = (blk - 1) * BLK + kk >= PAD
        variants.append(jnp.where((in_win & key_ok)[None], bias, NEG))
    bias_at = jnp.stack(variants, axis=0)

    d = jnp.arange(nb + 1)[:, None, None]
    kb = jnp.arange(BLK)[None, :, None]
    qb = jnp.arange(BLK)[None, None, :]
    nn = jnp.where(d < nb, d * BLK + qb - kb, -1)
    bias_bt = _bucket_lookup(table_b, _t5_bucket(jnp.maximum(nn, 0)))
    bias_bt = jnp.where((nn >= 0)[None], bias_bt, NEG)
    return bias_at, bias_bt


def _block_diag_ones(gs):
    idx = np.arange(LANES) // gs
    return jnp.asarray((idx[:, None] == idx[None, :]).astype(np.float32), dtype=bf16)


def _col_scale(gain_a, gain_b):
    one = jnp.ones((256,), f32)
    cs = jnp.concatenate([one, jnp.tile(gain_b[1], 8), jnp.tile(gain_a[1], 2)]).reshape(1, D_ROW)
    cst = jnp.concatenate([jnp.tile(gain_a[0], 8) * (64 ** -0.5 * LOG2E),
                           jnp.tile(gain_b[0], 8) * (32 ** -0.5 * LOG2E)]).reshape(R_VB, 1)
    return cs.astype(f32), cst.astype(f32)


def _split_in_weight(w):
    main = jnp.concatenate([w[:, 1536:1792], w[:, 1024:1280], w[:, 512:640]], axis=1)
    tr = jnp.concatenate([w[:, 0:512], w[:, 768:1024], w[:, 1280:1536], w[:, 640:768]], axis=1).T
    return main.astype(bf16), tr.astype(bf16)


def _pool_weight(w_pool):
    out = jnp.zeros((256, 256), f32)
    for gi in range(4):
        out = out.at[gi * 64:(gi + 1) * 64, gi * 64:(gi + 1) * 64].set(w_pool[gi])
    return out.astype(bf16)


def _row_tile(n):
    for tm in (512, 256, 128):
        if n % tm == 0:
            return tm
    raise ValueError(n)


def kernel(x, meta_tokens, rel_bias_table, g_mix, w_in, qk_gain_a, sinks, qk_gain_b, diff_lambda, g_diff_out,
           w_pool, pool_scale, w_out, g_ffn, w_ffn_gate, w_ffn_up, w_ffn_down, w_router, w_exp_gate, w_exp_up,
           w_exp_down):
    B, seq, _ = x.shape
    assert seq % BLK == 0
    depth = g_mix.shape[0]
    nb = seq // BLK + 1
    L = nb * BLK
    n = B * L
    tm = _row_tile(n)
    tm_in = 2 * tm if n % (2 * tm) == 0 else tm
    tme = 1024
    tf_e = 1792
    tf_d = 256
    kb_d = 2

    bias_at, bias_bt = _bias_tables(rel_bias_table, nb)
    g64 = _block_diag_ones(64)
    g32 = _block_diag_ones(32)

    h = _embed(x, meta_tokens, B, nb)
    out = None
    for l in range(depth):
        lam_init = 0.8 - 0.6 * math.exp(-0.3 * l)
        cs, cst = _col_scale(qk_gain_a[l], qk_gain_b[l])
        w_main, w_tr = _split_in_weight(w_in[l])
        z, ft = _inproj(h, g_mix[l].reshape(1, D), w_main, w_tr, cs, cst, g64, g32, tm_in)
        oa, oc = _swa_pool(z, ft, bias_at, sinks[l], _pool_weight(w_pool[l]), pool_scale[l].reshape(1, 256), B, nb)
        ob = _diff_attn(ft, z.reshape(B, L, D_ROW), bias_bt, diff_lambda[l],
                        jnp.tile(g_diff_out[l], 2).reshape(BLK, 1), lam_init, B, nb, kb_d)
        wo = w_out[l].astype(bf16)
        gf = g_ffn[l].reshape(1, D)
        j = l // 2
        if l % 2 == 0:
            h = _outproj_ffn(oa, ob, oc, wo, h, gf, w_ffn_gate[j].astype(bf16), w_ffn_up[j].astype(bf16),
                             w_ffn_down[j].astype(bf16), tm, tf_d)
            out = None
        else:
            assert l == depth - 1, "the expert layer must be the last layer"
            wr = jnp.pad(w_router[j], ((0, 0), (0, LANES - N_EXPERTS)))
            wrh = wr.astype(bf16)
            wrl = (wr - wrh.astype(f32)).astype(bf16)
            tri = jnp.asarray(np.tril(np.ones((tm // 2, tm // 2), np.float32), -1), dtype=bf16)
            h, xn, info, cnt = _outproj_router(oa, ob, oc, wo, h, gf, tm, (wrh, wrl, tri), nb)

            tab, lp_rows, aux, ends = _routing_tables(info, cnt, B, nb, tm, tme)
            rows_max = B * (seq + N_META) * 2 + (n // tm) * N_EXPERTS * (ROW_ALIGN - 1) + N_EXPERTS * (tme - 1)
            T = -(-rows_max // tme)
            starts = jnp.arange(T, dtype=i32) * tme
            te = jnp.minimum(jnp.sum((ends[None, :] <= starts[:, None]).astype(i32), axis=1), N_EXPERTS - 1)
            tv = (starts < ends[-1]).astype(i32)

            xs = _dispatch(xn, tab, lp_rows, jnp.zeros((T * tme, D), bf16), tm)
            ys = _experts(xs, te, tv, w_exp_gate[j].astype(bf16), w_exp_up[j].astype(bf16),
                          w_exp_down[j].astype(bf16), tme, tf_e, 256)
            out = _combine(h, aux, tab, ys, B, nb, tm)
    if out is None:
        out = _strip_meta(h, B, nb)
    return out.reshape(B, seq, D)
```
